```python
import math
import jax
import jax.numpy as jnp
from jax import lax
import numpy as np

D_MODEL = 2048
BATCH = 4
SEQ = 4096
DEPTH = 1
DEC_BATCH = 2
DEC_SEQ = 4096
PAST_LEN = 128

ATTN_HEADS = 16
ATTN_KV_HEADS = 4
ATTN_HEAD_DIM = 64
ATTN_WINDOW = 128
ATTN_BLOCK = 128
ROPE_THETA = 500000.0
ROPE_DIM = ATTN_HEAD_DIM // 4
DN_QK_HEADS = 4
DN_V_HEADS = 8
DN_K_DIM = 128
DN_V_DIM = 128
DN_CONV = 5
DN_CHUNK = 64
N_EXPERTS = 32
TOP_K = 4
D_FF = D_MODEL
SWIGLU_LIMIT = 7.0
SWIGLU_ALPHA = 1.702
MOE_BLOCK = 128
NORM_EPS = 1e-6
NEG_INF = -1e30

ATTN_Q_W = ATTN_HEADS * ATTN_HEAD_DIM
ATTN_KV_W = ATTN_KV_HEADS * ATTN_HEAD_DIM
DN_QK_W = DN_QK_HEADS * DN_K_DIM
DN_V_W = DN_V_HEADS * DN_V_DIM
DN_CONV_W = 2 * DN_QK_W + DN_V_W
IN_SIZES = (ATTN_Q_W, ATTN_KV_W, ATTN_KV_W, DN_CONV_W, DN_V_W, 2 * DN_V_HEADS, 2 * DN_V_HEADS, D_MODEL, D_MODEL)
IN_W = sum(IN_SIZES)

kernel_name = 'hybrid_window_gqa_deltanet_moe_encoder'


def rms_norm(x, w):
    xf = x.astype(jnp.float32)
    y = xf * lax.rsqrt(jnp.mean(xf * xf, axis=-1, keepdims=True) + NORM_EPS)
    return (y * w.astype(jnp.float32)).astype(x.dtype)


def split_columns(p, sizes):
    outs, off = [], 0
    for s in sizes:
        outs.append(p[..., off:off + s])
        off += s
    return outs


def rope_tables(seq_len):
    pos = jnp.arange(seq_len, dtype=jnp.float32)
    inv_freq = 1.0 / (ROPE_THETA ** (jnp.arange(0, ROPE_DIM, 2, dtype=jnp.float32) / ROPE_DIM))
    ang = pos[:, None] * inv_freq[None, :]
    return jnp.cos(ang), jnp.sin(ang)


def partial_rope(x, cos, sin):
    half = ROPE_DIM // 2
    xr = x[..., :ROPE_DIM].astype(jnp.float32)
    x1, x2 = xr[..., :half], xr[..., half:]
    c, s = cos[None, :, None, :], sin[None, :, None, :]
    rot = jnp.concatenate([x1 * c - x2 * s, x2 * c + x1 * s], axis=-1).astype(x.dtype)
    return jnp.concatenate([rot, x[..., ROPE_DIM:]], axis=-1)


def banded_window_attention(q, k, v, sinks):
    B, S, HA, HD = q.shape
    W = ATTN_BLOCK
    NB = S // W
    G = HA // ATTN_KV_HEADS
    qb = q.reshape(B, NB, W, ATTN_KV_HEADS, G, HD)
    pad = ((0, 0), (W, W), (0, 0), (0, 0))
    kp, vp = jnp.pad(k, pad), jnp.pad(v, pad)

    def band(t):
        return jnp.concatenate([t[:, i * W:i * W + S].reshape(B, NB, W, ATTN_KV_HEADS, HD) for i in range(3)], axis=2)

    kb, vb = band(kp), band(vp)
    s = jnp.einsum('bnqkgd,bnckd->bnkgqc', qb, kb, preferred_element_type=jnp.float32) * (HD ** -0.5)
    qpos = jnp.arange(NB)[:, None] * W + jnp.arange(W)[None, :]
    kpos = jnp.arange(NB)[:, None] * W - W + jnp.arange(3 * W)[None, :]
    valid = ((jnp.abs(qpos[:, :, None] - kpos[:, None, :]) <= ATTN_WINDOW)
             & (kpos[:, None, :] >= 0) & (kpos[:, None, :] < S))
    s = jnp.where(valid[None, :, None, None], s, NEG_INF)
    sink = sinks.astype(jnp.float32).reshape(ATTN_KV_HEADS, G)[None, None, :, :, None, None]
    m = jnp.maximum(jnp.max(s, axis=-1, keepdims=True), sink)
    p = jnp.exp(s - m)
    denom = jnp.sum(p, axis=-1, keepdims=True) + jnp.exp(sink - m)
    p = (p / denom).astype(v.dtype)
    o = jnp.einsum('bnkgqc,bnckd->bnqkgd', p, vb)
    return o.reshape(B, S, HA * HD)


def window_attention_branch(a_q, a_k, a_v, sinks, cos, sin):
    B, S, _ = a_q.shape
    q = partial_rope(a_q.reshape(B, S, ATTN_HEADS, ATTN_HEAD_DIM), cos, sin)
    k = partial_rope(a_k.reshape(B, S, ATTN_KV_HEADS, ATTN_HEAD_DIM), cos, sin)
    v = a_v.reshape(B, S, ATTN_KV_HEADS, ATTN_HEAD_DIM)
    return banded_window_attention(q, k, v, sinks)


def l2_normalize(x):
    return x * lax.rsqrt(jnp.sum(x * x, axis=-1, keepdims=True) + NORM_EPS)


def chunk_gated_delta_rule(q, k, v, g, beta):
    B, H, L, DK = q.shape
    DV = v.shape[-1]
    C = DN_CHUNK
    N = L // C
    q = (q * (DK ** -0.5)).reshape(B, H, N, C, DK)
    k = k.reshape(B, H, N, C, DK)
    v = v.reshape(B, H, N, C, DV)
    g = jnp.cumsum(g.reshape(B, H, N, C), axis=-1)
    beta = beta.reshape(B, H, N, C, 1)
    incl = jnp.tril(jnp.ones((C, C), dtype=bool))
    strict = jnp.tril(jnp.ones((C, C), dtype=bool), -1)
    diff = g[..., :, None] - g[..., None, :]
    decay = jnp.where(incl, jnp.exp(jnp.where(incl, diff, 0.0)), 0.0)
    k_beta = k * beta
    kk = jnp.einsum('bhnid,bhnjd->bhnij', k_beta, k)
    a_mat = jnp.eye(C, dtype=q.dtype) + jnp.where(strict, kk * decay, 0.0)
    rhs = jnp.concatenate([v * beta, k_beta * jnp.exp(g)[..., None]], axis=-1)
    sol = lax.linalg.triangular_solve(a_mat, rhs, left_side=True, lower=True, unit_diagonal=True)
    u, w = sol[..., :DV], sol[..., DV:]
    a_qk = jnp.einsum('bhnid,bhnjd->bhnij', q, k) * decay
    g_last = g[..., -1]
    q_dec = q * jnp.exp(g)[..., None]
    k_dec = k * jnp.exp(g_last[..., None] - g)[..., None]
    xs = tuple(jnp.moveaxis(t, 2, 0) for t in (q_dec, k_dec, u, w, a_qk, g_last))

    def step(state, inp):
        qd, kd, uc, wc, aqk, gl = inp
        v_new = uc - jnp.einsum('bhcd,bhde->bhce', wc, state)
        o = jnp.einsum('bhcd,bhde->bhce', qd, state) + jnp.einsum('bhij,bhje->bhie', aqk, v_new)
        state = state * jnp.exp(gl)[..., None, None] + jnp.einsum('bhcd,bhce->bhde', kd, v_new)
        return state, o

    state0 = jnp.zeros((B, H, DK, DV), q.dtype)
    _, o = lax.scan(step, state0, xs)
    return jnp.moveaxis(o, 0, 2).reshape(B, H, L, DV)


def deltanet_branch(dn_qkv, dn_z, dn_b, dn_a, conv_w, a_log, dt_bias, norm_w):
    B, S, _ = dn_qkv.shape
    f32 = jnp.float32
    half = DN_CONV // 2
    xc = lax.conv_general_dilated(dn_qkv, conv_w[:, None, :], window_strides=(1,), padding=[(half, half)],
                                  dimension_numbers=('NWC', 'WIO', 'NWC'), feature_group_count=DN_CONV_W)
    xc = jax.nn.silu(xc).astype(f32)
    q = l2_normalize(xc[..., :DN_QK_W].reshape(B, S, DN_QK_HEADS, DN_K_DIM))
    k = l2_normalize(xc[..., DN_QK_W:2 * DN_QK_W].reshape(B, S, DN_QK_HEADS, DN_K_DIM))
    v = xc[..., 2 * DN_QK_W:].reshape(B, S, DN_V_HEADS, DN_V_DIM)
    rep = DN_V_HEADS // DN_QK_HEADS
    q = jnp.transpose(jnp.repeat(q, rep, axis=2), (0, 2, 1, 3))
    k = jnp.transpose(jnp.repeat(k, rep, axis=2), (0, 2, 1, 3))
    v = jnp.transpose(v, (0, 2, 1, 3))
    beta = jax.nn.sigmoid(dn_b.astype(f32)).reshape(B, S, 2, DN_V_HEADS)
    g = -jnp.exp(a_log.astype(f32)) * jax.nn.softplus(dn_a.astype(f32).reshape(B, S, 2, DN_V_HEADS) + dt_bias.astype(f32))
    g_f, g_b = jnp.transpose(g[:, :, 0], (0, 2, 1)), jnp.transpose(g[:, :, 1], (0, 2, 1))
    be_f, be_b = jnp.transpose(beta[:, :, 0], (0, 2, 1)), jnp.transpose(beta[:, :, 1], (0, 2, 1))
    flip = lambda t: jnp.flip(t, axis=2)
    o_f = chunk_gated_delta_rule(q, k, v, g_f, be_f)
    o_b = flip(chunk_gated_delta_rule(flip(q), flip(k), flip(v), flip(g_b), flip(be_b)))
    o = jnp.transpose(o_f + o_b, (0, 2, 1, 3))
    o = o * lax.rsqrt(jnp.mean(o * o, axis=-1, keepdims=True) + NORM_EPS) * norm_w.astype(f32)
    o = o * jax.nn.silu(dn_z.astype(f32).reshape(B, S, DN_V_HEADS, DN_V_DIM))
    return o.reshape(B, S, DN_V_W).astype(dn_qkv.dtype)


def moe_ffn(h, router_w, router_b, w_gate_up, b_gate_up, w_down, b_down):
    B, S, D = h.shape
    T = B * S
    M = MOE_BLOCK
    f32 = jnp.float32
    ht = h.reshape(T, D)
    logits = ht.astype(f32) @ router_w.astype(f32) + router_b.astype(f32)
    top_logit, top_idx = lax.top_k(logits, TOP_K)
    top_gate = jax.nn.softmax(top_logit, axis=-1)
    n_assign = T * TOP_K
    flat_e = top_idx.reshape(-1).astype(jnp.int32)
    flat_tok = jnp.arange(n_assign, dtype=jnp.int32) // TOP_K
    flat_gate = top_gate.reshape(-1)
    order = jnp.argsort(flat_e)
    sorted_e = flat_e[order]
    counts = jnp.bincount(flat_e, length=N_EXPERTS)
    padded = ((counts + M - 1) // M) * M
    start = jnp.cumsum(counts) - counts
    padded_end = jnp.cumsum(padded)
    padded_start = padded_end - padded
    dest = padded_start[sorted_e] + jnp.arange(n_assign, dtype=jnp.int32) - start[sorted_e]
    n_slots = ((n_assign + M - 1) // M) * M + N_EXPERTS * M
    n_blocks = n_slots // M
    slot_tok = jnp.full((n_slots,), T, jnp.int32).at[dest].set(flat_tok[order])
    slot_gate = jnp.zeros((n_slots,), f32).at[dest].set(flat_gate[order])
    block_start = jnp.arange(n_blocks, dtype=jnp.int32) * M
    block_expert = jnp.minimum(jnp.sum(padded_end[None, :] <= block_start[:, None], axis=1), N_EXPERTS - 1)
    h_ext = jnp.concatenate([ht, jnp.zeros((1, D), ht.dtype)], axis=0)
    xs = h_ext[slot_tok].reshape(n_blocks, M, D)

    def expert_block(args):
        xb, e = args
        gu = xb @ w_gate_up[e] + b_gate_up[e]
        gate = jnp.minimum(gu[:, ::2], SWIGLU_LIMIT)
        up = jnp.clip(gu[:, 1::2], -SWIGLU_LIMIT, SWIGLU_LIMIT)
        act = (up + 1.0) * (gate * jax.nn.sigmoid(SWIGLU_ALPHA * gate))
        return act @ w_down[e] + b_down[e]

    ys = lax.map(expert_block, (xs, block_expert)).reshape(n_slots, D)
    ys = ys * slot_gate[:, None].astype(ys.dtype)
    out = jnp.zeros((T + 1, D), ys.dtype).at[slot_tok].add(ys)[:T]
    return out.reshape(B, S, D).astype(h.dtype)


def trunk(x, norm1_w, w_in, b_in, attn_sinks, dn_conv_w, dn_a_log, dn_dt_bias, dn_norm_w,
          w_attn_o, w_dn_o, w_out, norm2_w, router_w, router_b, w_gate_up, b_gate_up,
          w_down, b_down, final_norm_w):
    cos, sin = rope_tables(x.shape[1])
    for l in range(DEPTH):
        h = rms_norm(x, norm1_w[l])
        proj = h @ w_in[l] + b_in[l]
        a_q, a_k, a_v, dn_qkv, dn_z, dn_b, dn_a, gate_a, gate_d = split_columns(proj, IN_SIZES)
        o_attn = window_attention_branch(a_q, a_k, a_v, attn_sinks[l], cos, sin) @ w_attn_o[l]
        o_dn = deltanet_branch(dn_qkv, dn_z, dn_b, dn_a, dn_conv_w[l], dn_a_log[l], dn_dt_bias[l], dn_norm_w[l]) @ w_dn_o[l]
        merged = jax.nn.sigmoid(gate_a) * o_attn + jax.nn.sigmoid(gate_d) * o_dn
        x = x + merged @ w_out[l]
        h = rms_norm(x, norm2_w[l])
        x = x + moe_ffn(h, router_w[l], router_b[l], w_gate_up[l], b_gate_up[l], w_down[l], b_down[l])
    return rms_norm(x, final_norm_w)


def setup_inputs(seed: int = 0) -> dict:
    key = jax.random.key(seed)
    ks = jax.random.split(key, 24)
    f32 = jnp.float32
    L = DEPTH

    def nrm(k, shape, scale):
        return jax.random.normal(k, shape, f32) * scale

    def gain(k, shape):
        return 1.0 + 0.02 * jax.random.normal(k, shape, f32)

    dt = jnp.exp(jax.random.uniform(ks[8], (L, 2, DN_V_HEADS), f32, math.log(1e-3), math.log(1e-1)))
    return {
        'x_prompt': nrm(ks[0], (BATCH, SEQ, D_MODEL), 1.0),
        'x_sample': nrm(ks[1], (DEC_BATCH, DEC_SEQ, D_MODEL), 1.0),
        'norm1_w': gain(ks[2], (L, D_MODEL)),
        'w_in': nrm(ks[3], (L, D_MODEL, IN_W), D_MODEL ** -0.5),
        'b_in': nrm(ks[4], (L, IN_W), 0.02),
        'attn_sinks': nrm(ks[5], (L, ATTN_HEADS), 0.5),
        'dn_conv_w': nrm(ks[6], (L, DN_CONV, DN_CONV_W), DN_CONV ** -0.5),
        'dn_a_log': jnp.log(jax.random.uniform(ks[7], (L, 2, DN_V_HEADS), f32, 1.0, 16.0)),
        'dn_dt_bias': dt + jnp.log(-jnp.expm1(-dt)),
        'dn_norm_w': gain(ks[9], (L, DN_V_DIM)),
        'w_attn_o': nrm(ks[10], (L, ATTN_Q_W, D_MODEL), ATTN_Q_W ** -0.5),
        'w_dn_o': nrm(ks[11], (L, DN_V_W, D_MODEL), DN_V_W ** -0.5),
        'w_out': nrm(ks[12], (L, D_MODEL, D_MODEL), D_MODEL ** -0.5),
        'norm2_w': gain(ks[13], (L, D_MODEL)),
        'router_w': nrm(ks[14], (L, D_MODEL, N_EXPERTS), D_MODEL ** -0.5),
        'router_b': nrm(ks[15], (L, N_EXPERTS), 0.01),
        'w_gate_up': nrm(ks[16], (L, N_EXPERTS, D_MODEL, 2 * D_FF), D_MODEL ** -0.5),
        'b_gate_up': nrm(ks[17], (L, N_EXPERTS, 2 * D_FF), 0.02),
        'w_down': nrm(ks[18], (L, N_EXPERTS, D_FF, D_MODEL), D_FF ** -0.5),
        'b_down': nrm(ks[19], (L, N_EXPERTS, D_MODEL), 0.02),
        'final_norm_w': gain(ks[20], (D_MODEL,)),
    }


def reference(x_prompt, x_sample, norm1_w, w_in, b_in, attn_sinks, dn_conv_w, dn_a_log, dn_dt_bias,
              dn_norm_w, w_attn_o, w_dn_o, w_out, norm2_w, router_w, router_b, w_gate_up, b_gate_up,
              w_down, b_down, final_norm_w):
    y_prompt = trunk(x_prompt, norm1_w, w_in, b_in, attn_sinks, dn_conv_w, dn_a_log, dn_dt_bias, dn_norm_w,
                     w_attn_o, w_dn_o, w_out, norm2_w, router_w, router_b, w_gate_up, b_gate_up,
                     w_down, b_down, final_norm_w)
    y_sample = trunk(x_sample, norm1_w, w_in, b_in, attn_sinks, dn_conv_w, dn_a_log, dn_dt_bias, dn_norm_w,
                     w_attn_o, w_dn_o, w_out, norm2_w, router_w, router_b, w_gate_up, b_gate_up,
                     w_down, b_down, final_norm_w)
    return (y_prompt, y_sample)
```

```python
import functools
import math

import jax
import jax.numpy as jnp
from jax import lax
from jax.experimental import pallas as pl
from jax.experimental.pallas import tpu as pltpu

f32 = jnp.float32
bf16 = jnp.bfloat16
u32 = jnp.uint32
i32 = jnp.int32

D_MODEL = 2048
ATTN_HEADS = 16
ATTN_KV_HEADS = 4
ATTN_HEAD_DIM = 64
ATTN_BLOCK = 128
ROPE_THETA = 500000.0
ROPE_DIM = ATTN_HEAD_DIM // 4
DN_QK_HEADS = 4
DN_V_HEADS = 8
DN_K_DIM = 128
DN_V_DIM = 128
DN_CONV = 5
DN_CHUNK = 64
N_EXPERTS = 32
TOP_K = 4
D_FF = D_MODEL
SWIGLU_LIMIT = 7.0
SWIGLU_ALPHA = 1.702
NORM_EPS = 1e-6
NEG_INF = -1e30

ATTN_Q_W = ATTN_HEADS * ATTN_HEAD_DIM
ATTN_KV_W = ATTN_KV_HEADS * ATTN_HEAD_DIM
DN_QK_W = DN_QK_HEADS * DN_K_DIM
DN_V_W = DN_V_HEADS * DN_V_DIM
DN_CONV_W = 2 * DN_QK_W + DN_V_W

_OFF_AQ = 0
_OFF_AK = _OFF_AQ + ATTN_Q_W
_OFF_AV = _OFF_AK + ATTN_KV_W
_OFF_DNQKV = _OFF_AV + ATTN_KV_W
_OFF_DNZ = _OFF_DNQKV + DN_CONV_W
_OFF_DNB = _OFF_DNZ + DN_V_W
_OFF_DNA = _OFF_DNB + 2 * DN_V_HEADS
_OFF_GA = _OFF_DNA + 2 * DN_V_HEADS
_OFF_GD = _OFF_GA + D_MODEL

P_GA = 0
P_GD = P_GA + D_MODEL
P_DNQKV = P_GD + D_MODEL
P_AQ = P_DNQKV + DN_CONV_W
P_DNZ = P_AQ + ATTN_Q_W
P_AK = P_DNZ + DN_V_W
P_AV = P_AK + ATTN_KV_W
P_W = P_AV + ATTN_KV_W

LANES = 128
VMEM_LIMIT = 56 * 1024 * 1024

IN_TM = 1024
IN_TN = 512
DN_TB = 256
POST_TM = 256
MOE_TM = 512
MOE_TF = 512
GATHER_R = 512
COMB_R = 128


def _cparams(sem):
    return pltpu.CompilerParams(dimension_semantics=sem, vmem_limit_bytes=VMEM_LIMIT)


def _sigmoid(x):
    return 1.0 / (1.0 + jnp.exp(-x))


def _softplus(x):
    return jnp.maximum(x, 0.0) + jnp.log(1.0 + jnp.exp(-jnp.abs(x)))


def _pack_halves(lo, hi):
    a = lax.bitcast_convert_type(lo.astype(bf16).astype(f32), u32)
    b = lax.bitcast_convert_type(hi.astype(bf16).astype(f32), u32)
    return (a >> 16) | (b & jnp.uint32(0xFFFF0000))


def _unpack_halves(w):
    lo = lax.bitcast_convert_type(w << 16, f32)
    hi = lax.bitcast_convert_type(w & jnp.uint32(0xFFFF0000), f32)
    return lo, hi


def _inproj_kernel(x_ref, nw_ref, w_ref, b_ref, ws_ref, bs_ref, wst_ref, bst_ref,
                   o_ref, os_ref, ost_ref, h_ref):
    @pl.when(pl.program_id(1) == 0)
    def _():
        x = x_ref[...]
        ms = jnp.mean(x * x, axis=-1, keepdims=True)
        hb = (x * lax.rsqrt(ms + NORM_EPS) * nw_ref[...]).astype(bf16)
        h_ref[...] = hb
        os_ref[...] = jnp.dot(hb, ws_ref[...], preferred_element_type=f32) + bs_ref[...]
        ost_ref[...] = lax.dot_general(wst_ref[...], hb, (((1,), (1,)), ((), ())),
                                       preferred_element_type=f32) + bst_ref[...]

    acc = jnp.dot(h_ref[...], w_ref[...], preferred_element_type=f32)
    o_ref[...] = (acc + b_ref[...]).astype(bf16)


def _inproj(x, nw, w, b, ws, bs, wst, bst):
    T = x.shape[0]
    tm = min(IN_TM, T)
    grid = (T // tm, P_W // IN_TN)
    return pl.pallas_call(
        _inproj_kernel,
        grid=grid,
        in_specs=[
            pl.BlockSpec((tm, D_MODEL), lambda i, j: (i, 0)),
            pl.BlockSpec((1, D_MODEL), lambda i, j: (0, 0)),
            pl.BlockSpec((D_MODEL, IN_TN), lambda i, j: (0, j)),
            pl.BlockSpec((1, IN_TN), lambda i, j: (0, j)),
            pl.BlockSpec((D_MODEL, LANES), lambda i, j: (0, 0)),
            pl.BlockSpec((1, LANES), lambda i, j: (0, 0)),
            pl.BlockSpec((LANES, D_MODEL), lambda i, j: (0, 0)),
            pl.BlockSpec((LANES, 1), lambda i, j: (0, 0)),
        ],
        out_specs=[
            pl.BlockSpec((tm, IN_TN), lambda i, j: (i, j)),
            pl.BlockSpec((tm, LANES), lambda i, j: (i, 0)),
            pl.BlockSpec((LANES, tm), lambda i, j: (0, i)),
        ],
        out_shape=[
            jax.ShapeDtypeStruct((T, P_W), bf16),
            jax.ShapeDtypeStruct((T, LANES), f32),
            jax.ShapeDtypeStruct((LANES, T), f32),
        ],
        scratch_shapes=[pltpu.VMEM((tm, D_MODEL), bf16)],
        compiler_params=_cparams(("parallel", "arbitrary")),
        name="inproj",
    )(x, nw, w, b, ws, bs, wst, bst)


def _rope(x, cos, sin):
    lane = lax.broadcasted_iota(i32, x.shape, 1) % ATTN_HEAD_DIM
    half = ROPE_DIM // 2
    partner = jnp.where((lane >= half) & (lane < ROPE_DIM),
                        pltpu.roll(x, half, 1), pltpu.roll(x, LANES - half, 1))
    return x * cos + partner * sin


def _attn_kernel(sink_ref, q_ref, kp_ref, kc_ref, kn_ref, vp_ref, vc_ref, vn_ref,
                 cos_ref, sin_ref, o_ref, *, nb):
    n = pl.program_id(1)
    W = ATTN_BLOCK
    HD = ATTN_HEAD_DIM
    G = ATTN_HEADS // ATTN_KV_HEADS

    q0 = pl.multiple_of(n * W, W)
    cq, sq = cos_ref[pl.ds(q0, W), :], sin_ref[pl.ds(q0, W), :]
    qs = []
    for c in range(ATTN_Q_W // LANES):
        xq = q_ref[:, c * LANES:(c + 1) * LANES].astype(f32)
        qs.append((_rope(xq, cq, sq) * (HD ** -0.5)).astype(bf16))

    kblocks = (kp_ref, kc_ref, kn_ref)
    ks = [[], []]
    for bi in range(3):
        p0 = pl.multiple_of(jnp.clip(n + bi - 1, 0, nb - 1) * W, W)
        ck, sk = cos_ref[pl.ds(p0, W), :], sin_ref[pl.ds(p0, W), :]
        for c in range(ATTN_KV_W // LANES):
            xk = kblocks[bi][:, c * LANES:(c + 1) * LANES].astype(f32)
            ks[c].append(_rope(xk, ck, sk).astype(bf16))
    kcat = [jnp.concatenate(ks[c], axis=0) for c in range(2)]
    vcat = jnp.concatenate([vp_ref[...], vc_ref[...], vn_ref[...]], axis=0)

    r = lax.broadcasted_iota(i32, (W, 3 * W), 0)
    c = lax.broadcasted_iota(i32, (W, 3 * W), 1)
    valid = (c >= r) & (c <= r + 2 * W)
    valid = valid & ((n > 0) | (c >= W)) & ((n < nb - 1) | (c < 2 * W))
    valid4 = jnp.concatenate([valid] * G, axis=0)

    for kh in range(ATTN_KV_HEADS):
        kc_, ko = divmod(kh * HD, LANES)
        k_h = kcat[kc_][:, ko:ko + HD]
        v_h = vcat[:, kh * HD:(kh + 1) * HD]
        parts = []
        sink_parts = []
        for g in range(G):
            h = kh * G + g
            qc, qo = divmod(h * HD, LANES)
            parts.append(qs[qc][:, qo:qo + HD])
            sink_parts.append(jnp.full((W, 1), sink_ref[h], f32))
        qg = jnp.concatenate(parts, axis=0)
        sink = jnp.concatenate(sink_parts, axis=0)
        s = lax.dot_general(qg, k_h, (((1,), (1,)), ((), ())), preferred_element_type=f32)
        s = jnp.where(valid4, s, NEG_INF)
        m = jnp.maximum(jnp.max(s, axis=-1, keepdims=True), sink)
        p = jnp.exp(s - m)
        denom = jnp.sum(p, axis=-1, keepdims=True) + jnp.exp(sink - m)
        o = jnp.dot(p.astype(bf16), v_h, preferred_element_type=f32) / denom
        for g in range(G):
            h = kh * G + g
            o_ref[:, h * HD:(h + 1) * HD] = o[g * W:(g + 1) * W].astype(bf16)


def _attention(proj, sinks, cos_t, sin_t, nseq, seq):
    W = ATTN_BLOCK
    nb = seq // W
    T = nseq * seq
    qcol = P_AQ // ATTN_Q_W
    kcol = P_AK // ATTN_KV_W
    vcol = P_AV // ATTN_KV_W

    def kv_spec(col, d):
        return pl.BlockSpec(
            (W, ATTN_KV_W), lambda b, n: (b * nb + jnp.clip(n + d, 0, nb - 1), col))

    return pl.pallas_call(
        functools.partial(_attn_kernel, nb=nb),
        grid=(nseq, nb),
        in_specs=[
            pl.BlockSpec(memory_space=pltpu.SMEM),
            pl.BlockSpec((W, ATTN_Q_W), lambda b, n: (b * nb + n, qcol)),
            kv_spec(kcol, -1), kv_spec(kcol, 0), kv_spec(kcol, 1),
            kv_spec(vcol, -1), kv_spec(vcol, 0), kv_spec(vcol, 1),
            pl.BlockSpec(memory_space=pltpu.VMEM),
            pl.BlockSpec(memory_space=pltpu.VMEM),
        ],
        out_specs=pl.BlockSpec((W, ATTN_Q_W), lambda b, n: (b * nb + n, 0)),
        out_shape=jax.ShapeDtypeStruct((T, ATTN_Q_W), bf16),
        compiler_params=_cparams(("parallel", "parallel")),
        name="window_attn",
    )(sinks, proj, proj, proj, proj, proj, proj, proj, cos_t, sin_t)


def _dnconv_kernel(x_ref, w_ref, o_ref, pad_ref, *, seq, rt):
    j = pl.program_id(1)
    half = DN_CONV // 2
    cw = x_ref.shape[1]
    pad_ref[0:8, :] = jnp.zeros((8, cw), f32)
    pad_ref[seq + 8:seq + 16, :] = jnp.zeros((8, cw), f32)
    pad_ref[8:seq + 8, :] = x_ref[...].astype(f32)
    w = w_ref[...]
    qk_blocks = 2 * DN_QK_W // cw
    q_blocks = DN_QK_W // cw
    for r0 in range(0, seq, rt):
        acc = jnp.zeros((rt, cw), f32)
        for t in range(DN_CONV):
            s0 = r0 + 8 - half + t
            acc = acc + pad_ref[s0:s0 + rt, :] * w[t:t + 1, :]
        y = acc * _sigmoid(acc)
        outs = []
        for hh in range(cw // DN_K_DIM):
            seg = y[:, hh * DN_K_DIM:(hh + 1) * DN_K_DIM]
            inv = lax.rsqrt(jnp.sum(seg * seg, axis=-1, keepdims=True) + NORM_EPS)
            fac = jnp.where(j < q_blocks, inv * (DN_K_DIM ** -0.5),
                            jnp.where(j < qk_blocks, inv, jnp.ones_like(inv)))
            outs.append(seg * fac)
        o_ref[r0:r0 + rt, :] = jnp.concatenate(outs, axis=1).astype(bf16)


def _dnconv(proj, conv_w, nseq, seq):
    cw = 256
    T = nseq * seq
    col0 = P_DNQKV // cw
    rt = min(512, seq)
    return pl.pallas_call(
        functools.partial(_dnconv_kernel, seq=seq, rt=rt),
        grid=(nseq, DN_CONV_W // cw),
        in_specs=[
            pl.BlockSpec((seq, cw), lambda b, j: (b, col0 + j)),
            pl.BlockSpec((DN_CONV, cw), lambda b, j: (0, j)),
        ],
        out_specs=pl.BlockSpec((seq, cw), lambda b, j: (b, j)),
        out_shape=jax.ShapeDtypeStruct((T, DN_CONV_W), bf16),
        scratch_shapes=[pltpu.VMEM((seq + 16, cw), f32)],
        compiler_params=_cparams(("parallel", "parallel")),
        name="dn_conv",
    )(proj, conv_w)


def _unit_tri_inverse(lmat, eye, blk_i, blk_j):
    def mm(a, b):
        return jnp.dot(a.astype(bf16), b.astype(bf16), preferred_element_type=f32)

    same8 = (blk_i // 8) == (blk_j // 8)
    n1 = jnp.where(same8, -lmat, 0.0)
    p = eye + n1
    n2 = mm(n1, n1)
    p = p + mm(p, n2)
    n4 = mm(n2, n2)
    inv = p + mm(p, n4)
    for s in (8, 16, 32):
        coupling = ((blk_i // (2 * s)) == (blk_j // (2 * s))) & ((blk_i // s) != (blk_j // s))
        e = jnp.where(coupling, lmat, 0.0)
        inv = inv - mm(inv, mm(e, inv))
    return inv


def _dnprep_kernel(q_ref, k_ref, v_ref, sm_ref, smt_ref, ar_ref, dr_ref, ac_ref, dc_ref,
                   u_ref, w_ref, qd_ref, kd_ref, aqk_ref, egl_ref):
    hq = pl.program_id(1)
    C = DN_CHUNK
    TB = q_ref.shape[0]
    nch = TB // C

    sm = sm_ref[...]
    beta_full = _sigmoid(sm)
    g_full = -jnp.exp(ar_ref[...]) * _softplus(sm + dr_ref[...])
    shift = (LANES - hq * 8) % LANES
    beta_c = pltpu.roll(beta_full, shift, 1)
    g_c = pltpu.roll(g_full, shift, 1)
    smt = smt_ref[...]
    g_r = -jnp.exp(ac_ref[...]) * _softplus(smt + dc_ref[...])

    ri = lax.broadcasted_iota(i32, (TB, TB), 0)
    ci = lax.broadcasted_iota(i32, (TB, TB), 1)
    same = (ri // C) == (ci // C)
    bi = lax.broadcasted_iota(i32, (C, C), 0)
    bj = lax.broadcasted_iota(i32, (C, C), 1)
    eye = (bi == bj).astype(f32)

    hp = lax.Precision.HIGHEST
    for d in range(2):
        incl_full = same & ((ci <= ri) if d == 0 else (ci >= ri))
        tri = incl_full.astype(f32)
        gcum_c = jnp.dot(tri, g_c, precision=hp, preferred_element_type=f32)
        gcum_r = lax.dot_general(g_r, tri, (((1,), (1,)), ((), ())), precision=hp,
                                 preferred_element_type=f32)
        incl = (bj <= bi) if d == 0 else (bj >= bi)
        strict = (bj < bi) if d == 0 else (bj > bi)
        last = C - 1 if d == 0 else 0
        for c in range(nch):
            rows = slice(c * C, (c + 1) * C)
            k = k_ref[rows, :]
            q = q_ref[rows, :]
            kf = k.astype(f32)
            qf = q.astype(f32)
            kk = lax.dot_general(k, k, (((1,), (1,)), ((), ())), preferred_element_type=f32)
            qk = lax.dot_general(q, k, (((1,), (1,)), ((), ())), preferred_element_type=f32)
            for hvl in range(2):
                bl = d * 2 + hvl
                gl = 4 + d * 2 + hvl
                gcol = gcum_c[rows, gl:gl + 1]
                grow = gcum_r[gl:gl + 1, rows]
                bcol = beta_c[rows, bl:bl + 1]
                diff = jnp.where(incl, gcol - grow, 0.0)
                decay = jnp.where(incl, jnp.exp(diff), 0.0)
                lmat = jnp.where(strict, kk * bcol * decay, 0.0)
                inv = _unit_tri_inverse(lmat, eye, bi, bj)
                eg = jnp.exp(gcol)
                v = v_ref[rows, hvl * DN_V_DIM:(hvl + 1) * DN_V_DIM].astype(f32)
                kb = kf * bcol
                rhs = jnp.concatenate([v * bcol, kb * eg], axis=1).astype(bf16)
                sol = jnp.dot(inv.astype(bf16), rhs, preferred_element_type=f32)
                glast = gcol[last:last + 1, :]
                cols = slice(hvl * DN_V_DIM, (hvl + 1) * DN_V_DIM)
                u_ref[d, rows, cols] = sol[:, :DN_V_DIM].astype(bf16)
                w_ref[d, rows, cols] = sol[:, DN_V_DIM:].astype(bf16)
                qd_ref[d, rows, cols] = (qf * eg).astype(bf16)
                kd_ref[d, rows, cols] = (kf * jnp.exp(glast - gcol)).astype(bf16)
                aqk_ref[d, rows, hvl * C:(hvl + 1) * C] = (qk * decay).astype(bf16)
                egl_ref[d, 0, c, hvl:hvl + 1, :] = jnp.broadcast_to(jnp.exp(glast), (1, LANES))


def _dnprep(qkv, small, small_t, ar, dr, ac, dc):
    T = qkv.shape[0]
    TB = DN_TB
    nblk = T // TB
    nch = TB // DN_CHUNK
    kcol = DN_QK_W // DN_K_DIM
    vcol = 2 * DN_QK_W // (2 * DN_V_DIM)
    wide = lambda: pl.BlockSpec((2, TB, 2 * DN_V_DIM), lambda i, h: (0, i, h))
    big = jax.ShapeDtypeStruct((2, T, DN_V_W), bf16)
    return pl.pallas_call(
        _dnprep_kernel,
        grid=(nblk, DN_QK_HEADS),
        in_specs=[
            pl.BlockSpec((TB, DN_K_DIM), lambda i, h: (i, h)),
            pl.BlockSpec((TB, DN_K_DIM), lambda i, h: (i, kcol + h)),
            pl.BlockSpec((TB, 2 * DN_V_DIM), lambda i, h: (i, vcol + h)),
            pl.BlockSpec((TB, LANES), lambda i, h: (i, 0)),
            pl.BlockSpec((8, TB), lambda i, h: (h, i)),
            pl.BlockSpec((1, LANES), lambda i, h: (0, 0)),
            pl.BlockSpec((1, LANES), lambda i, h: (0, 0)),
            pl.BlockSpec((8, 1), lambda i, h: (h, 0)),
            pl.BlockSpec((8, 1), lambda i, h: (h, 0)),
        ],
        out_specs=[
            wide(), wide(), wide(), wide(),
            pl.BlockSpec((2, TB, 2 * DN_CHUNK), lambda i, h: (0, i, h)),
            pl.BlockSpec((2, 1, nch, 2, LANES), lambda i, h: (0, h, i, 0, 0)),
        ],
        out_shape=[
            big, big, big, big,
            jax.ShapeDtypeStruct((2, T, DN_QK_HEADS * 2 * DN_CHUNK), bf16),
            jax.ShapeDtypeStruct((2, DN_QK_HEADS, T // DN_CHUNK, 2, LANES), f32),
        ],
        compiler_params=_cparams(("parallel", "parallel")),
        name="dn_prep",
    )(qkv, qkv, qkv, small, small_t, ar, dr, ac, dc)


def _dnscan_kernel(uf, wf, qf, kf, af, ef, ub, wb, qb, kb, ab, eb, of_ref, ob_ref, s_ref):
    t = pl.program_id(2)
    C = DN_CHUNK
    nch = uf.shape[1] // C

    @pl.when(t == 0)
    def _():
        s_ref[...] = jnp.zeros(s_ref.shape, f32)

    dirs = ((uf, wf, qf, kf, af, ef, of_ref), (ub, wb, qb, kb, ab, eb, ob_ref))
    for step in range(nch):
        for d in range(2):
            u_r, w_r, q_r, k_r, a_r, e_r, o_r = dirs[d]
            c = step if d == 0 else nch - 1 - step
            rows = slice(c * C, (c + 1) * C)
            for hvl in range(2):
                cols = slice(hvl * DN_V_DIM, (hvl + 1) * DN_V_DIM)
                si = d * 2 + hvl
                s = s_ref[si]
                sb = s.astype(bf16)
                v_new = u_r[0, rows, cols].astype(f32) - jnp.dot(
                    w_r[0, rows, cols], sb, preferred_element_type=f32)
                vb = v_new.astype(bf16)
                o = jnp.dot(q_r[0, rows, cols], sb, preferred_element_type=f32) + jnp.dot(
                    a_r[0, rows, hvl * C:(hvl + 1) * C], vb, preferred_element_type=f32)
                upd = lax.dot_general(k_r[0, rows, cols], vb, (((0,), (0,)), ((), ())),
                                      preferred_element_type=f32)
                s_ref[si] = s * e_r[0, 0, c, hvl:hvl + 1, :] + upd
                o_r[rows, cols] = o


def _dnscan(u, w, qd, kd, aqk, egl, nseq, seq):
    TB = DN_TB
    nt = seq // TB
    nch = TB // DN_CHUNK
    T = nseq * seq

    def specs(d):
        if d == 0:
            blk = lambda b, h, t: b * nt + t
        else:
            blk = lambda b, h, t: b * nt + nt - 1 - t
        wide = lambda: pl.BlockSpec((1, TB, 2 * DN_V_DIM), lambda b, h, t: (d, blk(b, h, t), h))
        return [wide(), wide(), wide(), wide(),
                pl.BlockSpec((1, TB, 2 * DN_CHUNK), lambda b, h, t: (d, blk(b, h, t), h)),
                pl.BlockSpec((1, 1, nch, 2, LANES), lambda b, h, t: (d, h, blk(b, h, t), 0, 0))]

    def ospec(d):
        if d == 0:
            return pl.BlockSpec((TB, 2 * DN_V_DIM), lambda b, h, t: (b * nt + t, h))
        return pl.BlockSpec((TB, 2 * DN_V_DIM), lambda b, h, t: (b * nt + nt - 1 - t, h))

    args = (u, w, qd, kd, aqk, egl)
    return pl.pallas_call(
        _dnscan_kernel,
        grid=(nseq, DN_QK_HEADS, nt),
        in_specs=specs(0) + specs(1),
        out_specs=[ospec(0), ospec(1)],
        out_shape=[jax.ShapeDtypeStruct((T, DN_V_W), f32)] * 2,
        scratch_shapes=[pltpu.VMEM((4, DN_K_DIM, DN_V_DIM), f32)],
        compiler_params=_cparams(("parallel", "parallel", "arbitrary")),
        name="dn_scan",
    )(*args, *args)


def _post_kernel(x_ref, oa_ref, of_ref, ob_ref, z_ref, ga_ref, gd_ref, dnw_ref, wa_ref, wd_ref,
                 wo_ref, n2_ref, rw_ref, rb_ref, x2_ref, hp_ref, idx_ref, gate_ref):
    od = of_ref[...] + ob_ref[...]
    z = z_ref[...].astype(f32)
    parts = []
    for h in range(DN_V_HEADS):
        cols = slice(h * DN_V_DIM, (h + 1) * DN_V_DIM)
        seg = od[:, cols]
        seg = seg * lax.rsqrt(jnp.mean(seg * seg, axis=-1, keepdims=True) + NORM_EPS) * dnw_ref[...]
        zz = z[:, cols]
        parts.append((seg * (zz * _sigmoid(zz))).astype(bf16))
    odn = jnp.concatenate(parts, axis=1)
    ya = jnp.dot(oa_ref[...], wa_ref[...], preferred_element_type=f32)
    yd = jnp.dot(odn, wd_ref[...], preferred_element_type=f32)
    merged = _sigmoid(ga_ref[...].astype(f32)) * ya + _sigmoid(gd_ref[...].astype(f32)) * yd
    x2 = x_ref[...] + jnp.dot(merged.astype(bf16), wo_ref[...], preferred_element_type=f32)
    x2_ref[...] = x2
    h2 = x2 * lax.rsqrt(jnp.mean(x2 * x2, axis=-1, keepdims=True) + NORM_EPS) * n2_ref[...]
    half = D_MODEL // 2
    hp_ref[...] = _pack_halves(h2[:, :half], h2[:, half:])

    logits = jnp.dot(h2, rw_ref[...], precision=lax.Precision.HIGHEST,
                     preferred_element_type=f32) + rb_ref[...]
    lane = lax.broadcasted_iota(i32, logits.shape, 1).astype(f32)
    vals, idxs = [], []
    cur = logits
    for _ in range(TOP_K):
        m = jnp.max(cur, axis=-1, keepdims=True)
        idx = jnp.min(jnp.where(cur == m, lane, float(LANES)), axis=-1, keepdims=True)
        vals.append(m)
        idxs.append(idx)
        cur = jnp.where(lane == idx, -jnp.inf, cur)
    exps = [jnp.exp(v - vals[0]) for v in vals]
    tot = exps[0] + exps[1] + exps[2] + exps[3]
    idx_out = jnp.zeros(logits.shape, f32)
    gate_out = jnp.zeros(logits.shape, f32)
    for kk in range(TOP_K):
        idx_out = jnp.where(lane == float(kk), idxs[kk], idx_out)
        gate_out = jnp.where(lane == float(kk), exps[kk] / tot, gate_out)
    idx_ref[...] = idx_out.astype(i32)
    gate_ref[...] = gate_out


def _post(x, o_attn, o_f, o_b, proj, dnw, wa, wd, wo, n2, rw, rb):
    T = x.shape[0]
    tm = min(POST_TM, T)
    zcol = P_DNZ // DN_V_W
    const = lambda shape: pl.BlockSpec(shape, lambda i: (0, 0))
    row = lambda width, col=0: pl.BlockSpec((tm, width), lambda i: (i, col))
    return pl.pallas_call(
        _post_kernel,
        grid=(T // tm,),
        in_specs=[
            row(D_MODEL), row(ATTN_Q_W), row(DN_V_W), row(DN_V_W),
            row(DN_V_W, zcol), row(D_MODEL, P_GA // D_MODEL), row(D_MODEL, P_GD // D_MODEL),
            const((1, DN_V_DIM)), const((ATTN_Q_W, D_MODEL)), const((DN_V_W, D_MODEL)),
            const((D_MODEL, D_MODEL)), const((1, D_MODEL)), const((D_MODEL, LANES)),
            const((1, LANES)),
        ],
        out_specs=[row(D_MODEL), row(D_MODEL // 2), row(LANES), row(LANES)],
        out_shape=[
            jax.ShapeDtypeStruct((T, D_MODEL), f32),
            jax.ShapeDtypeStruct((T, D_MODEL // 2), u32),
            jax.ShapeDtypeStruct((T, LANES), i32),
            jax.ShapeDtypeStruct((T, LANES), f32),
        ],
        compiler_params=_cparams(("parallel",)),
        name="post_mix",
    )(x, o_attn, o_f, o_b, proj, proj, proj, dnw, wa, wd, wo, n2, rw, rb)


def _gather_kernel(tok_ref, h_ref, o_ref, sem):
    R = o_ref.shape[0]

    def body(r, carry):
        t = tok_ref[0, 0, r]
        pltpu.make_async_copy(h_ref.at[pl.ds(t, 1)], o_ref.at[pl.ds(r, 1)], sem).start()
        return carry

    lax.fori_loop(0, R, body, 0)
    pltpu.make_async_copy(h_ref.at[pl.ds(0, R)], o_ref, sem).wait()


def _gather_rows(slot_tok, hp, n_slots):
    R = GATHER_R
    width = hp.shape[1]
    return pl.pallas_call(
        _gather_kernel,
        grid=(n_slots // R,),
        in_specs=[
            pl.BlockSpec((1, 1, R), lambda i: (i, 0, 0), memory_space=pltpu.SMEM),
            pl.BlockSpec(memory_space=pl.ANY),
        ],
        out_specs=pl.BlockSpec((R, width), lambda i: (i, 0)),
        out_shape=jax.ShapeDtypeStruct((n_slots, width), u32),
        scratch_shapes=[pltpu.SemaphoreType.DMA(())],
        compiler_params=_cparams(("arbitrary",)),
        name="moe_gather",
    )(slot_tok.reshape(n_slots // R, 1, R), hp)


def _moe_kernel(te_ref, nu_ref, x_ref, wg_ref, wu_ref, bg_ref, bu_ref, wd_ref, bd_ref,
                o_ref, xb_ref, acc_ref):
    i = pl.program_id(0)
    f = pl.program_id(1)
    nf = pl.num_programs(1)
    half = D_MODEL // 2
    used = i < nu_ref[0]

    @pl.when(used & (f == 0))
    def _():
        lo, hi = _unpack_halves(x_ref[...])
        xb_ref[:, :half] = lo.astype(bf16)
        xb_ref[:, half:] = hi.astype(bf16)

    @pl.when(used)
    def _():
        xb = xb_ref[...]
        g = jnp.dot(xb, wg_ref[0], preferred_element_type=f32) + bg_ref[0]
        u = jnp.dot(xb, wu_ref[0], preferred_element_type=f32) + bu_ref[0]
        gate = jnp.minimum(g, SWIGLU_LIMIT)
        up = jnp.clip(u, -SWIGLU_LIMIT, SWIGLU_LIMIT)
        act = (up + 1.0) * (gate * _sigmoid(SWIGLU_ALPHA * gate))
        part = jnp.dot(act.astype(bf16), wd_ref[0], preferred_element_type=f32)

        @pl.when(f == 0)
        def _():
            acc_ref[...] = part

        @pl.when(f > 0)
        def _():
            acc_ref[...] += part

    @pl.when(used & (f == nf - 1))
    def _():
        y = acc_ref[...] + bd_ref[0]
        o_ref[...] = _pack_halves(y[:, :half], y[:, half:])

    @pl.when(jnp.logical_not(used) & (f == nf - 1))
    def _():
        o_ref[...] = jnp.zeros(o_ref.shape, u32)


def _moe(tile_expert, n_used, xs, wg, wu, bg, bu, wd, bd):
    n_slots = xs.shape[0]
    tm, tf = MOE_TM, MOE_TF
    n_tiles = n_slots // tm
    nf = D_FF // tf

    def fidx(i, f, te, nu):
        return jnp.where(i < nu[0], f, nf - 1)

    grid_spec = pltpu.PrefetchScalarGridSpec(
        num_scalar_prefetch=2,
        grid=(n_tiles, nf),
        in_specs=[
            pl.BlockSpec((tm, D_MODEL // 2), lambda i, f, te, nu: (i, 0)),
            pl.BlockSpec((1, D_MODEL, tf), lambda i, f, te, nu: (te[i], 0, fidx(i, f, te, nu))),
            pl.BlockSpec((1, D_MODEL, tf), lambda i, f, te, nu: (te[i], 0, fidx(i, f, te, nu))),
            pl.BlockSpec((1, 1, tf), lambda i, f, te, nu: (te[i], 0, fidx(i, f, te, nu))),
            pl.BlockSpec((1, 1, tf), lambda i, f, te, nu: (te[i], 0, fidx(i, f, te, nu))),
            pl.BlockSpec((1, tf, D_MODEL), lambda i, f, te, nu: (te[i], fidx(i, f, te, nu), 0)),
            pl.BlockSpec((1, 1, D_MODEL), lambda i, f, te, nu: (te[i], 0, 0)),
        ],
        out_specs=pl.BlockSpec((tm, D_MODEL // 2), lambda i, f, te, nu: (i, 0)),
        scratch_shapes=[pltpu.VMEM((tm, D_MODEL), bf16), pltpu.VMEM((tm, D_MODEL), f32)],
    )
    return pl.pallas_call(
        _moe_kernel,
        grid_spec=grid_spec,
        out_shape=jax.ShapeDtypeStruct((n_slots, D_MODEL // 2), u32),
        compiler_params=_cparams(("arbitrary", "arbitrary")),
        name="moe_experts",
    )(tile_expert, n_used, xs, wg, wu, bg, bu, wd, bd)


def _combine_kernel(pos_ref, x2_ref, gate_ref, fw_ref, ys_ref, o_ref, buf_ref, sem):
    R = x2_ref.shape[0]
    half = D_MODEL // 2

    def body(a, carry):
        p = pos_ref[0, 0, a]
        pltpu.make_async_copy(ys_ref.at[pl.ds(p, 1)], buf_ref.at[pl.ds(a, 1)], sem).start()
        return carry

    lax.fori_loop(0, TOP_K * R, body, 0)
    pltpu.make_async_copy(ys_ref.at[pl.ds(0, TOP_K * R)], buf_ref, sem).wait()

    gates = gate_ref[...]
    lo_acc = x2_ref[:, :half]
    hi_acc = x2_ref[:, half:]
    for k in range(TOP_K):
        lo, hi = _unpack_halves(buf_ref[k * R:(k + 1) * R, :])
        gk = gates[:, k:k + 1]
        lo_acc = lo_acc + gk * lo
        hi_acc = hi_acc + gk * hi
    ms = (jnp.sum(lo_acc * lo_acc, axis=-1, keepdims=True)
          + jnp.sum(hi_acc * hi_acc, axis=-1, keepdims=True)) / D_MODEL
    inv = lax.rsqrt(ms + NORM_EPS)
    o_ref[:, :half] = lo_acc * inv * fw_ref[:, :half]
    o_ref[:, half:] = hi_acc * inv * fw_ref[:, half:]


def _combine(pos, x2, gates, fw, ys):
    T = x2.shape[0]
    R = min(COMB_R, T)
    return pl.pallas_call(
        _combine_kernel,
        grid=(T // R,),
        in_specs=[
            pl.BlockSpec((1, 1, TOP_K * R), lambda i: (i, 0, 0), memory_space=pltpu.SMEM),
            pl.BlockSpec((R, D_MODEL), lambda i: (i, 0)),
            pl.BlockSpec((R, LANES), lambda i: (i, 0)),
            pl.BlockSpec((1, D_MODEL), lambda i: (0, 0)),
            pl.BlockSpec(memory_space=pl.ANY),
        ],
        out_specs=pl.BlockSpec((R, D_MODEL), lambda i: (i, 0)),
        out_shape=jax.ShapeDtypeStruct((T, D_MODEL), f32),
        scratch_shapes=[pltpu.VMEM((TOP_K * R, D_MODEL // 2), u32), pltpu.SemaphoreType.DMA(())],
        compiler_params=_cparams(("arbitrary",)),
        name="moe_combine",
    )(pos, x2, gates, fw, ys)


def _dn_channel_columns():
    cols = []
    for hq in range(DN_QK_HEADS):
        for kind in range(2):
            for d in range(2):
                for hvl in range(2):
                    cols.append(_OFF_DNB + kind * 2 * DN_V_HEADS + d * DN_V_HEADS + 2 * hq + hvl)
    return cols


def _routing(top_idx, T, tm):
    A = T * TOP_K
    n_tiles = A // tm + N_EXPERTS
    n_slots = n_tiles * tm
    flat_e = top_idx.reshape(-1)
    order = jnp.argsort(flat_e, stable=True).astype(i32)
    sorted_e = flat_e[order]
    counts = jnp.zeros((N_EXPERTS,), i32).at[flat_e].add(1)
    padded = ((counts + tm - 1) // tm) * tm
    start = jnp.cumsum(counts) - counts
    padded_end = jnp.cumsum(padded)
    padded_start = padded_end - padded
    dest = padded_start[sorted_e] + jnp.arange(A, dtype=i32) - start[sorted_e]
    pos = jnp.zeros((A,), i32).at[order].set(dest, unique_indices=True)
    tile_start = jnp.arange(n_tiles, dtype=i32) * tm
    tile_expert = jnp.minimum(
        jnp.sum(padded_end[None, :] <= tile_start[:, None], axis=1), N_EXPERTS - 1).astype(i32)
    slot = jnp.arange(n_slots, dtype=i32)
    slot_e = tile_expert[slot // tm]
    within = slot - padded_start[slot_e]
    src = jnp.clip(start[slot_e] + within, 0, A - 1)
    slot_tok = jnp.where(within < counts[slot_e], order[src] // TOP_K, 0).astype(i32)
    n_used = (padded_end[-1] // tm).astype(i32).reshape(1)
    return slot_tok, pos, tile_expert, n_used, n_slots


def _layer(x, nseq, seq, norm1_w, w_in, b_in, attn_sinks, dn_conv_w, dn_a_log, dn_dt_bias,
           dn_norm_w, w_attn_o, w_dn_o, w_out, norm2_w, router_w, router_b, w_gate_up, b_gate_up,
           w_down, b_down, final_norm_w):
    T = nseq * seq

    order = ([_OFF_GA, D_MODEL], [_OFF_GD, D_MODEL], [_OFF_DNQKV, DN_CONV_W], [_OFF_AQ, ATTN_Q_W],
             [_OFF_DNZ, DN_V_W], [_OFF_AK, ATTN_KV_W], [_OFF_AV, ATTN_KV_W])
    w_main = jnp.concatenate([w_in[:, o:o + n] for o, n in order], axis=1).astype(bf16)
    b_main = jnp.concatenate([b_in[o:o + n] for o, n in order])[None, :]
    ch = jnp.array(_dn_channel_columns(), i32)
    nch = ch.shape[0]
    w_small = jnp.zeros((D_MODEL, LANES), f32).at[:, :nch].set(w_in[:, ch]).astype(bf16)
    b_small = jnp.zeros((LANES,), f32).at[:nch].set(b_in[ch])
    is_a = ((jnp.arange(nch) // 4) % 2) == 1
    dsel = (jnp.arange(nch) // 2) % 2
    hsel = 2 * (jnp.arange(nch) // 8) + jnp.arange(nch) % 2
    a_ch = jnp.zeros((LANES,), f32).at[:nch].set(jnp.where(is_a, dn_a_log[dsel, hsel], 0.0))
    d_ch = jnp.zeros((LANES,), f32).at[:nch].set(jnp.where(is_a, dn_dt_bias[dsel, hsel], 0.0))

    proj, small, small_t = _inproj(x, norm1_w[None, :], w_main, b_main, w_small, b_small[None, :],
                                   w_small.T, b_small[:, None])

    pos = jnp.arange(seq, dtype=f32)
    inv_freq = 1.0 / (ROPE_THETA ** (jnp.arange(0, ROPE_DIM, 2, dtype=f32) / ROPE_DIM))
    ang = pos[:, None] * inv_freq[None, :]
    cos, sin = jnp.cos(ang), jnp.sin(ang)
    half = ROPE_DIM // 2
    ones_tail = jnp.ones((seq, ATTN_HEAD_DIM - ROPE_DIM), f32)
    cos_head = jnp.concatenate([cos, cos, ones_tail], axis=1)
    sin_head = jnp.concatenate([-sin, sin, 0.0 * ones_tail], axis=1)
    cos_t = jnp.concatenate([cos_head, cos_head], axis=1)
    sin_t = jnp.concatenate([sin_head, sin_head], axis=1)

    o_attn = _attention(proj, attn_sinks, cos_t, sin_t, nseq, seq)

    qkv = _dnconv(proj, dn_conv_w, nseq, seq)
    u, w, qd, kd, aqk, egl = _dnprep(qkv, small, small_t, a_ch[None, :], d_ch[None, :],
                                     a_ch[:, None], d_ch[:, None])
    o_f, o_b = _dnscan(u, w, qd, kd, aqk, egl, nseq, seq)

    rw = jnp.zeros((D_MODEL, LANES), f32).at[:, :N_EXPERTS].set(router_w)
    rb = jnp.full((LANES,), NEG_INF, f32).at[:N_EXPERTS].set(router_b)
    x2, hp, top_idx, top_gate = _post(
        x, o_attn, o_f, o_b, proj, dn_norm_w[None, :], w_attn_o.astype(bf16), w_dn_o.astype(bf16),
        w_out.astype(bf16), norm2_w[None, :], rw, rb[None, :])

    slot_tok, pos_a, tile_expert, n_used, n_slots = _routing(top_idx[:, :TOP_K], T, MOE_TM)
    xs = _gather_rows(slot_tok, hp, n_slots)
    wg = w_gate_up[:, :, 0::2].astype(bf16)
    wu = w_gate_up[:, :, 1::2].astype(bf16)
    bg = b_gate_up[:, None, 0::2]
    bu = b_gate_up[:, None, 1::2]
    ys = _moe(tile_expert, n_used, xs, wg, wu, bg, bu, w_down.astype(bf16), b_down[:, None, :])
    R = min(COMB_R, T)
    pos_tiles = pos_a.reshape(T // R, R, TOP_K).transpose(0, 2, 1).reshape(T // R, 1, TOP_K * R)
    return _combine(pos_tiles, x2, top_gate, final_norm_w[None, :], ys)


def kernel(x_prompt, x_sample, norm1_w, w_in, b_in, attn_sinks, dn_conv_w, dn_a_log, dn_dt_bias,
           dn_norm_w, w_attn_o, w_dn_o, w_out, norm2_w, router_w, router_b, w_gate_up, b_gate_up,
           w_down, b_down, final_norm_w):
    bp, seq, d = x_prompt.shape
    bs = x_sample.shape[0]
    assert x_sample.shape[1] == seq and d == D_MODEL
    nseq = bp + bs
    x = jnp.concatenate([x_prompt.reshape(bp * seq, d), x_sample.reshape(bs * seq, d)], axis=0)
    y = _layer(x, nseq, seq, norm1_w[0], w_in[0], b_in[0], attn_sinks[0], dn_conv_w[0], dn_a_log[0],
               dn_dt_bias[0], dn_norm_w[0], w_attn_o[0], w_dn_o[0], w_out[0], norm2_w[0],
               router_w[0], router_b[0], w_gate_up[0], b_gate_up[0], w_down[0], b_down[0],
               final_norm_w)
    return (y[:bp * seq].reshape(bp, seq, d), y[bp * seq:].reshape(bs, seq, d))
```

```python
import functools
import math

import jax
import jax.numpy as jnp
from jax import lax
from jax.experimental import pallas as pl
from jax.experimental.pallas import tpu as pltpu

f32 = jnp.float32
bf16 = jnp.bfloat16
u32 = jnp.uint32
i32 = jnp.int32

D_MODEL = 2048
ATTN_HEADS = 16
ATTN_KV_HEADS = 4
ATTN_HEAD_DIM = 64
ATTN_BLOCK = 128
ROPE_THETA = 500000.0
ROPE_DIM = ATTN_HEAD_DIM // 4
DN_QK_HEADS = 4
DN_V_HEADS = 8
DN_K_DIM = 128
DN_V_DIM = 128
DN_CONV = 5
DN_CHUNK = 64
N_EXPERTS = 32
TOP_K = 4
D_FF = D_MODEL
SWIGLU_LIMIT = 7.0
SWIGLU_ALPHA = 1.702
NORM_EPS = 1e-6
NEG_INF = -1e30

ATTN_Q_W = ATTN_HEADS * ATTN_HEAD_DIM
ATTN_KV_W = ATTN_KV_HEADS * ATTN_HEAD_DIM
DN_QK_W = DN_QK_HEADS * DN_K_DIM
DN_V_W = DN_V_HEADS * DN_V_DIM
DN_CONV_W = 2 * DN_QK_W + DN_V_W

_OFF_AQ = 0
_OFF_AK = _OFF_AQ + ATTN_Q_W
_OFF_AV = _OFF_AK + ATTN_KV_W
_OFF_DNQKV = _OFF_AV + ATTN_KV_W
_OFF_DNZ = _OFF_DNQKV + DN_CONV_W
_OFF_DNB = _OFF_DNZ + DN_V_W
_OFF_DNA = _OFF_DNB + 2 * DN_V_HEADS
_OFF_GA = _OFF_DNA + 2 * DN_V_HEADS
_OFF_GD = _OFF_GA + D_MODEL

P_GA = 0
P_GD = P_GA + D_MODEL
P_DNQKV = P_GD + D_MODEL
P_AQ = P_DNQKV + DN_CONV_W
P_DNZ = P_AQ + ATTN_Q_W
P_AK = P_DNZ + DN_V_W
P_AV = P_AK + ATTN_KV_W
P_W = P_AV + ATTN_KV_W

LANES = 128
VMEM_LIMIT = 56 * 1024 * 1024

IN_TM = 1024
IN_TN = 512
DN_TB = 256
POST_TM = 256
MOE_TM = 512
MOE_TF = 512
GATHER_R = 512
COMB_R = 128


def _cparams(sem):
    return pltpu.CompilerParams(dimension_semantics=sem, vmem_limit_bytes=VMEM_LIMIT)


def _sigmoid(x):
    return 1.0 / (1.0 + jnp.exp(-x))


def _softplus(x):
    return jnp.maximum(x, 0.0) + jnp.log(1.0 + jnp.exp(-jnp.abs(x)))


def _pack_halves(lo, hi):
    a = lax.bitcast_convert_type(lo.astype(bf16).astype(f32), u32)
    b = lax.bitcast_convert_type(hi.astype(bf16).astype(f32), u32)
    return (a >> 16) | (b & jnp.uint32(0xFFFF0000))


def _unpack_halves(w):
    lo = lax.bitcast_convert_type(w << 16, f32)
    hi = lax.bitcast_convert_type(w & jnp.uint32(0xFFFF0000), f32)
    return lo, hi


def _inproj_kernel(x_ref, nw_ref, w_ref, b_ref, ws_ref, bs_ref, wst_ref, bst_ref,
                   o_ref, os_ref, ost_ref, h_ref):
    @pl.when(pl.program_id(1) == 0)
    def _():
        x = x_ref[...]
        ms = jnp.mean(x * x, axis=-1, keepdims=True)
        hb = (x * lax.rsqrt(ms + NORM_EPS) * nw_ref[...]).astype(bf16)
        h_ref[...] = hb
        os_ref[...] = jnp.dot(hb, ws_ref[...], preferred_element_type=f32) + bs_ref[...]
        ost_ref[...] = lax.dot_general(wst_ref[...], hb, (((1,), (1,)), ((), ())),
                                       preferred_element_type=f32) + bst_ref[...]

    acc = jnp.dot(h_ref[...], w_ref[...], preferred_element_type=f32)
    o_ref[...] = (acc + b_ref[...]).astype(bf16)


def _inproj(x, nw, w, b, ws, bs, wst, bst):
    T = x.shape[0]
    tm = min(IN_TM, T)
    grid = (T // tm, P_W // IN_TN)
    return pl.pallas_call(
        _inproj_kernel,
        grid=grid,
        in_specs=[
            pl.BlockSpec((tm, D_MODEL), lambda i, j: (i, 0)),
            pl.BlockSpec((1, D_MODEL), lambda i, j: (0, 0)),
            pl.BlockSpec((D_MODEL, IN_TN), lambda i, j: (0, j)),
            pl.BlockSpec((1, IN_TN), lambda i, j: (0, j)),
            pl.BlockSpec((D_MODEL, LANES), lambda i, j: (0, 0)),
            pl.BlockSpec((1, LANES), lambda i, j: (0, 0)),
            pl.BlockSpec((LANES, D_MODEL), lambda i, j: (0, 0)),
            pl.BlockSpec((LANES, 1), lambda i, j: (0, 0)),
        ],
        out_specs=[
            pl.BlockSpec((tm, IN_TN), lambda i, j: (i, j)),
            pl.BlockSpec((tm, LANES), lambda i, j: (i, 0)),
            pl.BlockSpec((LANES, tm), lambda i, j: (0, i)),
        ],
        out_shape=[
            jax.ShapeDtypeStruct((T, P_W), bf16),
            jax.ShapeDtypeStruct((T, LANES), f32),
            jax.ShapeDtypeStruct((LANES, T), f32),
        ],
        scratch_shapes=[pltpu.VMEM((tm, D_MODEL), bf16)],
        compiler_params=_cparams(("parallel", "arbitrary")),
        name="inproj",
    )(x, nw, w, b, ws, bs, wst, bst)


def _rope(x, cos, sin):
    lane = lax.broadcasted_iota(i32, x.shape, 1) % ATTN_HEAD_DIM
    half = ROPE_DIM // 2
    partner = jnp.where((lane >= half) & (lane < ROPE_DIM),
                        pltpu.roll(x, half, 1), pltpu.roll(x, LANES - half, 1))
    return x * cos + partner * sin


def _attn_kernel(sink_ref, q_ref, kp_ref, kc_ref, kn_ref, vp_ref, vc_ref, vn_ref,
                 cos_ref, sin_ref, o_ref, *, nb):
    n = pl.program_id(1)
    W = ATTN_BLOCK
    HD = ATTN_HEAD_DIM
    G = ATTN_HEADS // ATTN_KV_HEADS

    q0 = pl.multiple_of(n * W, W)
    cq, sq = cos_ref[pl.ds(q0, W), :], sin_ref[pl.ds(q0, W), :]
    qs = []
    for c in range(ATTN_Q_W // LANES):
        xq = q_ref[:, c * LANES:(c + 1) * LANES].astype(f32)
        qs.append((_rope(xq, cq, sq) * (HD ** -0.5)).astype(bf16))

    kblocks = (kp_ref, kc_ref, kn_ref)
    ks = [[], []]
    for bi in range(3):
        p0 = pl.multiple_of(jnp.clip(n + bi - 1, 0, nb - 1) * W, W)
        ck, sk = cos_ref[pl.ds(p0, W), :], sin_ref[pl.ds(p0, W), :]
        for c in range(ATTN_KV_W // LANES):
            xk = kblocks[bi][:, c * LANES:(c + 1) * LANES].astype(f32)
            ks[c].append(_rope(xk, ck, sk).astype(bf16))
    kcat = [jnp.concatenate(ks[c], axis=0) for c in range(2)]
    vcat = jnp.concatenate([vp_ref[...], vc_ref[...], vn_ref[...]], axis=0)

    r = lax.broadcasted_iota(i32, (W, 3 * W), 0)
    c = lax.broadcasted_iota(i32, (W, 3 * W), 1)
    valid = (c >= r) & (c <= r + 2 * W)
    valid = valid & ((n > 0) | (c >= W)) & ((n < nb - 1) | (c < 2 * W))
    valid4 = jnp.concatenate([valid] * G, axis=0)

    for kh in range(ATTN_KV_HEADS):
        kc_, ko = divmod(kh * HD, LANES)
        k_h = kcat[kc_][:, ko:ko + HD]
        v_h = vcat[:, kh * HD:(kh + 1) * HD]
        parts = []
        sink_parts = []
        for g in range(G):
            h = kh * G + g
            qc, qo = divmod(h * HD, LANES)
            parts.append(qs[qc][:, qo:qo + HD])
            sink_parts.append(jnp.full((W, 1), sink_ref[h], f32))
        qg = jnp.concatenate(parts, axis=0)
        sink = jnp.concatenate(sink_parts, axis=0)
        s = lax.dot_general(qg, k_h, (((1,), (1,)), ((), ())), preferred_element_type=f32)
        s = jnp.where(valid4, s, NEG_INF)
        m = jnp.maximum(jnp.max(s, axis=-1, keepdims=True), sink)
        p = jnp.exp(s - m)
        denom = jnp.sum(p, axis=-1, keepdims=True) + jnp.exp(sink - m)
        o = jnp.dot(p.astype(bf16), v_h, preferred_element_type=f32) / denom
        for g in range(G):
            h = kh * G + g
            o_ref[:, h * HD:(h + 1) * HD] = o[g * W:(g + 1) * W].astype(bf16)


def _attention(proj, sinks, cos_t, sin_t, nseq, seq):
    W = ATTN_BLOCK
    nb = seq // W
    T = nseq * seq
    qcol = P_AQ // ATTN_Q_W
    kcol = P_AK // ATTN_KV_W
    vcol = P_AV // ATTN_KV_W

    def kv_spec(col, d):
        return pl.BlockSpec(
            (W, ATTN_KV_W), lambda b, n: (b * nb + jnp.clip(n + d, 0, nb - 1), col))

    return pl.pallas_call(
        functools.partial(_attn_kernel, nb=nb),
        grid=(nseq, nb),
        in_specs=[
            pl.BlockSpec(memory_space=pltpu.SMEM),
            pl.BlockSpec((W, ATTN_Q_W), lambda b, n: (b * nb + n, qcol)),
            kv_spec(kcol, -1), kv_spec(kcol, 0), kv_spec(kcol, 1),
            kv_spec(vcol, -1), kv_spec(vcol, 0), kv_spec(vcol, 1),
            pl.BlockSpec(memory_space=pltpu.VMEM),
            pl.BlockSpec(memory_space=pltpu.VMEM),
        ],
        out_specs=pl.BlockSpec((W, ATTN_Q_W), lambda b, n: (b * nb + n, 0)),
        out_shape=jax.ShapeDtypeStruct((T, ATTN_Q_W), bf16),
        compiler_params=_cparams(("parallel", "parallel")),
        name="window_attn",
    )(sinks, proj, proj, proj, proj, proj, proj, proj, cos_t, sin_t)


def _dnconv_kernel(x_ref, w_ref, o_ref, pad_ref, *, seq, rt):
    j = pl.program_id(1)
    half = DN_CONV // 2
    cw = x_ref.shape[1]
    pad_ref[0:8, :] = jnp.zeros((8, cw), f32)
    pad_ref[seq + 8:seq + 16, :] = jnp.zeros((8, cw), f32)
    pad_ref[8:seq + 8, :] = x_ref[...].astype(f32)
    w = w_ref[...]
    qk_blocks = 2 * DN_QK_W // cw
    q_blocks = DN_QK_W // cw
    for r0 in range(0, seq, rt):
        acc = jnp.zeros((rt, cw), f32)
        for t in range(DN_CONV):
            s0 = r0 + 8 - half + t
            acc = acc + pad_ref[s0:s0 + rt, :] * w[t:t + 1, :]
        y = acc * _sigmoid(acc)
        outs = []
        for hh in range(cw // DN_K_DIM):
            seg = y[:, hh * DN_K_DIM:(hh + 1) * DN_K_DIM]
            inv = lax.rsqrt(jnp.sum(seg * seg, axis=-1, keepdims=True) + NORM_EPS)
            fac = jnp.where(j < q_blocks, inv * (DN_K_DIM ** -0.5),
                            jnp.where(j < qk_blocks, inv, jnp.ones_like(inv)))
            outs.append(seg * fac)
        o_ref[r0:r0 + rt, :] = jnp.concatenate(outs, axis=1).astype(bf16)


def _dnconv(proj, conv_w, nseq, seq):
    cw = 256
    T = nseq * seq
    col0 = P_DNQKV // cw
    rt = min(512, seq)
    return pl.pallas_call(
        functools.partial(_dnconv_kernel, seq=seq, rt=rt),
        grid=(nseq, DN_CONV_W // cw),
        in_specs=[
            pl.BlockSpec((seq, cw), lambda b, j: (b, col0 + j)),
            pl.BlockSpec((DN_CONV, cw), lambda b, j: (0, j)),
        ],
        out_specs=pl.BlockSpec((seq, cw), lambda b, j: (b, j)),
        out_shape=jax.ShapeDtypeStruct((T, DN_CONV_W), bf16),
        scratch_shapes=[pltpu.VMEM((seq + 16, cw), f32)],
        compiler_params=_cparams(("parallel", "parallel")),
        name="dn_conv",
    )(proj, conv_w)


def _mm(a, b):
    return jnp.dot(a.astype(bf16), b.astype(bf16), preferred_element_type=f32)


def _unit_tri_inverse_all(lmats, eye, blk_i, blk_j):
    same8 = (blk_i // 8) == (blk_j // 8)
    n1 = [jnp.where(same8, -l, 0.0) for l in lmats]
    p = [eye + n for n in n1]
    n2 = [_mm(n, n) for n in n1]
    p = [pp + _mm(pp, nn) for pp, nn in zip(p, n2)]
    n4 = [_mm(n, n) for n in n2]
    inv = [pp + _mm(pp, nn) for pp, nn in zip(p, n4)]
    for s in (8, 16, 32):
        coupling = ((blk_i // (2 * s)) == (blk_j // (2 * s))) & ((blk_i // s) != (blk_j // s))
        t = [_mm(jnp.where(coupling, l, 0.0), iv) for l, iv in zip(lmats, inv)]
        inv = [iv - _mm(iv, tt) for iv, tt in zip(inv, t)]
    return inv


def _dnprep_kernel(q_ref, k_ref, v_ref, sm_ref, smt_ref, ar_ref, dr_ref, ac_ref, dc_ref,
                   u_ref, w_ref, qd_ref, kd_ref, aqk_ref, egl_ref):
    hq = pl.program_id(1)
    C = DN_CHUNK
    P = 2 * C
    TB = q_ref.shape[0]
    nch = TB // C

    sm = sm_ref[...]
    beta_full = _sigmoid(sm)
    g_full = -jnp.exp(ar_ref[...]) * _softplus(sm + dr_ref[...])
    shift = (LANES - hq * 8) % LANES
    beta_c = pltpu.roll(beta_full, shift, 1)
    g_c = pltpu.roll(g_full, shift, 1)
    smt = smt_ref[...]
    g_r = -jnp.exp(ac_ref[...]) * _softplus(smt + dc_ref[...])

    ri = lax.broadcasted_iota(i32, (TB, TB), 0)
    ci = lax.broadcasted_iota(i32, (TB, TB), 1)
    same = (ri // C) == (ci // C)
    bi = lax.broadcasted_iota(i32, (P, P), 0)
    bj = lax.broadcasted_iota(i32, (P, P), 1)
    eye = (bi == bj).astype(f32)
    same_head = (bi // C) == (bj // C)

    hp = lax.Precision.HIGHEST
    gcum_c, gcum_r = [], []
    for d in range(2):
        tri = (same & ((ci <= ri) if d == 0 else (ci >= ri))).astype(f32)
        gcum_c.append(jnp.dot(tri, g_c, precision=hp, preferred_element_type=f32))
        gcum_r.append(lax.dot_general(g_r, tri, (((1,), (1,)), ((), ())), precision=hp,
                                      preferred_element_type=f32))

    k2, q2, kk2, qk2, v2 = [], [], [], [], []
    for c in range(nch):
        rows = slice(c * C, (c + 1) * C)
        k = k_ref[rows, :]
        q = q_ref[rows, :]
        kp = jnp.concatenate([k, k], axis=0)
        qp = jnp.concatenate([q, q], axis=0)
        k2.append(kp.astype(f32))
        q2.append(qp.astype(f32))
        kk2.append(lax.dot_general(kp, kp, (((1,), (1,)), ((), ())), preferred_element_type=f32))
        qk2.append(lax.dot_general(qp, kp, (((1,), (1,)), ((), ())), preferred_element_type=f32))
        v2.append(jnp.concatenate([v_ref[rows, :DN_V_DIM], v_ref[rows, DN_V_DIM:]], axis=0).astype(f32))

    inst = [(d, c) for d in range(2) for c in range(nch)]
    gcol, bcol, decay, lmats = [], [], [], []
    for d, c in inst:
        rows = slice(c * C, (c + 1) * C)
        b0, g0 = d * 2, 4 + d * 2
        gc = jnp.concatenate([gcum_c[d][rows, g0:g0 + 1], gcum_c[d][rows, g0 + 1:g0 + 2]], axis=0)
        gr = jnp.concatenate([gcum_r[d][g0:g0 + 1, rows], gcum_r[d][g0 + 1:g0 + 2, rows]], axis=1)
        bc = jnp.concatenate([beta_c[rows, b0:b0 + 1], beta_c[rows, b0 + 1:b0 + 2]], axis=0)
        incl = same_head & ((bj <= bi) if d == 0 else (bj >= bi))
        strict = same_head & ((bj < bi) if d == 0 else (bj > bi))
        dec = jnp.where(incl, jnp.exp(jnp.where(incl, gc - gr, 0.0)), 0.0)
        gcol.append(gc)
        bcol.append(bc)
        decay.append(dec)
        lmats.append(jnp.where(strict, kk2[c] * bc * dec, 0.0))

    invs = _unit_tri_inverse_all(lmats, eye, bi, bj)

    for n, (d, c) in enumerate(inst):
        rows = slice(c * C, (c + 1) * C)
        last = C - 1 if d == 0 else 0
        gc, bc = gcol[n], bcol[n]
        eg = jnp.exp(gc)
        rhs = jnp.concatenate([v2[c] * bc, k2[c] * (bc * eg)], axis=1)
        sol = _mm(invs[n], rhs)
        gl0 = gc[last:last + 1, :]
        gl1 = gc[C + last:C + last + 1, :]
        glast = jnp.concatenate([jnp.broadcast_to(gl0, (C, 1)), jnp.broadcast_to(gl1, (C, 1))], axis=0)
        qd = q2[c] * eg
        kd = k2[c] * jnp.exp(glast - gc)
        aqk = qk2[c] * decay[n]
        for hvl in range(2):
            pr = slice(hvl * C, (hvl + 1) * C)
            cols = slice(hvl * DN_V_DIM, (hvl + 1) * DN_V_DIM)
            u_ref[d, rows, cols] = sol[pr, :DN_V_DIM].astype(bf16)
            w_ref[d, rows, cols] = sol[pr, DN_V_DIM:].astype(bf16)
            qd_ref[d, rows, cols] = qd[pr].astype(bf16)
            kd_ref[d, rows, cols] = kd[pr].astype(bf16)
            aqk_ref[d, rows, pr] = aqk[pr, pr].astype(bf16)
        egl_ref[d, 0, c, 0:1, :] = jnp.broadcast_to(jnp.exp(gl0), (1, LANES))
        egl_ref[d, 0, c, 1:2, :] = jnp.broadcast_to(jnp.exp(gl1), (1, LANES))


def _dnprep(qkv, small, small_t, ar, dr, ac, dc):
    T = qkv.shape[0]
    TB = DN_TB
    nblk = T // TB
    nch = TB // DN_CHUNK
    kcol = DN_QK_W // DN_K_DIM
    vcol = 2 * DN_QK_W // (2 * DN_V_DIM)
    wide = lambda: pl.BlockSpec((2, TB, 2 * DN_V_DIM), lambda i, h: (0, i, h))
    big = jax.ShapeDtypeStruct((2, T, DN_V_W), bf16)
    return pl.pallas_call(
        _dnprep_kernel,
        grid=(nblk, DN_QK_HEADS),
        in_specs=[
            pl.BlockSpec((TB, DN_K_DIM), lambda i, h: (i, h)),
            pl.BlockSpec((TB, DN_K_DIM), lambda i, h: (i, kcol + h)),
            pl.BlockSpec((TB, 2 * DN_V_DIM), lambda i, h: (i, vcol + h)),
            pl.BlockSpec((TB, LANES), lambda i, h: (i, 0)),
            pl.BlockSpec((8, TB), lambda i, h: (h, i)),
            pl.BlockSpec((1, LANES), lambda i, h: (0, 0)),
            pl.BlockSpec((1, LANES), lambda i, h: (0, 0)),
            pl.BlockSpec((8, 1), lambda i, h: (h, 0)),
            pl.BlockSpec((8, 1), lambda i, h: (h, 0)),
        ],
        out_specs=[
            wide(), wide(), wide(), wide(),
            pl.BlockSpec((2, TB, 2 * DN_CHUNK), lambda i, h: (0, i, h)),
            pl.BlockSpec((2, 1, nch, 2, LANES), lambda i, h: (0, h, i, 0, 0)),
        ],
        out_shape=[
            big, big, big, big,
            jax.ShapeDtypeStruct((2, T, DN_QK_HEADS * 2 * DN_CHUNK), bf16),
            jax.ShapeDtypeStruct((2, DN_QK_HEADS, T // DN_CHUNK, 2, LANES), f32),
        ],
        compiler_params=_cparams(("parallel", "parallel")),
        name="dn_prep",
    )(qkv, qkv, qkv, small, small_t, ar, dr, ac, dc)


def _dnscan_kernel(uf, wf, qf, kf, af, ef, ub, wb, qb, kb, ab, eb, of_ref, ob_ref, s_ref):
    t = pl.program_id(2)
    C = DN_CHUNK
    nch = uf.shape[1] // C

    @pl.when(t == 0)
    def _():
        s_ref[...] = jnp.zeros(s_ref.shape, f32)

    dirs = ((uf, wf, qf, kf, af, ef, of_ref), (ub, wb, qb, kb, ab, eb, ob_ref))
    for step in range(nch):
        for d in range(2):
            u_r, w_r, q_r, k_r, a_r, e_r, o_r = dirs[d]
            c = step if d == 0 else nch - 1 - step
            rows = slice(c * C, (c + 1) * C)
            for hvl in range(2):
                cols = slice(hvl * DN_V_DIM, (hvl + 1) * DN_V_DIM)
                si = d * 2 + hvl
                s = s_ref[si]
                sb = s.astype(bf16)
                v_new = u_r[0, rows, cols].astype(f32) - jnp.dot(
                    w_r[0, rows, cols], sb, preferred_element_type=f32)
                vb = v_new.astype(bf16)
                o = jnp.dot(q_r[0, rows, cols], sb, preferred_element_type=f32) + jnp.dot(
                    a_r[0, rows, hvl * C:(hvl + 1) * C], vb, preferred_element_type=f32)
                upd = lax.dot_general(k_r[0, rows, cols], vb, (((0,), (0,)), ((), ())),
                                      preferred_element_type=f32)
                s_ref[si] = s * e_r[0, 0, c, hvl:hvl + 1, :] + upd
                o_r[rows, cols] = o


def _dnscan(u, w, qd, kd, aqk, egl, nseq, seq):
    TB = DN_TB
    nt = seq // TB
    nch = TB // DN_CHUNK
    T = nseq * seq

    def specs(d):
        if d == 0:
            blk = lambda b, h, t: b * nt + t
        else:
            blk = lambda b, h, t: b * nt + nt - 1 - t
        wide = lambda: pl.BlockSpec((1, TB, 2 * DN_V_DIM), lambda b, h, t: (d, blk(b, h, t), h))
        return [wide(), wide(), wide(), wide(),
                pl.BlockSpec((1, TB, 2 * DN_CHUNK), lambda b, h, t: (d, blk(b, h, t), h)),
                pl.BlockSpec((1, 1, nch, 2, LANES), lambda b, h, t: (d, h, blk(b, h, t), 0, 0))]

    def ospec(d):
        if d == 0:
            return pl.BlockSpec((TB, 2 * DN_V_DIM), lambda b, h, t: (b * nt + t, h))
        return pl.BlockSpec((TB, 2 * DN_V_DIM), lambda b, h, t: (b * nt + nt - 1 - t, h))

    args = (u, w, qd, kd, aqk, egl)
    return pl.pallas_call(
        _dnscan_kernel,
        grid=(nseq, DN_QK_HEADS, nt),
        in_specs=specs(0) + specs(1),
        out_specs=[ospec(0), ospec(1)],
        out_shape=[jax.ShapeDtypeStruct((T, DN_V_W), f32)] * 2,
        scratch_shapes=[pltpu.VMEM((4, DN_K_DIM, DN_V_DIM), f32)],
        compiler_params=_cparams(("parallel", "parallel", "arbitrary")),
        name="dn_scan",
    )(*args, *args)


def _post_kernel(x_ref, oa_ref, of_ref, ob_ref, z_ref, ga_ref, gd_ref, dnw_ref, wa_ref, wd_ref,
                 wo_ref, n2_ref, rw_ref, rb_ref, x2_ref, hp_ref, idx_ref, gate_ref):
    od = of_ref[...] + ob_ref[...]
    z = z_ref[...].astype(f32)
    parts = []
    for h in range(DN_V_HEADS):
        cols = slice(h * DN_V_DIM, (h + 1) * DN_V_DIM)
        seg = od[:, cols]
        seg = seg * lax.rsqrt(jnp.mean(seg * seg, axis=-1, keepdims=True) + NORM_EPS) * dnw_ref[...]
        zz = z[:, cols]
        parts.append((seg * (zz * _sigmoid(zz))).astype(bf16))
    odn = jnp.concatenate(parts, axis=1)
    ya = jnp.dot(oa_ref[...], wa_ref[...], preferred_element_type=f32)
    yd = jnp.dot(odn, wd_ref[...], preferred_element_type=f32)
    merged = _sigmoid(ga_ref[...].astype(f32)) * ya + _sigmoid(gd_ref[...].astype(f32)) * yd
    x2 = x_ref[...] + jnp.dot(merged.astype(bf16), wo_ref[...], preferred_element_type=f32)
    x2_ref[...] = x2
    h2 = x2 * lax.rsqrt(jnp.mean(x2 * x2, axis=-1, keepdims=True) + NORM_EPS) * n2_ref[...]
    half = D_MODEL // 2
    hp_ref[...] = _pack_halves(h2[:, :half], h2[:, half:])

    logits = jnp.dot(h2, rw_ref[...], precision=lax.Precision.HIGHEST,
                     preferred_element_type=f32) + rb_ref[...]
    lane = lax.broadcasted_iota(i32, logits.shape, 1).astype(f32)
    vals, idxs = [], []
    cur = logits
    for _ in range(TOP_K):
        m = jnp.max(cur, axis=-1, keepdims=True)
        idx = jnp.min(jnp.where(cur == m, lane, float(LANES)), axis=-1, keepdims=True)
        vals.append(m)
        idxs.append(idx)
        cur = jnp.where(lane == idx, -jnp.inf, cur)
    exps = [jnp.exp(v - vals[0]) for v in vals]
    tot = exps[0] + exps[1] + exps[2] + exps[3]
    idx_out = jnp.zeros(logits.shape, f32)
    gate_out = jnp.zeros(logits.shape, f32)
    for kk in range(TOP_K):
        idx_out = jnp.where(lane == float(kk), idxs[kk], idx_out)
        gate_out = jnp.where(lane == float(kk), exps[kk] / tot, gate_out)
    idx_ref[...] = idx_out.astype(i32)
    gate_ref[...] = gate_out


def _post(x, o_attn, o_f, o_b, proj, dnw, wa, wd, wo, n2, rw, rb):
    T = x.shape[0]
    tm = min(POST_TM, T)
    zcol = P_DNZ // DN_V_W
    const = lambda shape: pl.BlockSpec(shape, lambda i: (0, 0))
    row = lambda width, col=0: pl.BlockSpec((tm, width), lambda i: (i, col))
    return pl.pallas_call(
        _post_kernel,
        grid=(T // tm,),
        in_specs=[
            row(D_MODEL), row(ATTN_Q_W), row(DN_V_W), row(DN_V_W),
            row(DN_V_W, zcol), row(D_MODEL, P_GA // D_MODEL), row(D_MODEL, P_GD // D_MODEL),
            const((1, DN_V_DIM)), const((ATTN_Q_W, D_MODEL)), const((DN_V_W, D_MODEL)),
            const((D_MODEL, D_MODEL)), const((1, D_MODEL)), const((D_MODEL, LANES)),
            const((1, LANES)),
        ],
        out_specs=[row(D_MODEL), row(D_MODEL // 2), row(LANES), row(LANES)],
        out_shape=[
            jax.ShapeDtypeStruct((T, D_MODEL), f32),
            jax.ShapeDtypeStruct((T, D_MODEL // 2), u32),
            jax.ShapeDtypeStruct((T, LANES), i32),
            jax.ShapeDtypeStruct((T, LANES), f32),
        ],
        compiler_params=_cparams(("parallel",)),
        name="post_mix",
    )(x, o_attn, o_f, o_b, proj, proj, proj, dnw, wa, wd, wo, n2, rw, rb)


def _gather_kernel(tok_ref, h_ref, o_ref, sem):
    R = o_ref.shape[0]

    def body(r, carry):
        t = tok_ref[0, 0, r]
        pltpu.make_async_copy(h_ref.at[pl.ds(t, 1)], o_ref.at[pl.ds(r, 1)], sem).start()
        return carry

    lax.fori_loop(0, R, body, 0)
    pltpu.make_async_copy(h_ref.at[pl.ds(0, R)], o_ref, sem).wait()


def _gather_rows(slot_tok, hp, n_slots):
    R = GATHER_R
    width = hp.shape[1]
    return pl.pallas_call(
        _gather_kernel,
        grid=(n_slots // R,),
        in_specs=[
            pl.BlockSpec((1, 1, R), lambda i: (i, 0, 0), memory_space=pltpu.SMEM),
            pl.BlockSpec(memory_space=pl.ANY),
        ],
        out_specs=pl.BlockSpec((R, width), lambda i: (i, 0)),
        out_shape=jax.ShapeDtypeStruct((n_slots, width), u32),
        scratch_shapes=[pltpu.SemaphoreType.DMA(())],
        compiler_params=_cparams(("arbitrary",)),
        name="moe_gather",
    )(slot_tok.reshape(n_slots // R, 1, R), hp)


def _moe_kernel(te_ref, nu_ref, x_ref, wg_ref, wu_ref, bg_ref, bu_ref, wd_ref, bd_ref,
                o_ref, xb_ref, acc_ref):
    i = pl.program_id(0)
    f = pl.program_id(1)
    nf = pl.num_programs(1)
    half = D_MODEL // 2
    used = i < nu_ref[0]

    @pl.when(used & (f == 0))
    def _():
        lo, hi = _unpack_halves(x_ref[...])
        xb_ref[:, :half] = lo.astype(bf16)
        xb_ref[:, half:] = hi.astype(bf16)

    @pl.when(used)
    def _():
        xb = xb_ref[...]
        g = jnp.dot(xb, wg_ref[0], preferred_element_type=f32) + bg_ref[0]
        u = jnp.dot(xb, wu_ref[0], preferred_element_type=f32) + bu_ref[0]
        gate = jnp.minimum(g, SWIGLU_LIMIT)
        up = jnp.clip(u, -SWIGLU_LIMIT, SWIGLU_LIMIT)
        act = (up + 1.0) * (gate * _sigmoid(SWIGLU_ALPHA * gate))
        part = jnp.dot(act.astype(bf16), wd_ref[0], preferred_element_type=f32)

        @pl.when(f == 0)
        def _():
            acc_ref[...] = part

        @pl.when(f > 0)
        def _():
            acc_ref[...] += part

    @pl.when(used & (f == nf - 1))
    def _():
        y = acc_ref[...] + bd_ref[0]
        o_ref[...] = _pack_halves(y[:, :half], y[:, half:])

    @pl.when(jnp.logical_not(used) & (f == nf - 1))
    def _():
        o_ref[...] = jnp.zeros(o_ref.shape, u32)


def _moe(tile_expert, n_used, xs, wg, wu, bg, bu, wd, bd):
    n_slots = xs.shape[0]
    tm, tf = MOE_TM, MOE_TF
    n_tiles = n_slots // tm
    nf = D_FF // tf

    def fidx(i, f, te, nu):
        return jnp.where(i < nu[0], f, nf - 1)

    grid_spec = pltpu.PrefetchScalarGridSpec(
        num_scalar_prefetch=2,
        grid=(n_tiles, nf),
        in_specs=[
            pl.BlockSpec((tm, D_MODEL // 2), lambda i, f, te, nu: (i, 0)),
            pl.BlockSpec((1, D_MODEL, tf), lambda i, f, te, nu: (te[i], 0, fidx(i, f, te, nu))),
            pl.BlockSpec((1, D_MODEL, tf), lambda i, f, te, nu: (te[i], 0, fidx(i, f, te, nu))),
            pl.BlockSpec((1, 1, tf), lambda i, f, te, nu: (te[i], 0, fidx(i, f, te, nu))),
            pl.BlockSpec((1, 1, tf), lambda i, f, te, nu: (te[i], 0, fidx(i, f, te, nu))),
            pl.BlockSpec((1, tf, D_MODEL), lambda i, f, te, nu: (te[i], fidx(i, f, te, nu), 0)),
            pl.BlockSpec((1, 1, D_MODEL), lambda i, f, te, nu: (te[i], 0, 0)),
        ],
        out_specs=pl.BlockSpec((tm, D_MODEL // 2), lambda i, f, te, nu: (i, 0)),
        scratch_shapes=[pltpu.VMEM((tm, D_MODEL), bf16), pltpu.VMEM((tm, D_MODEL), f32)],
    )
    return pl.pallas_call(
        _moe_kernel,
        grid_spec=grid_spec,
        out_shape=jax.ShapeDtypeStruct((n_slots, D_MODEL // 2), u32),
        compiler_params=_cparams(("arbitrary", "arbitrary")),
        name="moe_experts",
    )(tile_expert, n_used, xs, wg, wu, bg, bu, wd, bd)


def _combine_kernel(pos_ref, x2_ref, gate_ref, fw_ref, ys_ref, o_ref, buf_ref, sem):
    R = x2_ref.shape[0]
    half = D_MODEL // 2

    def body(a, carry):
        p = pos_ref[0, 0, a]
        pltpu.make_async_copy(ys_ref.at[pl.ds(p, 1)], buf_ref.at[pl.ds(a, 1)], sem).start()
        return carry

    lax.fori_loop(0, TOP_K * R, body, 0)
    pltpu.make_async_copy(ys_ref.at[pl.ds(0, TOP_K * R)], buf_ref, sem).wait()

    gates = gate_ref[...]
    lo_acc = x2_ref[:, :half]
    hi_acc = x2_ref[:, half:]
    for k in range(TOP_K):
        lo, hi = _unpack_halves(buf_ref[k * R:(k + 1) * R, :])
        gk = gates[:, k:k + 1]
        lo_acc = lo_acc + gk * lo
        hi_acc = hi_acc + gk * hi
    ms = (jnp.sum(lo_acc * lo_acc, axis=-1, keepdims=True)
          + jnp.sum(hi_acc * hi_acc, axis=-1, keepdims=True)) / D_MODEL
    inv = lax.rsqrt(ms + NORM_EPS)
    o_ref[:, :half] = lo_acc * inv * fw_ref[:, :half]
    o_ref[:, half:] = hi_acc * inv * fw_ref[:, half:]


def _combine(pos, x2, gates, fw, ys):
    T = x2.shape[0]
    R = min(COMB_R, T)
    return pl.pallas_call(
        _combine_kernel,
        grid=(T // R,),
        in_specs=[
            pl.BlockSpec((1, 1, TOP_K * R), lambda i: (i, 0, 0), memory_space=pltpu.SMEM),
            pl.BlockSpec((R, D_MODEL), lambda i: (i, 0)),
            pl.BlockSpec((R, LANES), lambda i: (i, 0)),
            pl.BlockSpec((1, D_MODEL), lambda i: (0, 0)),
            pl.BlockSpec(memory_space=pl.ANY),
        ],
        out_specs=pl.BlockSpec((R, D_MODEL), lambda i: (i, 0)),
        out_shape=jax.ShapeDtypeStruct((T, D_MODEL), f32),
        scratch_shapes=[pltpu.VMEM((TOP_K * R, D_MODEL // 2), u32), pltpu.SemaphoreType.DMA(())],
        compiler_params=_cparams(("arbitrary",)),
        name="moe_combine",
    )(pos, x2, gates, fw, ys)


def _dn_channels(t):
    lead = t.shape[:-1]
    t = t.reshape(lead + (2, 2, DN_QK_HEADS, 2))
    perm = tuple(range(len(lead))) + tuple(len(lead) + a for a in (2, 0, 1, 3))
    return t.transpose(perm).reshape(lead + (4 * DN_V_HEADS,))


def _pad_lanes(t):
    pad = [(0, 0)] * (t.ndim - 1) + [(0, LANES - t.shape[-1])]
    return jnp.pad(t, pad)


def _deint_kernel(w_ref, p_ref, g_ref, u_ref):
    for c in range(w_ref.shape[2] // (2 * LANES)):
        w = w_ref[0, :, c * 2 * LANES:(c + 1) * 2 * LANES].astype(bf16)
        r = jnp.dot(w, p_ref[...], preferred_element_type=f32)
        g_ref[0, :, c * LANES:(c + 1) * LANES] = r[:, :LANES].astype(bf16)
        u_ref[0, :, c * LANES:(c + 1) * LANES] = r[:, LANES:].astype(bf16)


def _deinterleave(w):
    E, D, F2 = w.shape
    rb, cb = 1024, 1024
    i = jnp.arange(2 * LANES)
    src = jnp.where(i < LANES, 2 * i, 2 * (i - LANES) + 1)
    perm = (jnp.arange(2 * LANES)[:, None] == src[None, :]).astype(bf16)
    out = jax.ShapeDtypeStruct((E, D, F2 // 2), bf16)
    return pl.pallas_call(
        _deint_kernel,
        grid=(E, D // rb, F2 // cb),
        in_specs=[pl.BlockSpec((1, rb, cb), lambda e, r, c: (e, r, c)),
                  pl.BlockSpec((2 * LANES, 2 * LANES), lambda e, r, c: (0, 0))],
        out_specs=[pl.BlockSpec((1, rb, cb // 2), lambda e, r, c: (e, r, c))] * 2,
        out_shape=[out, out],
        compiler_params=_cparams(("parallel", "parallel", "parallel")),
        name="moe_deinterleave",
    )(w, perm)


def _routing(top_idx, T, tm):
    A = T * TOP_K
    n_tiles = A // tm + N_EXPERTS
    n_slots = n_tiles * tm
    flat_e = top_idx.reshape(-1)
    order = jnp.argsort(flat_e, stable=True).astype(i32)
    onehot = (flat_e[:, None] == jnp.arange(N_EXPERTS, dtype=i32)[None, :]).astype(i32)
    csum = jnp.cumsum(onehot, axis=0)
    counts = csum[-1]
    padded = ((counts + tm - 1) // tm) * tm
    start = jnp.cumsum(counts) - counts
    padded_end = jnp.cumsum(padded)
    padded_start = padded_end - padded
    pos = jnp.sum(onehot * (csum - 1 + padded_start[None, :]), axis=1).astype(i32)
    tile_start = jnp.arange(n_tiles, dtype=i32) * tm
    tile_expert = jnp.minimum(
        jnp.sum(padded_end[None, :] <= tile_start[:, None], axis=1), N_EXPERTS - 1).astype(i32)
    slot = jnp.arange(n_slots, dtype=i32)
    slot_e = tile_expert[slot // tm]
    within = slot - padded_start[slot_e]
    src = jnp.clip(start[slot_e] + within, 0, A - 1)
    slot_tok = jnp.where(within < counts[slot_e], order[src] // TOP_K, 0).astype(i32)
    n_used = (padded_end[-1] // tm).astype(i32).reshape(1)
    return slot_tok, pos, tile_expert, n_used, n_slots


def _layer(x, nseq, seq, norm1_w, w_in, b_in, attn_sinks, dn_conv_w, dn_a_log, dn_dt_bias,
           dn_norm_w, w_attn_o, w_dn_o, w_out, norm2_w, router_w, router_b, w_gate_up, b_gate_up,
           w_down, b_down, final_norm_w):
    T = nseq * seq

    order = ([_OFF_GA, D_MODEL], [_OFF_GD, D_MODEL], [_OFF_DNQKV, DN_CONV_W], [_OFF_AQ, ATTN_Q_W],
             [_OFF_DNZ, DN_V_W], [_OFF_AK, ATTN_KV_W], [_OFF_AV, ATTN_KV_W])
    w_main = jnp.concatenate([w_in[:, o:o + n] for o, n in order], axis=1).astype(bf16)
    b_main = jnp.concatenate([b_in[o:o + n] for o, n in order])[None, :]
    nsm = 4 * DN_V_HEADS
    w_small = _pad_lanes(_dn_channels(w_in[:, _OFF_DNB:_OFF_DNB + nsm])).astype(bf16)
    b_small = _pad_lanes(_dn_channels(b_in[_OFF_DNB:_OFF_DNB + nsm]))
    zeros_b = jnp.zeros((2 * DN_V_HEADS,), f32)
    a_ch = _pad_lanes(_dn_channels(jnp.concatenate([zeros_b, dn_a_log.reshape(-1)])))
    d_ch = _pad_lanes(_dn_channels(jnp.concatenate([zeros_b, dn_dt_bias.reshape(-1)])))

    proj, small, small_t = _inproj(x, norm1_w[None, :], w_main, b_main, w_small, b_small[None, :],
                                   w_small.T, b_small[:, None])

    pos = jnp.arange(seq, dtype=f32)
    inv_freq = 1.0 / (ROPE_THETA ** (jnp.arange(0, ROPE_DIM, 2, dtype=f32) / ROPE_DIM))
    ang = pos[:, None] * inv_freq[None, :]
    cos, sin = jnp.cos(ang), jnp.sin(ang)
    half = ROPE_DIM // 2
    ones_tail = jnp.ones((seq, ATTN_HEAD_DIM - ROPE_DIM), f32)
    cos_head = jnp.concatenate([cos, cos, ones_tail], axis=1)
    sin_head = jnp.concatenate([-sin, sin, 0.0 * ones_tail], axis=1)
    cos_t = jnp.concatenate([cos_head, cos_head], axis=1)
    sin_t = jnp.concatenate([sin_head, sin_head], axis=1)

    o_attn = _attention(proj, attn_sinks, cos_t, sin_t, nseq, seq)

    qkv = _dnconv(proj, dn_conv_w, nseq, seq)
    u, w, qd, kd, aqk, egl = _dnprep(qkv, small, small_t, a_ch[None, :], d_ch[None, :],
                                     a_ch[:, None], d_ch[:, None])
    o_f, o_b = _dnscan(u, w, qd, kd, aqk, egl, nseq, seq)

    rw = jnp.zeros((D_MODEL, LANES), f32).at[:, :N_EXPERTS].set(router_w)
    rb = jnp.full((LANES,), NEG_INF, f32).at[:N_EXPERTS].set(router_b)
    x2, hp, top_idx, top_gate = _post(
        x, o_attn, o_f, o_b, proj, dn_norm_w[None, :], w_attn_o.astype(bf16), w_dn_o.astype(bf16),
        w_out.astype(bf16), norm2_w[None, :], rw, rb[None, :])

    slot_tok, pos_a, tile_expert, n_used, n_slots = _routing(top_idx[:, :TOP_K], T, MOE_TM)
    xs = _gather_rows(slot_tok, hp, n_slots)
    wg, wu = _deinterleave(w_gate_up)
    b_pairs = b_gate_up.reshape(N_EXPERTS, 1, D_FF, 2)
    bg, bu = b_pairs[..., 0], b_pairs[..., 1]
    ys = _moe(tile_expert, n_used, xs, wg, wu, bg, bu, w_down.astype(bf16), b_down[:, None, :])
    R = min(COMB_R, T)
    pos_tiles = pos_a.reshape(T // R, R, TOP_K).transpose(0, 2, 1).reshape(T // R, 1, TOP_K * R)
    return _combine(pos_tiles, x2, top_gate, final_norm_w[None, :], ys)


def kernel(x_prompt, x_sample, norm1_w, w_in, b_in, attn_sinks, dn_conv_w, dn_a_log, dn_dt_bias,
           dn_norm_w, w_attn_o, w_dn_o, w_out, norm2_w, router_w, router_b, w_gate_up, b_gate_up,
           w_down, b_down, final_norm_w):
    bp, seq, d = x_prompt.shape
    bs = x_sample.shape[0]
    assert x_sample.shape[1] == seq and d == D_MODEL
    nseq = bp + bs
    x = jnp.concatenate([x_prompt.reshape(bp * seq, d), x_sample.reshape(bs * seq, d)], axis=0)
    y = _layer(x, nseq, seq, norm1_w[0], w_in[0], b_in[0], attn_sinks[0], dn_conv_w[0], dn_a_log[0],
               dn_dt_bias[0], dn_norm_w[0], w_attn_o[0], w_dn_o[0], w_out[0], norm2_w[0],
               router_w[0], router_b[0], w_gate_up[0], b_gate_up[0], w_down[0], b_down[0],
               final_norm_w)
    return (y[:bp * seq].reshape(bp, seq, d), y[bp * seq:].reshape(bs, seq, d))
```

```python
import functools
import math

import jax
import jax.numpy as jnp
from jax import lax
from jax.experimental import pallas as pl
from jax.experimental.pallas import tpu as pltpu

f32 = jnp.float32
bf16 = jnp.bfloat16
u32 = jnp.uint32
i32 = jnp.int32

D_MODEL = 2048
ATTN_HEADS = 16
ATTN_KV_HEADS = 4
ATTN_HEAD_DIM = 64
ATTN_BLOCK = 128
ROPE_THETA = 500000.0
ROPE_DIM = ATTN_HEAD_DIM // 4
DN_QK_HEADS = 4
DN_V_HEADS = 8
DN_K_DIM = 128
DN_V_DIM = 128
DN_CONV = 5
DN_CHUNK = 64
N_EXPERTS = 32
TOP_K = 4
D_FF = D_MODEL
SWIGLU_LIMIT = 7.0
SWIGLU_ALPHA = 1.702
NORM_EPS = 1e-6
NEG_INF = -1e30

ATTN_Q_W = ATTN_HEADS * ATTN_HEAD_DIM
ATTN_KV_W = ATTN_KV_HEADS * ATTN_HEAD_DIM
DN_QK_W = DN_QK_HEADS * DN_K_DIM
DN_V_W = DN_V_HEADS * DN_V_DIM
DN_CONV_W = 2 * DN_QK_W + DN_V_W

_OFF_AQ = 0
_OFF_AK = _OFF_AQ + ATTN_Q_W
_OFF_AV = _OFF_AK + ATTN_KV_W
_OFF_DNQKV = _OFF_AV + ATTN_KV_W
_OFF_DNZ = _OFF_DNQKV + DN_CONV_W
_OFF_DNB = _OFF_DNZ + DN_V_W
_OFF_DNA = _OFF_DNB + 2 * DN_V_HEADS
_OFF_GA = _OFF_DNA + 2 * DN_V_HEADS
_OFF_GD = _OFF_GA + D_MODEL

P_GA = 0
P_GD = P_GA + D_MODEL
P_DNQKV = P_GD + D_MODEL
P_AQ = P_DNQKV + DN_CONV_W
P_DNZ = P_AQ + ATTN_Q_W
P_AK = P_DNZ + DN_V_W
P_AV = P_AK + ATTN_KV_W
P_W = P_AV + ATTN_KV_W

LANES = 128
VMEM_LIMIT = 56 * 1024 * 1024

IN_TM = 1024
IN_TN = 512
DN_TB = 256
POST_TM = 256
MOE_TM = 512
MOE_TF = 512
SCATTER_R = 256
COMB_R = 128


def _cparams(sem):
    return pltpu.CompilerParams(dimension_semantics=sem, vmem_limit_bytes=VMEM_LIMIT)


def _sigmoid(x):
    return 1.0 / (1.0 + jnp.exp(-x))


def _softplus(x):
    return jnp.maximum(x, 0.0) + jnp.log(1.0 + jnp.exp(-jnp.abs(x)))


def _pack_halves(lo, hi):
    a = lax.bitcast_convert_type(lo.astype(bf16).astype(f32), u32)
    b = lax.bitcast_convert_type(hi.astype(bf16).astype(f32), u32)
    return (a >> 16) | (b & jnp.uint32(0xFFFF0000))


def _unpack_halves(w):
    lo = lax.bitcast_convert_type(w << 16, f32)
    hi = lax.bitcast_convert_type(w & jnp.uint32(0xFFFF0000), f32)
    return lo, hi


ROW_TILES = D_MODEL // 2 // LANES


def _store_row_tiles(ref, packed):
    for s in range(ROW_TILES):
        ref[:, s, :] = packed[:, s * LANES:(s + 1) * LANES]


def _inproj_kernel(x_ref, nw_ref, w_ref, b_ref, ws_ref, bs_ref, wst_ref, bst_ref,
                   o_ref, os_ref, ost_ref, h_ref):
    @pl.when(pl.program_id(1) == 0)
    def _():
        x = x_ref[...]
        ms = jnp.mean(x * x, axis=-1, keepdims=True)
        hb = (x * lax.rsqrt(ms + NORM_EPS) * nw_ref[...]).astype(bf16)
        h_ref[...] = hb
        os_ref[...] = jnp.dot(hb, ws_ref[...], preferred_element_type=f32) + bs_ref[...]
        ost_ref[...] = lax.dot_general(wst_ref[...], hb, (((1,), (1,)), ((), ())),
                                       preferred_element_type=f32) + bst_ref[...]

    acc = jnp.dot(h_ref[...], w_ref[...], preferred_element_type=f32)
    o_ref[...] = (acc + b_ref[...]).astype(bf16)


def _inproj(x, nw, w, b, ws, bs, wst, bst):
    T = x.shape[0]
    tm = min(IN_TM, T)
    grid = (T // tm, P_W // IN_TN)
    return pl.pallas_call(
        _inproj_kernel,
        grid=grid,
        in_specs=[
            pl.BlockSpec((tm, D_MODEL), lambda i, j: (i, 0)),
            pl.BlockSpec((1, D_MODEL), lambda i, j: (0, 0)),
            pl.BlockSpec((D_MODEL, IN_TN), lambda i, j: (0, j)),
            pl.BlockSpec((1, IN_TN), lambda i, j: (0, j)),
            pl.BlockSpec((D_MODEL, LANES), lambda i, j: (0, 0)),
            pl.BlockSpec((1, LANES), lambda i, j: (0, 0)),
            pl.BlockSpec((LANES, D_MODEL), lambda i, j: (0, 0)),
            pl.BlockSpec((LANES, 1), lambda i, j: (0, 0)),
        ],
        out_specs=[
            pl.BlockSpec((tm, IN_TN), lambda i, j: (i, j)),
            pl.BlockSpec((tm, LANES), lambda i, j: (i, 0)),
            pl.BlockSpec((LANES, tm), lambda i, j: (0, i)),
        ],
        out_shape=[
            jax.ShapeDtypeStruct((T, P_W), bf16),
            jax.ShapeDtypeStruct((T, LANES), f32),
            jax.ShapeDtypeStruct((LANES, T), f32),
        ],
        scratch_shapes=[pltpu.VMEM((tm, D_MODEL), bf16)],
        compiler_params=_cparams(("parallel", "arbitrary")),
        name="inproj",
    )(x, nw, w, b, ws, bs, wst, bst)


def _rope(x, cos, sin):
    lane = lax.broadcasted_iota(i32, x.shape, 1) % ATTN_HEAD_DIM
    half = ROPE_DIM // 2
    partner = jnp.where((lane >= half) & (lane < ROPE_DIM),
                        pltpu.roll(x, half, 1), pltpu.roll(x, LANES - half, 1))
    return x * cos + partner * sin


def _attn_kernel(sink_ref, q_ref, kp_ref, kc_ref, kn_ref, vp_ref, vc_ref, vn_ref,
                 cos_ref, sin_ref, o_ref, *, nb):
    n = pl.program_id(1)
    W = ATTN_BLOCK
    HD = ATTN_HEAD_DIM
    G = ATTN_HEADS // ATTN_KV_HEADS

    q0 = pl.multiple_of(n * W, W)
    cq, sq = cos_ref[pl.ds(q0, W), :], sin_ref[pl.ds(q0, W), :]
    qs = []
    for c in range(ATTN_Q_W // LANES):
        xq = q_ref[:, c * LANES:(c + 1) * LANES].astype(f32)
        qs.append((_rope(xq, cq, sq) * (HD ** -0.5)).astype(bf16))

    kblocks = (kp_ref, kc_ref, kn_ref)
    ks = [[], []]
    for bi in range(3):
        p0 = pl.multiple_of(jnp.clip(n + bi - 1, 0, nb - 1) * W, W)
        ck, sk = cos_ref[pl.ds(p0, W), :], sin_ref[pl.ds(p0, W), :]
        for c in range(ATTN_KV_W // LANES):
            xk = kblocks[bi][:, c * LANES:(c + 1) * LANES].astype(f32)
            ks[c].append(_rope(xk, ck, sk).astype(bf16))
    kcat = [jnp.concatenate(ks[c], axis=0) for c in range(2)]
    vcat = jnp.concatenate([vp_ref[...], vc_ref[...], vn_ref[...]], axis=0)

    r = lax.broadcasted_iota(i32, (W, 3 * W), 0)
    c = lax.broadcasted_iota(i32, (W, 3 * W), 1)
    valid = (c >= r) & (c <= r + 2 * W)
    valid = valid & ((n > 0) | (c >= W)) & ((n < nb - 1) | (c < 2 * W))
    valid4 = jnp.concatenate([valid] * G, axis=0)

    for kh in range(ATTN_KV_HEADS):
        kc_, ko = divmod(kh * HD, LANES)
        k_h = kcat[kc_][:, ko:ko + HD]
        v_h = vcat[:, kh * HD:(kh + 1) * HD]
        parts = []
        sink_parts = []
        for g in range(G):
            h = kh * G + g
            qc, qo = divmod(h * HD, LANES)
            parts.append(qs[qc][:, qo:qo + HD])
            sink_parts.append(jnp.full((W, 1), sink_ref[h], f32))
        qg = jnp.concatenate(parts, axis=0)
        sink = jnp.concatenate(sink_parts, axis=0)
        s = lax.dot_general(qg, k_h, (((1,), (1,)), ((), ())), preferred_element_type=f32)
        s = jnp.where(valid4, s, NEG_INF)
        m = jnp.maximum(jnp.max(s, axis=-1, keepdims=True), sink)
        p = jnp.exp(s - m)
        denom = jnp.sum(p, axis=-1, keepdims=True) + jnp.exp(sink - m)
        o = jnp.dot(p.astype(bf16), v_h, preferred_element_type=f32) / denom
        for g in range(G):
            h = kh * G + g
            o_ref[:, h * HD:(h + 1) * HD] = o[g * W:(g + 1) * W].astype(bf16)


def _attention(proj, sinks, cos_t, sin_t, nseq, seq):
    W = ATTN_BLOCK
    nb = seq // W
    T = nseq * seq
    qcol = P_AQ // ATTN_Q_W
    kcol = P_AK // ATTN_KV_W
    vcol = P_AV // ATTN_KV_W

    def kv_spec(col, d):
        return pl.BlockSpec(
            (W, ATTN_KV_W), lambda b, n: (b * nb + jnp.clip(n + d, 0, nb - 1), col))

    return pl.pallas_call(
        functools.partial(_attn_kernel, nb=nb),
        grid=(nseq, nb),
        in_specs=[
            pl.BlockSpec(memory_space=pltpu.SMEM),
            pl.BlockSpec((W, ATTN_Q_W), lambda b, n: (b * nb + n, qcol)),
            kv_spec(kcol, -1), kv_spec(kcol, 0), kv_spec(kcol, 1),
            kv_spec(vcol, -1), kv_spec(vcol, 0), kv_spec(vcol, 1),
            pl.BlockSpec(memory_space=pltpu.VMEM),
            pl.BlockSpec(memory_space=pltpu.VMEM),
        ],
        out_specs=pl.BlockSpec((W, ATTN_Q_W), lambda b, n: (b * nb + n, 0)),
        out_shape=jax.ShapeDtypeStruct((T, ATTN_Q_W), bf16),
        compiler_params=_cparams(("parallel", "parallel")),
        name="window_attn",
    )(sinks, proj, proj, proj, proj, proj, proj, proj, cos_t, sin_t)


def _dnconv_kernel(x_ref, w_ref, o_ref, pad_ref, *, seq, rt):
    j = pl.program_id(1)
    half = DN_CONV // 2
    cw = x_ref.shape[1]
    pad_ref[0:8, :] = jnp.zeros((8, cw), f32)
    pad_ref[seq + 8:seq + 16, :] = jnp.zeros((8, cw), f32)
    pad_ref[8:seq + 8, :] = x_ref[...].astype(f32)
    w = w_ref[...]
    qk_blocks = 2 * DN_QK_W // cw
    q_blocks = DN_QK_W // cw
    for r0 in range(0, seq, rt):
        acc = jnp.zeros((rt, cw), f32)
        for t in range(DN_CONV):
            s0 = r0 + 8 - half + t
            acc = acc + pad_ref[s0:s0 + rt, :] * w[t:t + 1, :]
        y = acc * _sigmoid(acc)
        outs = []
        for hh in range(cw // DN_K_DIM):
            seg = y[:, hh * DN_K_DIM:(hh + 1) * DN_K_DIM]
            inv = lax.rsqrt(jnp.sum(seg * seg, axis=-1, keepdims=True) + NORM_EPS)
            fac = jnp.where(j < q_blocks, inv * (DN_K_DIM ** -0.5),
                            jnp.where(j < qk_blocks, inv, jnp.ones_like(inv)))
            outs.append(seg * fac)
        o_ref[r0:r0 + rt, :] = jnp.concatenate(outs, axis=1).astype(bf16)


def _dnconv(proj, conv_w, nseq, seq):
    cw = 256
    T = nseq * seq
    col0 = P_DNQKV // cw
    rt = min(512, seq)
    return pl.pallas_call(
        functools.partial(_dnconv_kernel, seq=seq, rt=rt),
        grid=(nseq, DN_CONV_W // cw),
        in_specs=[
            pl.BlockSpec((seq, cw), lambda b, j: (b, col0 + j)),
            pl.BlockSpec((DN_CONV, cw), lambda b, j: (0, j)),
        ],
        out_specs=pl.BlockSpec((seq, cw), lambda b, j: (b, j)),
        out_shape=jax.ShapeDtypeStruct((T, DN_CONV_W), bf16),
        scratch_shapes=[pltpu.VMEM((seq + 16, cw), f32)],
        compiler_params=_cparams(("parallel", "parallel")),
        name="dn_conv",
    )(proj, conv_w)


def _mm(a, b):
    return jnp.dot(a.astype(bf16), b.astype(bf16), preferred_element_type=f32)


def _unit_tri_inverse_all(lmats, eye, blk_i, blk_j):
    same8 = (blk_i // 8) == (blk_j // 8)
    n1 = [jnp.where(same8, -l, 0.0) for l in lmats]
    p = [eye + n for n in n1]
    n2 = [_mm(n, n) for n in n1]
    p = [pp + _mm(pp, nn) for pp, nn in zip(p, n2)]
    n4 = [_mm(n, n) for n in n2]
    inv = [pp + _mm(pp, nn) for pp, nn in zip(p, n4)]
    for s in (8, 16, 32):
        coupling = ((blk_i // (2 * s)) == (blk_j // (2 * s))) & ((blk_i // s) != (blk_j // s))
        t = [_mm(jnp.where(coupling, l, 0.0), iv) for l, iv in zip(lmats, inv)]
        inv = [iv - _mm(iv, tt) for iv, tt in zip(inv, t)]
    return inv


def _dnprep_kernel(q_ref, k_ref, v_ref, sm_ref, smt_ref, ar_ref, dr_ref, ac_ref, dc_ref,
                   u_ref, w_ref, qd_ref, kd_ref, aqk_ref, egl_ref):
    hq = pl.program_id(1)
    C = DN_CHUNK
    P = 2 * C
    TB = q_ref.shape[0]
    nch = TB // C

    sm = sm_ref[...]
    beta_full = _sigmoid(sm)
    g_full = -jnp.exp(ar_ref[...]) * _softplus(sm + dr_ref[...])
    shift = (LANES - hq * 8) % LANES
    beta_c = pltpu.roll(beta_full, shift, 1)
    g_c = pltpu.roll(g_full, shift, 1)
    smt = smt_ref[...]
    g_r = -jnp.exp(ac_ref[...]) * _softplus(smt + dc_ref[...])

    ri = lax.broadcasted_iota(i32, (TB, TB), 0)
    ci = lax.broadcasted_iota(i32, (TB, TB), 1)
    same = (ri // C) == (ci // C)
    bi = lax.broadcasted_iota(i32, (P, P), 0)
    bj = lax.broadcasted_iota(i32, (P, P), 1)
    eye = (bi == bj).astype(f32)
    same_head = (bi // C) == (bj // C)

    hp = lax.Precision.HIGHEST
    gcum_c, gcum_r = [], []
    for d in range(2):
        tri = (same & ((ci <= ri) if d == 0 else (ci >= ri))).astype(f32)
        gcum_c.append(jnp.dot(tri, g_c, precision=hp, preferred_element_type=f32))
        gcum_r.append(lax.dot_general(g_r, tri, (((1,), (1,)), ((), ())), precision=hp,
                                      preferred_element_type=f32))

    k2, q2, kk2, qk2, v2 = [], [], [], [], []
    for c in range(nch):
        rows = slice(c * C, (c + 1) * C)
        k = k_ref[rows, :]
        q = q_ref[rows, :]
        kp = jnp.concatenate([k, k], axis=0)
        qp = jnp.concatenate([q, q], axis=0)
        k2.append(kp.astype(f32))
        q2.append(qp.astype(f32))
        kk2.append(lax.dot_general(kp, kp, (((1,), (1,)), ((), ())), preferred_element_type=f32))
        qk2.append(lax.dot_general(qp, kp, (((1,), (1,)), ((), ())), preferred_element_type=f32))
        v2.append(jnp.concatenate([v_ref[rows, :DN_V_DIM], v_ref[rows, DN_V_DIM:]], axis=0).astype(f32))

    inst = [(d, c) for d in range(2) for c in range(nch)]
    gcol, bcol, decay, lmats = [], [], [], []
    for d, c in inst:
        rows = slice(c * C, (c + 1) * C)
        b0, g0 = d * 2, 4 + d * 2
        gc = jnp.concatenate([gcum_c[d][rows, g0:g0 + 1], gcum_c[d][rows, g0 + 1:g0 + 2]], axis=0)
        gr = jnp.concatenate([gcum_r[d][g0:g0 + 1, rows], gcum_r[d][g0 + 1:g0 + 2, rows]], axis=1)
        bc = jnp.concatenate([beta_c[rows, b0:b0 + 1], beta_c[rows, b0 + 1:b0 + 2]], axis=0)
        incl = same_head & ((bj <= bi) if d == 0 else (bj >= bi))
        strict = same_head & ((bj < bi) if d == 0 else (bj > bi))
        dec = jnp.where(incl, jnp.exp(jnp.where(incl, gc - gr, 0.0)), 0.0)
        gcol.append(gc)
        bcol.append(bc)
        decay.append(dec)
        lmats.append(jnp.where(strict, kk2[c] * bc * dec, 0.0))

    invs = _unit_tri_inverse_all(lmats, eye, bi, bj)

    for n, (d, c) in enumerate(inst):
        rows = slice(c * C, (c + 1) * C)
        last = C - 1 if d == 0 else 0
        gc, bc = gcol[n], bcol[n]
        eg = jnp.exp(gc)
        rhs = jnp.concatenate([v2[c] * bc, k2[c] * (bc * eg)], axis=1)
        sol = _mm(invs[n], rhs)
        gl0 = gc[last:last + 1, :]
        gl1 = gc[C + last:C + last + 1, :]
        glast = jnp.concatenate([jnp.broadcast_to(gl0, (C, 1)), jnp.broadcast_to(gl1, (C, 1))], axis=0)
        qd = q2[c] * eg
        kd = k2[c] * jnp.exp(glast - gc)
        aqk = qk2[c] * decay[n]
        for hvl in range(2):
            pr = slice(hvl * C, (hvl + 1) * C)
            cols = slice(hvl * DN_V_DIM, (hvl + 1) * DN_V_DIM)
            u_ref[d, rows, cols] = sol[pr, :DN_V_DIM].astype(bf16)
            w_ref[d, rows, cols] = sol[pr, DN_V_DIM:].astype(bf16)
            qd_ref[d, rows, cols] = qd[pr].astype(bf16)
            kd_ref[d, rows, cols] = kd[pr].astype(bf16)
            aqk_ref[d, rows, pr] = aqk[pr, pr].astype(bf16)
        egl_ref[d, 0, c, 0:1, :] = jnp.broadcast_to(jnp.exp(gl0), (1, LANES))
        egl_ref[d, 0, c, 1:2, :] = jnp.broadcast_to(jnp.exp(gl1), (1, LANES))


def _dnprep(qkv, small, small_t, ar, dr, ac, dc):
    T = qkv.shape[0]
    TB = DN_TB
    nblk = T // TB
    nch = TB // DN_CHUNK
    kcol = DN_QK_W // DN_K_DIM
    vcol = 2 * DN_QK_W // (2 * DN_V_DIM)
    wide = lambda: pl.BlockSpec((2, TB, 2 * DN_V_DIM), lambda i, h: (0, i, h))
    big = jax.ShapeDtypeStruct((2, T, DN_V_W), bf16)
    return pl.pallas_call(
        _dnprep_kernel,
        grid=(nblk, DN_QK_HEADS),
        in_specs=[
            pl.BlockSpec((TB, DN_K_DIM), lambda i, h: (i, h)),
            pl.BlockSpec((TB, DN_K_DIM), lambda i, h: (i, kcol + h)),
            pl.BlockSpec((TB, 2 * DN_V_DIM), lambda i, h: (i, vcol + h)),
            pl.BlockSpec((TB, LANES), lambda i, h: (i, 0)),
            pl.BlockSpec((8, TB), lambda i, h: (h, i)),
            pl.BlockSpec((1, LANES), lambda i, h: (0, 0)),
            pl.BlockSpec((1, LANES), lambda i, h: (0, 0)),
            pl.BlockSpec((8, 1), lambda i, h: (h, 0)),
            pl.BlockSpec((8, 1), lambda i, h: (h, 0)),
        ],
        out_specs=[
            wide(), wide(), wide(), wide(),
            pl.BlockSpec((2, TB, 2 * DN_CHUNK), lambda i, h: (0, i, h)),
            pl.BlockSpec((2, 1, nch, 2, LANES), lambda i, h: (0, h, i, 0, 0)),
        ],
        out_shape=[
            big, big, big, big,
            jax.ShapeDtypeStruct((2, T, DN_QK_HEADS * 2 * DN_CHUNK), bf16),
            jax.ShapeDtypeStruct((2, DN_QK_HEADS, T // DN_CHUNK, 2, LANES), f32),
        ],
        compiler_params=_cparams(("parallel", "parallel")),
        name="dn_prep",
    )(qkv, qkv, qkv, small, small_t, ar, dr, ac, dc)


def _dnscan_kernel(uf, wf, qf, kf, af, ef, ub, wb, qb, kb, ab, eb, of_ref, ob_ref, s_ref):
    t = pl.program_id(2)
    C = DN_CHUNK
    nch = uf.shape[1] // C

    @pl.when(t == 0)
    def _():
        s_ref[...] = jnp.zeros(s_ref.shape, f32)

    dirs = ((uf, wf, qf, kf, af, ef, of_ref), (ub, wb, qb, kb, ab, eb, ob_ref))
    for step in range(nch):
        for d in range(2):
            u_r, w_r, q_r, k_r, a_r, e_r, o_r = dirs[d]
            c = step if d == 0 else nch - 1 - step
            rows = slice(c * C, (c + 1) * C)
            for hvl in range(2):
                cols = slice(hvl * DN_V_DIM, (hvl + 1) * DN_V_DIM)
                si = d * 2 + hvl
                s = s_ref[si]
                sb = s.astype(bf16)
                v_new = u_r[0, rows, cols].astype(f32) - jnp.dot(
                    w_r[0, rows, cols], sb, preferred_element_type=f32)
                vb = v_new.astype(bf16)
                o = jnp.dot(q_r[0, rows, cols], sb, preferred_element_type=f32) + jnp.dot(
                    a_r[0, rows, hvl * C:(hvl + 1) * C], vb, preferred_element_type=f32)
                upd = lax.dot_general(k_r[0, rows, cols], vb, (((0,), (0,)), ((), ())),
                                      preferred_element_type=f32)
                s_ref[si] = s * e_r[0, 0, c, hvl:hvl + 1, :] + upd
                o_r[rows, cols] = o


def _dnscan(u, w, qd, kd, aqk, egl, nseq, seq):
    TB = DN_TB
    nt = seq // TB
    nch = TB // DN_CHUNK
    T = nseq * seq

    def specs(d):
        if d == 0:
            blk = lambda b, h, t: b * nt + t
        else:
            blk = lambda b, h, t: b * nt + nt - 1 - t
        wide = lambda: pl.BlockSpec((1, TB, 2 * DN_V_DIM), lambda b, h, t: (d, blk(b, h, t), h))
        return [wide(), wide(), wide(), wide(),
                pl.BlockSpec((1, TB, 2 * DN_CHUNK), lambda b, h, t: (d, blk(b, h, t), h)),
                pl.BlockSpec((1, 1, nch, 2, LANES), lambda b, h, t: (d, h, blk(b, h, t), 0, 0))]

    def ospec(d):
        if d == 0:
            return pl.BlockSpec((TB, 2 * DN_V_DIM), lambda b, h, t: (b * nt + t, h))
        return pl.BlockSpec((TB, 2 * DN_V_DIM), lambda b, h, t: (b * nt + nt - 1 - t, h))

    args = (u, w, qd, kd, aqk, egl)
    return pl.pallas_call(
        _dnscan_kernel,
        grid=(nseq, DN_QK_HEADS, nt),
        in_specs=specs(0) + specs(1),
        out_specs=[ospec(0), ospec(1)],
        out_shape=[jax.ShapeDtypeStruct((T, DN_V_W), f32)] * 2,
        scratch_shapes=[pltpu.VMEM((4, DN_K_DIM, DN_V_DIM), f32)],
        compiler_params=_cparams(("parallel", "parallel", "arbitrary")),
        name="dn_scan",
    )(*args, *args)


def _post_kernel(x_ref, oa_ref, of_ref, ob_ref, z_ref, ga_ref, gd_ref, dnw_ref, wa_ref, wd_ref,
                 wo_ref, n2_ref, rw_ref, rb_ref, x2_ref, hp_ref, idx_ref, gate_ref):
    od = of_ref[...] + ob_ref[...]
    z = z_ref[...].astype(f32)
    parts = []
    for h in range(DN_V_HEADS):
        cols = slice(h * DN_V_DIM, (h + 1) * DN_V_DIM)
        seg = od[:, cols]
        seg = seg * lax.rsqrt(jnp.mean(seg * seg, axis=-1, keepdims=True) + NORM_EPS) * dnw_ref[...]
        zz = z[:, cols]
        parts.append((seg * (zz * _sigmoid(zz))).astype(bf16))
    odn = jnp.concatenate(parts, axis=1)
    ya = jnp.dot(oa_ref[...], wa_ref[...], preferred_element_type=f32)
    yd = jnp.dot(odn, wd_ref[...], preferred_element_type=f32)
    merged = _sigmoid(ga_ref[...].astype(f32)) * ya + _sigmoid(gd_ref[...].astype(f32)) * yd
    x2 = x_ref[...] + jnp.dot(merged.astype(bf16), wo_ref[...], preferred_element_type=f32)
    x2_ref[...] = x2
    h2 = x2 * lax.rsqrt(jnp.mean(x2 * x2, axis=-1, keepdims=True) + NORM_EPS) * n2_ref[...]
    half = D_MODEL // 2
    _store_row_tiles(hp_ref, _pack_halves(h2[:, :half], h2[:, half:]))

    logits = jnp.dot(h2, rw_ref[...], precision=lax.Precision.HIGHEST,
                     preferred_element_type=f32) + rb_ref[...]
    lane = lax.broadcasted_iota(i32, logits.shape, 1).astype(f32)
    vals, idxs = [], []
    cur = logits
    for _ in range(TOP_K):
        m = jnp.max(cur, axis=-1, keepdims=True)
        idx = jnp.min(jnp.where(cur == m, lane, float(LANES)), axis=-1, keepdims=True)
        vals.append(m)
        idxs.append(idx)
        cur = jnp.where(lane == idx, -jnp.inf, cur)
    exps = [jnp.exp(v - vals[0]) for v in vals]
    tot = exps[0] + exps[1] + exps[2] + exps[3]
    idx_out = jnp.zeros(logits.shape, f32)
    gate_out = jnp.zeros(logits.shape, f32)
    for kk in range(TOP_K):
        idx_out = jnp.where(lane == float(kk), idxs[kk], idx_out)
        gate_out = jnp.where(lane == float(kk), exps[kk] / tot, gate_out)
    idx_ref[...] = idx_out.astype(i32)
    gate_ref[...] = gate_out


def _post(x, o_attn, o_f, o_b, proj, dnw, wa, wd, wo, n2, rw, rb):
    T = x.shape[0]
    tm = min(POST_TM, T)
    zcol = P_DNZ // DN_V_W
    const = lambda shape: pl.BlockSpec(shape, lambda i: (0, 0))
    row = lambda width, col=0: pl.BlockSpec((tm, width), lambda i: (i, col))
    return pl.pallas_call(
        _post_kernel,
        grid=(T // tm,),
        in_specs=[
            row(D_MODEL), row(ATTN_Q_W), row(DN_V_W), row(DN_V_W),
            row(DN_V_W, zcol), row(D_MODEL, P_GA // D_MODEL), row(D_MODEL, P_GD // D_MODEL),
            const((1, DN_V_DIM)), const((ATTN_Q_W, D_MODEL)), const((DN_V_W, D_MODEL)),
            const((D_MODEL, D_MODEL)), const((1, D_MODEL)), const((D_MODEL, LANES)),
            const((1, LANES)),
        ],
        out_specs=[row(D_MODEL), pl.BlockSpec((tm, ROW_TILES, LANES), lambda i: (i, 0, 0)),
                   row(LANES), row(LANES)],
        out_shape=[
            jax.ShapeDtypeStruct((T, D_MODEL), f32),
            jax.ShapeDtypeStruct((T, ROW_TILES, LANES), u32),
            jax.ShapeDtypeStruct((T, LANES), i32),
            jax.ShapeDtypeStruct((T, LANES), f32),
        ],
        compiler_params=_cparams(("parallel",)),
        name="post_mix",
    )(x, o_attn, o_f, o_b, proj, proj, proj, dnw, wa, wd, wo, n2, rw, rb)


def _scatter_kernel(poff_ref, pn_ref, pos_ref, h_ref, xs_ref, zero_ref, sem, zsem):
    R = h_ref.shape[0]
    nbits = MOE_TM.bit_length() - 1

    @pl.when(pl.program_id(0) == 0)
    def _():
        zero_ref[...] = jnp.zeros(zero_ref.shape, u32)

        def per_expert(e, carry):
            off = poff_ref[e]
            n = pn_ref[e]
            for bit in reversed(range(nbits)):
                size = 1 << bit
                take = ((n >> bit) & 1) == 1

                @pl.when(take)
                def _():
                    cp = pltpu.make_async_copy(zero_ref.at[pl.ds(0, size)],
                                               xs_ref.at[pl.ds(off, size)], zsem)
                    cp.start()
                    cp.wait()

                off = off + jnp.where(take, size, 0)
            return carry

        lax.fori_loop(0, N_EXPERTS, per_expert, 0)

        zrows = zero_ref.shape[0]

        def tail(j, carry):
            row = pl.multiple_of(poff_ref[N_EXPERTS] + j * zrows, zrows)
            cp = pltpu.make_async_copy(zero_ref, xs_ref.at[pl.ds(row, zrows)], zsem)
            cp.start()
            cp.wait()
            return carry

        lax.fori_loop(0, pn_ref[N_EXPERTS], tail, 0)

    def body(a, carry):
        p = pos_ref[0, 0, a]
        pltpu.make_async_copy(h_ref.at[pl.ds(a // TOP_K, 1)], xs_ref.at[pl.ds(p, 1)], sem).start()
        return carry

    lax.fori_loop(0, TOP_K * R, body, 0, unroll=8)
    pltpu.make_async_copy(xs_ref.at[pl.ds(0, TOP_K * R)], xs_ref.at[pl.ds(0, TOP_K * R)], sem).wait()


def _scatter_rows(pad_off, pad_n, pos, hp, n_slots):
    T = hp.shape[0]
    R = min(SCATTER_R, T)
    grid_spec = pltpu.PrefetchScalarGridSpec(
        num_scalar_prefetch=2,
        grid=(T // R,),
        in_specs=[
            pl.BlockSpec((1, 1, TOP_K * R), lambda i, po, pn: (i, 0, 0), memory_space=pltpu.SMEM),
            pl.BlockSpec((R, ROW_TILES, LANES), lambda i, po, pn: (i, 0, 0)),
        ],
        out_specs=pl.BlockSpec(memory_space=pl.ANY),
        scratch_shapes=[pltpu.VMEM((MOE_TM // 2, ROW_TILES, LANES), u32),
                        pltpu.SemaphoreType.DMA(()), pltpu.SemaphoreType.DMA(())],
    )
    return pl.pallas_call(
        _scatter_kernel,
        grid_spec=grid_spec,
        out_shape=jax.ShapeDtypeStruct((n_slots, ROW_TILES, LANES), u32),
        compiler_params=_cparams(("arbitrary",)),
        name="moe_scatter",
    )(pad_off, pad_n, pos.reshape(T // R, 1, TOP_K * R), hp)


def _moe_kernel(te_ref, nu_ref, x_ref, wg_ref, wu_ref, bg_ref, bu_ref, wd_ref, bd_ref,
                o_ref, xb_ref, acc_ref):
    i = pl.program_id(0)
    f = pl.program_id(1)
    nf = pl.num_programs(1)
    half = D_MODEL // 2
    used = i < nu_ref[0]

    @pl.when(used & (f == 0))
    def _():
        for s in range(ROW_TILES):
            lo, hi = _unpack_halves(x_ref[:, s, :])
            xb_ref[:, s * LANES:(s + 1) * LANES] = lo.astype(bf16)
            xb_ref[:, half + s * LANES:half + (s + 1) * LANES] = hi.astype(bf16)

    @pl.when(used)
    def _():
        xb = xb_ref[...]
        g = jnp.dot(xb, wg_ref[0], preferred_element_type=f32) + bg_ref[0]
        u = jnp.dot(xb, wu_ref[0], preferred_element_type=f32) + bu_ref[0]
        gate = jnp.minimum(g, SWIGLU_LIMIT)
        up = jnp.clip(u, -SWIGLU_LIMIT, SWIGLU_LIMIT)
        act = (up + 1.0) * (gate * _sigmoid(SWIGLU_ALPHA * gate))
        part = jnp.dot(act.astype(bf16), wd_ref[0], preferred_element_type=f32)

        @pl.when(f == 0)
        def _():
            acc_ref[...] = part

        @pl.when(f > 0)
        def _():
            acc_ref[...] += part

    @pl.when(used & (f == nf - 1))
    def _():
        y = acc_ref[...] + bd_ref[0]
        _store_row_tiles(o_ref, _pack_halves(y[:, :half], y[:, half:]))

    @pl.when(jnp.logical_not(used) & (f == nf - 1))
    def _():
        o_ref[...] = jnp.zeros(o_ref.shape, u32)


def _moe(tile_expert, n_used, xs, wg, wu, bg, bu, wd, bd):
    n_slots = xs.shape[0]
    tm, tf = MOE_TM, MOE_TF
    n_tiles = n_slots // tm
    nf = D_FF // tf

    def fidx(i, f, te, nu):
        return jnp.where(i < nu[0], f, nf - 1)

    grid_spec = pltpu.PrefetchScalarGridSpec(
        num_scalar_prefetch=2,
        grid=(n_tiles, nf),
        in_specs=[
            pl.BlockSpec((tm, ROW_TILES, LANES), lambda i, f, te, nu: (i, 0, 0)),
            pl.BlockSpec((1, D_MODEL, tf), lambda i, f, te, nu: (te[i], 0, fidx(i, f, te, nu))),
            pl.BlockSpec((1, D_MODEL, tf), lambda i, f, te, nu: (te[i], 0, fidx(i, f, te, nu))),
            pl.BlockSpec((1, 1, tf), lambda i, f, te, nu: (te[i], 0, fidx(i, f, te, nu))),
            pl.BlockSpec((1, 1, tf), lambda i, f, te, nu: (te[i], 0, fidx(i, f, te, nu))),
            pl.BlockSpec((1, tf, D_MODEL), lambda i, f, te, nu: (te[i], fidx(i, f, te, nu), 0)),
            pl.BlockSpec((1, 1, D_MODEL), lambda i, f, te, nu: (te[i], 0, 0)),
        ],
        out_specs=pl.BlockSpec((tm, ROW_TILES, LANES), lambda i, f, te, nu: (i, 0, 0)),
        scratch_shapes=[pltpu.VMEM((tm, D_MODEL), bf16), pltpu.VMEM((tm, D_MODEL), f32)],
    )
    return pl.pallas_call(
        _moe_kernel,
        grid_spec=grid_spec,
        out_shape=jax.ShapeDtypeStruct((n_slots, ROW_TILES, LANES), u32),
        compiler_params=_cparams(("arbitrary", "arbitrary")),
        name="moe_experts",
    )(tile_expert, n_used, xs, wg, wu, bg, bu, wd, bd)


def _combine_kernel(pos_ref, x2_ref, gate_ref, fw_ref, ys_ref, o_ref, buf_ref, sem):
    R = x2_ref.shape[0]
    half = D_MODEL // 2

    def body(a, carry):
        p = pos_ref[0, 0, a]
        pltpu.make_async_copy(ys_ref.at[pl.ds(p, 1)], buf_ref.at[pl.ds(a, 1)], sem).start()
        return carry

    lax.fori_loop(0, TOP_K * R, body, 0, unroll=8)
    pltpu.make_async_copy(ys_ref.at[pl.ds(0, TOP_K * R)], buf_ref, sem).wait()

    gates = gate_ref[...]
    gk = [gates[:, k:k + 1] for k in range(TOP_K)]
    ss = jnp.zeros((R, 1), f32)
    for s in range(ROW_TILES):
        cl = slice(s * LANES, (s + 1) * LANES)
        ch = slice(half + s * LANES, half + (s + 1) * LANES)
        lo_acc = x2_ref[:, cl]
        hi_acc = x2_ref[:, ch]
        for k in range(TOP_K):
            lo, hi = _unpack_halves(buf_ref[k * R:(k + 1) * R, s, :])
            lo_acc = lo_acc + gk[k] * lo
            hi_acc = hi_acc + gk[k] * hi
        ss = ss + jnp.sum(lo_acc * lo_acc, axis=-1, keepdims=True) + jnp.sum(
            hi_acc * hi_acc, axis=-1, keepdims=True)
        o_ref[:, cl] = lo_acc
        o_ref[:, ch] = hi_acc
    inv = lax.rsqrt(ss / D_MODEL + NORM_EPS)
    o_ref[...] = o_ref[...] * inv * fw_ref[...]


def _combine(pos, x2, gates, fw, ys):
    T = x2.shape[0]
    R = min(COMB_R, T)
    return pl.pallas_call(
        _combine_kernel,
        grid=(T // R,),
        in_specs=[
            pl.BlockSpec((1, 1, TOP_K * R), lambda i: (i, 0, 0), memory_space=pltpu.SMEM),
            pl.BlockSpec((R, D_MODEL), lambda i: (i, 0)),
            pl.BlockSpec((R, LANES), lambda i: (i, 0)),
            pl.BlockSpec((1, D_MODEL), lambda i: (0, 0)),
            pl.BlockSpec(memory_space=pl.ANY),
        ],
        out_specs=pl.BlockSpec((R, D_MODEL), lambda i: (i, 0)),
        out_shape=jax.ShapeDtypeStruct((T, D_MODEL), f32),
        scratch_shapes=[pltpu.VMEM((TOP_K * R, ROW_TILES, LANES), u32), pltpu.SemaphoreType.DMA(())],
        compiler_params=_cparams(("arbitrary",)),
        name="moe_combine",
    )(pos, x2, gates, fw, ys)


def _dn_channels(t):
    lead = t.shape[:-1]
    t = t.reshape(lead + (2, 2, DN_QK_HEADS, 2))
    perm = tuple(range(len(lead))) + tuple(len(lead) + a for a in (2, 0, 1, 3))
    return t.transpose(perm).reshape(lead + (4 * DN_V_HEADS,))


def _pad_lanes(t):
    pad = [(0, 0)] * (t.ndim - 1) + [(0, LANES - t.shape[-1])]
    return jnp.pad(t, pad)


def _deint_kernel(w_ref, p_ref, g_ref, u_ref):
    for c in range(w_ref.shape[2] // (2 * LANES)):
        w = w_ref[0, :, c * 2 * LANES:(c + 1) * 2 * LANES].astype(bf16)
        r = jnp.dot(w, p_ref[...], preferred_element_type=f32)
        g_ref[0, :, c * LANES:(c + 1) * LANES] = r[:, :LANES].astype(bf16)
        u_ref[0, :, c * LANES:(c + 1) * LANES] = r[:, LANES:].astype(bf16)


def _deinterleave(w):
    E, D, F2 = w.shape
    rb, cb = 1024, 1024
    i = jnp.arange(2 * LANES)
    src = jnp.where(i < LANES, 2 * i, 2 * (i - LANES) + 1)
    perm = (jnp.arange(2 * LANES)[:, None] == src[None, :]).astype(bf16)
    out = jax.ShapeDtypeStruct((E, D, F2 // 2), bf16)
    return pl.pallas_call(
        _deint_kernel,
        grid=(E, D // rb, F2 // cb),
        in_specs=[pl.BlockSpec((1, rb, cb), lambda e, r, c: (e, r, c)),
                  pl.BlockSpec((2 * LANES, 2 * LANES), lambda e, r, c: (0, 0))],
        out_specs=[pl.BlockSpec((1, rb, cb // 2), lambda e, r, c: (e, r, c))] * 2,
        out_shape=[out, out],
        compiler_params=_cparams(("parallel", "parallel", "parallel")),
        name="moe_deinterleave",
    )(w, perm)


def _routing(top_idx, T, tm):
    A = T * TOP_K
    n_tiles = A // tm + N_EXPERTS
    n_slots = n_tiles * tm
    flat_e = top_idx.reshape(-1)
    onehot = (flat_e[:, None] == jnp.arange(N_EXPERTS, dtype=i32)[None, :]).astype(i32)
    csum = jnp.cumsum(onehot, axis=0)
    counts = csum[-1]
    padded = ((counts + tm - 1) // tm) * tm
    padded_end = jnp.cumsum(padded)
    padded_start = padded_end - padded
    pos = jnp.sum(onehot * (csum - 1 + padded_start[None, :]), axis=1).astype(i32)
    tile_start = jnp.arange(n_tiles, dtype=i32) * tm
    tile_expert = jnp.minimum(
        jnp.sum(padded_end[None, :] <= tile_start[:, None], axis=1), N_EXPERTS - 1).astype(i32)
    n_used = (padded_end[-1] // tm).astype(i32).reshape(1)
    pad_off = jnp.concatenate([padded_start + counts, padded_end[-1:]]).astype(i32)
    pad_n = jnp.concatenate([padded - counts, (n_slots - padded_end[-1:]) // (tm // 2)]).astype(i32)
    return pos, pad_off, pad_n, tile_expert, n_used, n_slots


def _layer(x, nseq, seq, norm1_w, w_in, b_in, attn_sinks, dn_conv_w, dn_a_log, dn_dt_bias,
           dn_norm_w, w_attn_o, w_dn_o, w_out, norm2_w, router_w, router_b, w_gate_up, b_gate_up,
           w_down, b_down, final_norm_w):
    T = nseq * seq

    order = ([_OFF_GA, D_MODEL], [_OFF_GD, D_MODEL], [_OFF_DNQKV, DN_CONV_W], [_OFF_AQ, ATTN_Q_W],
             [_OFF_DNZ, DN_V_W], [_OFF_AK, ATTN_KV_W], [_OFF_AV, ATTN_KV_W])
    w_main = jnp.concatenate([w_in[:, o:o + n] for o, n in order], axis=1).astype(bf16)
    b_main = jnp.concatenate([b_in[o:o + n] for o, n in order])[None, :]
    nsm = 4 * DN_V_HEADS
    w_small = _pad_lanes(_dn_channels(w_in[:, _OFF_DNB:_OFF_DNB + nsm])).astype(bf16)
    b_small = _pad_lanes(_dn_channels(b_in[_OFF_DNB:_OFF_DNB + nsm]))
    zeros_b = jnp.zeros((2 * DN_V_HEADS,), f32)
    a_ch = _pad_lanes(_dn_channels(jnp.concatenate([zeros_b, dn_a_log.reshape(-1)])))
    d_ch = _pad_lanes(_dn_channels(jnp.concatenate([zeros_b, dn_dt_bias.reshape(-1)])))

    proj, small, small_t = _inproj(x, norm1_w[None, :], w_main, b_main, w_small, b_small[None, :],
                                   w_small.T, b_small[:, None])

    pos = jnp.arange(seq, dtype=f32)
    inv_freq = 1.0 / (ROPE_THETA ** (jnp.arange(0, ROPE_DIM, 2, dtype=f32) / ROPE_DIM))
    ang = pos[:, None] * inv_freq[None, :]
    cos, sin = jnp.cos(ang), jnp.sin(ang)
    half = ROPE_DIM // 2
    ones_tail = jnp.ones((seq, ATTN_HEAD_DIM - ROPE_DIM), f32)
    cos_head = jnp.concatenate([cos, cos, ones_tail], axis=1)
    sin_head = jnp.concatenate([-sin, sin, 0.0 * ones_tail], axis=1)
    cos_t = jnp.concatenate([cos_head, cos_head], axis=1)
    sin_t = jnp.concatenate([sin_head, sin_head], axis=1)

    o_attn = _attention(proj, attn_sinks, cos_t, sin_t, nseq, seq)

    qkv = _dnconv(proj, dn_conv_w, nseq, seq)
    u, w, qd, kd, aqk, egl = _dnprep(qkv, small, small_t, a_ch[None, :], d_ch[None, :],
                                     a_ch[:, None], d_ch[:, None])
    o_f, o_b = _dnscan(u, w, qd, kd, aqk, egl, nseq, seq)

    rw = jnp.zeros((D_MODEL, LANES), f32).at[:, :N_EXPERTS].set(router_w)
    rb = jnp.full((LANES,), NEG_INF, f32).at[:N_EXPERTS].set(router_b)
    x2, hp, top_idx, top_gate = _post(
        x, o_attn, o_f, o_b, proj, dn_norm_w[None, :], w_attn_o.astype(bf16), w_dn_o.astype(bf16),
        w_out.astype(bf16), norm2_w[None, :], rw, rb[None, :])

    pos_a, pad_off, pad_n, tile_expert, n_used, n_slots = _routing(top_idx[:, :TOP_K], T, MOE_TM)
    xs = _scatter_rows(pad_off, pad_n, pos_a, hp, n_slots)
    wg, wu = _deinterleave(w_gate_up)
    b_pairs = b_gate_up.reshape(N_EXPERTS, 1, D_FF, 2)
    bg, bu = b_pairs[..., 0], b_pairs[..., 1]
    ys = _moe(tile_expert, n_used, xs, wg, wu, bg, bu, w_down.astype(bf16), b_down[:, None, :])
    R = min(COMB_R, T)
    pos_tiles = pos_a.reshape(T // R, R, TOP_K).transpose(0, 2, 1).reshape(T // R, 1, TOP_K * R)
    return _combine(pos_tiles, x2, top_gate, final_norm_w[None, :], ys)


def kernel(x_prompt, x_sample, norm1_w, w_in, b_in, attn_sinks, dn_conv_w, dn_a_log, dn_dt_bias,
           dn_norm_w, w_attn_o, w_dn_o, w_out, norm2_w, router_w, router_b, w_gate_up, b_gate_up,
           w_down, b_down, final_norm_w):
    bp, seq, d = x_prompt.shape
    bs = x_sample.shape[0]
    assert x_sample.shape[1] == seq and d == D_MODEL
    nseq = bp + bs
    x = jnp.concatenate([x_prompt.reshape(bp * seq, d), x_sample.reshape(bs * seq, d)], axis=0)
    y = _layer(x, nseq, seq, norm1_w[0], w_in[0], b_in[0], attn_sinks[0], dn_conv_w[0], dn_a_log[0],
               dn_dt_bias[0], dn_norm_w[0], w_attn_o[0], w_dn_o[0], w_out[0], norm2_w[0],
               router_w[0], router_b[0], w_gate_up[0], b_gate_up[0], w_down[0], b_down[0],
               final_norm_w)
    return (y[:bp * seq].reshape(bp, seq, d), y[bp * seq:].reshape(bs, seq, d))
```

```python
import functools
import math

import jax
import jax.numpy as jnp
from jax import lax
from jax.experimental import pallas as pl
from jax.experimental.pallas import tpu as pltpu

f32 = jnp.float32
bf16 = jnp.bfloat16
u32 = jnp.uint32
i32 = jnp.int32

D_MODEL = 2048
ATTN_HEADS = 16
ATTN_KV_HEADS = 4
ATTN_HEAD_DIM = 64
ATTN_BLOCK = 128
ROPE_THETA = 500000.0
ROPE_DIM = ATTN_HEAD_DIM // 4
DN_QK_HEADS = 4
DN_V_HEADS = 8
DN_K_DIM = 128
DN_V_DIM = 128
DN_CONV = 5
DN_CHUNK = 64
N_EXPERTS = 32
TOP_K = 4
D_FF = D_MODEL
SWIGLU_LIMIT = 7.0
SWIGLU_ALPHA = 1.702
NORM_EPS = 1e-6
NEG_INF = -1e30

ATTN_Q_W = ATTN_HEADS * ATTN_HEAD_DIM
ATTN_KV_W = ATTN_KV_HEADS * ATTN_HEAD_DIM
DN_QK_W = DN_QK_HEADS * DN_K_DIM
DN_V_W = DN_V_HEADS * DN_V_DIM
DN_CONV_W = 2 * DN_QK_W + DN_V_W

_OFF_AQ = 0
_OFF_AK = _OFF_AQ + ATTN_Q_W
_OFF_AV = _OFF_AK + ATTN_KV_W
_OFF_DNQKV = _OFF_AV + ATTN_KV_W
_OFF_DNZ = _OFF_DNQKV + DN_CONV_W
_OFF_DNB = _OFF_DNZ + DN_V_W
_OFF_DNA = _OFF_DNB + 2 * DN_V_HEADS
_OFF_GA = _OFF_DNA + 2 * DN_V_HEADS
_OFF_GD = _OFF_GA + D_MODEL

P_GA = 0
P_GD = P_GA + D_MODEL
P_DNQKV = P_GD + D_MODEL
P_AQ = P_DNQKV + DN_CONV_W
P_DNZ = P_AQ + ATTN_Q_W
P_AK = P_DNZ + DN_V_W
P_AV = P_AK + ATTN_KV_W
P_W = P_AV + ATTN_KV_W

LANES = 128
VMEM_LIMIT = 56 * 1024 * 1024

IN_TM = 1024
IN_TN = 512
DN_TB = 256
POST_TM = 256
MOE_TM = 512
MOE_TF = 512
SCATTER_R = 256
COMB_R = 128


def _cparams(sem):
    return pltpu.CompilerParams(dimension_semantics=sem, vmem_limit_bytes=VMEM_LIMIT)


def _sigmoid(x):
    return 1.0 / (1.0 + jnp.exp(-x))


def _softplus(x):
    return jnp.maximum(x, 0.0) + jnp.log(1.0 + jnp.exp(-jnp.abs(x)))


def _pack_halves(lo, hi):
    a = lax.bitcast_convert_type(lo.astype(bf16).astype(f32), u32)
    b = lax.bitcast_convert_type(hi.astype(bf16).astype(f32), u32)
    return (a >> 16) | (b & jnp.uint32(0xFFFF0000))


def _unpack_halves(w):
    lo = lax.bitcast_convert_type(w << 16, f32)
    hi = lax.bitcast_convert_type(w & jnp.uint32(0xFFFF0000), f32)
    return lo, hi


ROW_TILES = D_MODEL // 2 // LANES


def _row_tile(r0, n, s):
    return pl.ds(r0 * ROW_TILES + s, n, stride=ROW_TILES)


def _row_span(r, n=1):
    start = r * ROW_TILES
    if not isinstance(r, int):
        start = pl.multiple_of(start, ROW_TILES)
    return pl.ds(start, n * ROW_TILES)


def _store_row_tiles(ref, packed):
    for s in range(ROW_TILES):
        ref[_row_tile(0, packed.shape[0], s), :] = packed[:, s * LANES:(s + 1) * LANES]


def _inproj_kernel(x_ref, nw_ref, w_ref, b_ref, ws_ref, bs_ref, wst_ref, bst_ref,
                   o_ref, os_ref, ost_ref, h_ref):
    @pl.when(pl.program_id(1) == 0)
    def _():
        x = x_ref[...]
        ms = jnp.mean(x * x, axis=-1, keepdims=True)
        hb = (x * lax.rsqrt(ms + NORM_EPS) * nw_ref[...]).astype(bf16)
        h_ref[...] = hb
        os_ref[...] = jnp.dot(hb, ws_ref[...], preferred_element_type=f32) + bs_ref[...]
        ost_ref[...] = lax.dot_general(wst_ref[...], hb, (((1,), (1,)), ((), ())),
                                       preferred_element_type=f32) + bst_ref[...]

    acc = jnp.dot(h_ref[...], w_ref[...], preferred_element_type=f32)
    o_ref[...] = (acc + b_ref[...]).astype(bf16)


def _inproj(x, nw, w, b, ws, bs, wst, bst):
    T = x.shape[0]
    tm = min(IN_TM, T)
    grid = (T // tm, P_W // IN_TN)
    return pl.pallas_call(
        _inproj_kernel,
        grid=grid,
        in_specs=[
            pl.BlockSpec((tm, D_MODEL), lambda i, j: (i, 0)),
            pl.BlockSpec((1, D_MODEL), lambda i, j: (0, 0)),
            pl.BlockSpec((D_MODEL, IN_TN), lambda i, j: (0, j)),
            pl.BlockSpec((1, IN_TN), lambda i, j: (0, j)),
            pl.BlockSpec((D_MODEL, LANES), lambda i, j: (0, 0)),
            pl.BlockSpec((1, LANES), lambda i, j: (0, 0)),
            pl.BlockSpec((LANES, D_MODEL), lambda i, j: (0, 0)),
            pl.BlockSpec((LANES, 1), lambda i, j: (0, 0)),
        ],
        out_specs=[
            pl.BlockSpec((tm, IN_TN), lambda i, j: (i, j)),
            pl.BlockSpec((tm, LANES), lambda i, j: (i, 0)),
            pl.BlockSpec((LANES, tm), lambda i, j: (0, i)),
        ],
        out_shape=[
            jax.ShapeDtypeStruct((T, P_W), bf16),
            jax.ShapeDtypeStruct((T, LANES), f32),
            jax.ShapeDtypeStruct((LANES, T), f32),
        ],
        scratch_shapes=[pltpu.VMEM((tm, D_MODEL), bf16)],
        compiler_params=_cparams(("parallel", "arbitrary")),
        name="inproj",
    )(x, nw, w, b, ws, bs, wst, bst)


def _rope(x, cos, sin):
    lane = lax.broadcasted_iota(i32, x.shape, 1) % ATTN_HEAD_DIM
    half = ROPE_DIM // 2
    partner = jnp.where((lane >= half) & (lane < ROPE_DIM),
                        pltpu.roll(x, half, 1), pltpu.roll(x, LANES - half, 1))
    return x * cos + partner * sin


def _attn_kernel(sink_ref, q_ref, kp_ref, kc_ref, kn_ref, vp_ref, vc_ref, vn_ref,
                 cos_ref, sin_ref, o_ref, *, nb):
    n = pl.program_id(1)
    W = ATTN_BLOCK
    HD = ATTN_HEAD_DIM
    G = ATTN_HEADS // ATTN_KV_HEADS

    q0 = pl.multiple_of(n * W, W)
    cq, sq = cos_ref[pl.ds(q0, W), :], sin_ref[pl.ds(q0, W), :]
    qs = []
    for c in range(ATTN_Q_W // LANES):
        xq = q_ref[:, c * LANES:(c + 1) * LANES].astype(f32)
        qs.append((_rope(xq, cq, sq) * (HD ** -0.5)).astype(bf16))

    kblocks = (kp_ref, kc_ref, kn_ref)
    ks = [[], []]
    for bi in range(3):
        p0 = pl.multiple_of(jnp.clip(n + bi - 1, 0, nb - 1) * W, W)
        ck, sk = cos_ref[pl.ds(p0, W), :], sin_ref[pl.ds(p0, W), :]
        for c in range(ATTN_KV_W // LANES):
            xk = kblocks[bi][:, c * LANES:(c + 1) * LANES].astype(f32)
            ks[c].append(_rope(xk, ck, sk).astype(bf16))
    kcat = [jnp.concatenate(ks[c], axis=0) for c in range(2)]
    vcat = jnp.concatenate([vp_ref[...], vc_ref[...], vn_ref[...]], axis=0)

    r = lax.broadcasted_iota(i32, (W, 3 * W), 0)
    c = lax.broadcasted_iota(i32, (W, 3 * W), 1)
    valid = (c >= r) & (c <= r + 2 * W)
    valid = valid & ((n > 0) | (c >= W)) & ((n < nb - 1) | (c < 2 * W))
    valid4 = jnp.concatenate([valid] * G, axis=0)

    for kh in range(ATTN_KV_HEADS):
        kc_, ko = divmod(kh * HD, LANES)
        k_h = kcat[kc_][:, ko:ko + HD]
        v_h = vcat[:, kh * HD:(kh + 1) * HD]
        parts = []
        sink_parts = []
        for g in range(G):
            h = kh * G + g
            qc, qo = divmod(h * HD, LANES)
            parts.append(qs[qc][:, qo:qo + HD])
            sink_parts.append(jnp.full((W, 1), sink_ref[h], f32))
        qg = jnp.concatenate(parts, axis=0)
        sink = jnp.concatenate(sink_parts, axis=0)
        s = lax.dot_general(qg, k_h, (((1,), (1,)), ((), ())), preferred_element_type=f32)
        s = jnp.where(valid4, s, NEG_INF)
        m = jnp.maximum(jnp.max(s, axis=-1, keepdims=True), sink)
        p = jnp.exp(s - m)
        denom = jnp.sum(p, axis=-1, keepdims=True) + jnp.exp(sink - m)
        o = jnp.dot(p.astype(bf16), v_h, preferred_element_type=f32) / denom
        for g in range(G):
            h = kh * G + g
            o_ref[:, h * HD:(h + 1) * HD] = o[g * W:(g + 1) * W].astype(bf16)


def _attention(proj, sinks, cos_t, sin_t, nseq, seq):
    W = ATTN_BLOCK
    nb = seq // W
    T = nseq * seq
    qcol = P_AQ // ATTN_Q_W
    kcol = P_AK // ATTN_KV_W
    vcol = P_AV // ATTN_KV_W

    def kv_spec(col, d):
        return pl.BlockSpec(
            (W, ATTN_KV_W), lambda b, n: (b * nb + jnp.clip(n + d, 0, nb - 1), col))

    return pl.pallas_call(
        functools.partial(_attn_kernel, nb=nb),
        grid=(nseq, nb),
        in_specs=[
            pl.BlockSpec(memory_space=pltpu.SMEM),
            pl.BlockSpec((W, ATTN_Q_W), lambda b, n: (b * nb + n, qcol)),
            kv_spec(kcol, -1), kv_spec(kcol, 0), kv_spec(kcol, 1),
            kv_spec(vcol, -1), kv_spec(vcol, 0), kv_spec(vcol, 1),
            pl.BlockSpec(memory_space=pltpu.VMEM),
            pl.BlockSpec(memory_space=pltpu.VMEM),
        ],
        out_specs=pl.BlockSpec((W, ATTN_Q_W), lambda b, n: (b * nb + n, 0)),
        out_shape=jax.ShapeDtypeStruct((T, ATTN_Q_W), bf16),
        compiler_params=_cparams(("parallel", "parallel")),
        name="window_attn",
    )(sinks, proj, proj, proj, proj, proj, proj, proj, cos_t, sin_t)


def _dnconv_kernel(x_ref, w_ref, o_ref, pad_ref, *, seq, rt):
    j = pl.program_id(1)
    half = DN_CONV // 2
    cw = x_ref.shape[1]
    pad_ref[0:8, :] = jnp.zeros((8, cw), f32)
    pad_ref[seq + 8:seq + 16, :] = jnp.zeros((8, cw), f32)
    pad_ref[8:seq + 8, :] = x_ref[...].astype(f32)
    w = w_ref[...]
    qk_blocks = 2 * DN_QK_W // cw
    q_blocks = DN_QK_W // cw
    for r0 in range(0, seq, rt):
        acc = jnp.zeros((rt, cw), f32)
        for t in range(DN_CONV):
            s0 = r0 + 8 - half + t
            acc = acc + pad_ref[s0:s0 + rt, :] * w[t:t + 1, :]
        y = acc * _sigmoid(acc)
        outs = []
        for hh in range(cw // DN_K_DIM):
            seg = y[:, hh * DN_K_DIM:(hh + 1) * DN_K_DIM]
            inv = lax.rsqrt(jnp.sum(seg * seg, axis=-1, keepdims=True) + NORM_EPS)
            fac = jnp.where(j < q_blocks, inv * (DN_K_DIM ** -0.5),
                            jnp.where(j < qk_blocks, inv, jnp.ones_like(inv)))
            outs.append(seg * fac)
        o_ref[r0:r0 + rt, :] = jnp.concatenate(outs, axis=1).astype(bf16)


def _dnconv(proj, conv_w, nseq, seq):
    cw = 256
    T = nseq * seq
    col0 = P_DNQKV // cw
    rt = min(512, seq)
    return pl.pallas_call(
        functools.partial(_dnconv_kernel, seq=seq, rt=rt),
        grid=(nseq, DN_CONV_W // cw),
        in_specs=[
            pl.BlockSpec((seq, cw), lambda b, j: (b, col0 + j)),
            pl.BlockSpec((DN_CONV, cw), lambda b, j: (0, j)),
        ],
        out_specs=pl.BlockSpec((seq, cw), lambda b, j: (b, j)),
        out_shape=jax.ShapeDtypeStruct((T, DN_CONV_W), bf16),
        scratch_shapes=[pltpu.VMEM((seq + 16, cw), f32)],
        compiler_params=_cparams(("parallel", "parallel")),
        name="dn_conv",
    )(proj, conv_w)


def _mm(a, b):
    return jnp.dot(a.astype(bf16), b.astype(bf16), preferred_element_type=f32)


def _unit_tri_inverse_all(lmats, eye, blk_i, blk_j):
    same8 = (blk_i // 8) == (blk_j // 8)
    n1 = [jnp.where(same8, -l, 0.0) for l in lmats]
    p = [eye + n for n in n1]
    n2 = [_mm(n, n) for n in n1]
    p = [pp + _mm(pp, nn) for pp, nn in zip(p, n2)]
    n4 = [_mm(n, n) for n in n2]
    inv = [pp + _mm(pp, nn) for pp, nn in zip(p, n4)]
    for s in (8, 16, 32):
        coupling = ((blk_i // (2 * s)) == (blk_j // (2 * s))) & ((blk_i // s) != (blk_j // s))
        t = [_mm(jnp.where(coupling, l, 0.0), iv) for l, iv in zip(lmats, inv)]
        inv = [iv - _mm(iv, tt) for iv, tt in zip(inv, t)]
    return inv


def _dnprep_kernel(q_ref, k_ref, v_ref, sm_ref, smt_ref, ar_ref, dr_ref, ac_ref, dc_ref,
                   u_ref, w_ref, qd_ref, kd_ref, aqk_ref, egl_ref):
    hq = pl.program_id(1)
    C = DN_CHUNK
    P = 2 * C
    TB = q_ref.shape[0]
    nch = TB // C

    sm = sm_ref[...]
    beta_full = _sigmoid(sm)
    g_full = -jnp.exp(ar_ref[...]) * _softplus(sm + dr_ref[...])
    shift = (LANES - hq * 8) % LANES
    beta_c = pltpu.roll(beta_full, shift, 1)
    g_c = pltpu.roll(g_full, shift, 1)
    smt = smt_ref[...]
    g_r = -jnp.exp(ac_ref[...]) * _softplus(smt + dc_ref[...])

    ri = lax.broadcasted_iota(i32, (TB, TB), 0)
    ci = lax.broadcasted_iota(i32, (TB, TB), 1)
    same = (ri // C) == (ci // C)
    bi = lax.broadcasted_iota(i32, (P, P), 0)
    bj = lax.broadcasted_iota(i32, (P, P), 1)
    eye = (bi == bj).astype(f32)
    same_head = (bi // C) == (bj // C)

    hp = lax.Precision.HIGHEST
    gcum_c, gcum_r = [], []
    for d in range(2):
        tri = (same & ((ci <= ri) if d == 0 else (ci >= ri))).astype(f32)
        gcum_c.append(jnp.dot(tri, g_c, precision=hp, preferred_element_type=f32))
        gcum_r.append(lax.dot_general(g_r, tri, (((1,), (1,)), ((), ())), precision=hp,
                                      preferred_element_type=f32))

    k2, q2, kk2, qk2, v2 = [], [], [], [], []
    for c in range(nch):
        rows = slice(c * C, (c + 1) * C)
        k = k_ref[rows, :]
        q = q_ref[rows, :]
        kp = jnp.concatenate([k, k], axis=0)
        qp = jnp.concatenate([q, q], axis=0)
        k2.append(kp.astype(f32))
        q2.append(qp.astype(f32))
        kk2.append(lax.dot_general(kp, kp, (((1,), (1,)), ((), ())), preferred_element_type=f32))
        qk2.append(lax.dot_general(qp, kp, (((1,), (1,)), ((), ())), preferred_element_type=f32))
        v2.append(jnp.concatenate([v_ref[rows, :DN_V_DIM], v_ref[rows, DN_V_DIM:]], axis=0).astype(f32))

    inst = [(d, c) for d in range(2) for c in range(nch)]
    gcol, bcol, decay, lmats = [], [], [], []
    for d, c in inst:
        rows = slice(c * C, (c + 1) * C)
        b0, g0 = d * 2, 4 + d * 2
        gc = jnp.concatenate([gcum_c[d][rows, g0:g0 + 1], gcum_c[d][rows, g0 + 1:g0 + 2]], axis=0)
        gr = jnp.concatenate([gcum_r[d][g0:g0 + 1, rows], gcum_r[d][g0 + 1:g0 + 2, rows]], axis=1)
        bc = jnp.concatenate([beta_c[rows, b0:b0 + 1], beta_c[rows, b0 + 1:b0 + 2]], axis=0)
        incl = same_head & ((bj <= bi) if d == 0 else (bj >= bi))
        strict = same_head & ((bj < bi) if d == 0 else (bj > bi))
        dec = jnp.where(incl, jnp.exp(jnp.where(incl, gc - gr, 0.0)), 0.0)
        gcol.append(gc)
        bcol.append(bc)
        decay.append(dec)
        lmats.append(jnp.where(strict, kk2[c] * bc * dec, 0.0))

    invs = _unit_tri_inverse_all(lmats, eye, bi, bj)

    for n, (d, c) in enumerate(inst):
        rows = slice(c * C, (c + 1) * C)
        last = C - 1 if d == 0 else 0
        gc, bc = gcol[n], bcol[n]
        eg = jnp.exp(gc)
        rhs = jnp.concatenate([v2[c] * bc, k2[c] * (bc * eg)], axis=1)
        sol = _mm(invs[n], rhs)
        gl0 = gc[last:last + 1, :]
        gl1 = gc[C + last:C + last + 1, :]
        glast = jnp.concatenate([jnp.broadcast_to(gl0, (C, 1)), jnp.broadcast_to(gl1, (C, 1))], axis=0)
        qd = q2[c] * eg
        kd = k2[c] * jnp.exp(glast - gc)
        aqk = qk2[c] * decay[n]
        for hvl in range(2):
            pr = slice(hvl * C, (hvl + 1) * C)
            cols = slice(hvl * DN_V_DIM, (hvl + 1) * DN_V_DIM)
            u_ref[d, rows, cols] = sol[pr, :DN_V_DIM].astype(bf16)
            w_ref[d, rows, cols] = sol[pr, DN_V_DIM:].astype(bf16)
            qd_ref[d, rows, cols] = qd[pr].astype(bf16)
            kd_ref[d, rows, cols] = kd[pr].astype(bf16)
            aqk_ref[d, rows, pr] = aqk[pr, pr].astype(bf16)
        egl_ref[d, 0, c, 0:1, :] = jnp.broadcast_to(jnp.exp(gl0), (1, LANES))
        egl_ref[d, 0, c, 1:2, :] = jnp.broadcast_to(jnp.exp(gl1), (1, LANES))


def _dnprep(qkv, small, small_t, ar, dr, ac, dc):
    T = qkv.shape[0]
    TB = DN_TB
    nblk = T // TB
    nch = TB // DN_CHUNK
    kcol = DN_QK_W // DN_K_DIM
    vcol = 2 * DN_QK_W // (2 * DN_V_DIM)
    wide = lambda: pl.BlockSpec((2, TB, 2 * DN_V_DIM), lambda i, h: (0, i, h))
    big = jax.ShapeDtypeStruct((2, T, DN_V_W), bf16)
    return pl.pallas_call(
        _dnprep_kernel,
        grid=(nblk, DN_QK_HEADS),
        in_specs=[
            pl.BlockSpec((TB, DN_K_DIM), lambda i, h: (i, h)),
            pl.BlockSpec((TB, DN_K_DIM), lambda i, h: (i, kcol + h)),
            pl.BlockSpec((TB, 2 * DN_V_DIM), lambda i, h: (i, vcol + h)),
            pl.BlockSpec((TB, LANES), lambda i, h: (i, 0)),
            pl.BlockSpec((8, TB), lambda i, h: (h, i)),
            pl.BlockSpec((1, LANES), lambda i, h: (0, 0)),
            pl.BlockSpec((1, LANES), lambda i, h: (0, 0)),
            pl.BlockSpec((8, 1), lambda i, h: (h, 0)),
            pl.BlockSpec((8, 1), lambda i, h: (h, 0)),
        ],
        out_specs=[
            wide(), wide(), wide(), wide(),
            pl.BlockSpec((2, TB, 2 * DN_CHUNK), lambda i, h: (0, i, h)),
            pl.BlockSpec((2, 1, nch, 2, LANES), lambda i, h: (0, h, i, 0, 0)),
        ],
        out_shape=[
            big, big, big, big,
            jax.ShapeDtypeStruct((2, T, DN_QK_HEADS * 2 * DN_CHUNK), bf16),
            jax.ShapeDtypeStruct((2, DN_QK_HEADS, T // DN_CHUNK, 2, LANES), f32),
        ],
        compiler_params=_cparams(("parallel", "parallel")),
        name="dn_prep",
    )(qkv, qkv, qkv, small, small_t, ar, dr, ac, dc)


def _dnscan_kernel(uf, wf, qf, kf, af, ef, ub, wb, qb, kb, ab, eb, of_ref, ob_ref, s_ref):
    t = pl.program_id(2)
    C = DN_CHUNK
    nch = uf.shape[1] // C

    @pl.when(t == 0)
    def _():
        s_ref[...] = jnp.zeros(s_ref.shape, f32)

    dirs = ((uf, wf, qf, kf, af, ef, of_ref), (ub, wb, qb, kb, ab, eb, ob_ref))
    for step in range(nch):
        for d in range(2):
            u_r, w_r, q_r, k_r, a_r, e_r, o_r = dirs[d]
            c = step if d == 0 else nch - 1 - step
            rows = slice(c * C, (c + 1) * C)
            for hvl in range(2):
                cols = slice(hvl * DN_V_DIM, (hvl + 1) * DN_V_DIM)
                si = d * 2 + hvl
                s = s_ref[si]
                sb = s.astype(bf16)
                v_new = u_r[0, rows, cols].astype(f32) - jnp.dot(
                    w_r[0, rows, cols], sb, preferred_element_type=f32)
                vb = v_new.astype(bf16)
                o = jnp.dot(q_r[0, rows, cols], sb, preferred_element_type=f32) + jnp.dot(
                    a_r[0, rows, hvl * C:(hvl + 1) * C], vb, preferred_element_type=f32)
                upd = lax.dot_general(k_r[0, rows, cols], vb, (((0,), (0,)), ((), ())),
                                      preferred_element_type=f32)
                s_ref[si] = s * e_r[0, 0, c, hvl:hvl + 1, :] + upd
                o_r[rows, cols] = o


def _dnscan(u, w, qd, kd, aqk, egl, nseq, seq):
    TB = DN_TB
    nt = seq // TB
    nch = TB // DN_CHUNK
    T = nseq * seq

    def specs(d):
        if d == 0:
            blk = lambda b, h, t: b * nt + t
        else:
            blk = lambda b, h, t: b * nt + nt - 1 - t
        wide = lambda: pl.BlockSpec((1, TB, 2 * DN_V_DIM), lambda b, h, t: (d, blk(b, h, t), h))
        return [wide(), wide(), wide(), wide(),
                pl.BlockSpec((1, TB, 2 * DN_CHUNK), lambda b, h, t: (d, blk(b, h, t), h)),
                pl.BlockSpec((1, 1, nch, 2, LANES), lambda b, h, t: (d, h, blk(b, h, t), 0, 0))]

    def ospec(d):
        if d == 0:
            return pl.BlockSpec((TB, 2 * DN_V_DIM), lambda b, h, t: (b * nt + t, h))
        return pl.BlockSpec((TB, 2 * DN_V_DIM), lambda b, h, t: (b * nt + nt - 1 - t, h))

    args = (u, w, qd, kd, aqk, egl)
    return pl.pallas_call(
        _dnscan_kernel,
        grid=(nseq, DN_QK_HEADS, nt),
        in_specs=specs(0) + specs(1),
        out_specs=[ospec(0), ospec(1)],
        out_shape=[jax.ShapeDtypeStruct((T, DN_V_W), f32)] * 2,
        scratch_shapes=[pltpu.VMEM((4, DN_K_DIM, DN_V_DIM), f32)],
        compiler_params=_cparams(("parallel", "parallel", "arbitrary")),
        name="dn_scan",
    )(*args, *args)


def _post_kernel(x_ref, oa_ref, of_ref, ob_ref, z_ref, ga_ref, gd_ref, dnw_ref, wa_ref, wd_ref,
                 wo_ref, n2_ref, rw_ref, rb_ref, x2_ref, hp_ref, idx_ref, gate_ref):
    od = of_ref[...] + ob_ref[...]
    z = z_ref[...].astype(f32)
    parts = []
    for h in range(DN_V_HEADS):
        cols = slice(h * DN_V_DIM, (h + 1) * DN_V_DIM)
        seg = od[:, cols]
        seg = seg * lax.rsqrt(jnp.mean(seg * seg, axis=-1, keepdims=True) + NORM_EPS) * dnw_ref[...]
        zz = z[:, cols]
        parts.append((seg * (zz * _sigmoid(zz))).astype(bf16))
    odn = jnp.concatenate(parts, axis=1)
    ya = jnp.dot(oa_ref[...], wa_ref[...], preferred_element_type=f32)
    yd = jnp.dot(odn, wd_ref[...], preferred_element_type=f32)
    merged = _sigmoid(ga_ref[...].astype(f32)) * ya + _sigmoid(gd_ref[...].astype(f32)) * yd
    x2 = x_ref[...] + jnp.dot(merged.astype(bf16), wo_ref[...], preferred_element_type=f32)
    x2_ref[...] = x2
    h2 = x2 * lax.rsqrt(jnp.mean(x2 * x2, axis=-1, keepdims=True) + NORM_EPS) * n2_ref[...]
    half = D_MODEL // 2
    _store_row_tiles(hp_ref, _pack_halves(h2[:, :half], h2[:, half:]))

    logits = jnp.dot(h2, rw_ref[...], precision=lax.Precision.HIGHEST,
                     preferred_element_type=f32) + rb_ref[...]
    lane = lax.broadcasted_iota(i32, logits.shape, 1).astype(f32)
    vals, idxs = [], []
    cur = logits
    for _ in range(TOP_K):
        m = jnp.max(cur, axis=-1, keepdims=True)
        idx = jnp.min(jnp.where(cur == m, lane, float(LANES)), axis=-1, keepdims=True)
        vals.append(m)
        idxs.append(idx)
        cur = jnp.where(lane == idx, -jnp.inf, cur)
    exps = [jnp.exp(v - vals[0]) for v in vals]
    tot = exps[0] + exps[1] + exps[2] + exps[3]
    idx_out = jnp.zeros(logits.shape, f32)
    gate_out = jnp.zeros(logits.shape, f32)
    for kk in range(TOP_K):
        idx_out = jnp.where(lane == float(kk), idxs[kk], idx_out)
        gate_out = jnp.where(lane == float(kk), exps[kk] / tot, gate_out)
    idx_ref[...] = idx_out.astype(i32)
    gate_ref[...] = gate_out


def _post(x, o_attn, o_f, o_b, proj, dnw, wa, wd, wo, n2, rw, rb):
    T = x.shape[0]
    tm = min(POST_TM, T)
    zcol = P_DNZ // DN_V_W
    const = lambda shape: pl.BlockSpec(shape, lambda i: (0, 0))
    row = lambda width, col=0: pl.BlockSpec((tm, width), lambda i: (i, col))
    return pl.pallas_call(
        _post_kernel,
        grid=(T // tm,),
        in_specs=[
            row(D_MODEL), row(ATTN_Q_W), row(DN_V_W), row(DN_V_W),
            row(DN_V_W, zcol), row(D_MODEL, P_GA // D_MODEL), row(D_MODEL, P_GD // D_MODEL),
            const((1, DN_V_DIM)), const((ATTN_Q_W, D_MODEL)), const((DN_V_W, D_MODEL)),
            const((D_MODEL, D_MODEL)), const((1, D_MODEL)), const((D_MODEL, LANES)),
            const((1, LANES)),
        ],
        out_specs=[row(D_MODEL), pl.BlockSpec((tm * ROW_TILES, LANES), lambda i: (i, 0)),
                   row(LANES), row(LANES)],
        out_shape=[
            jax.ShapeDtypeStruct((T, D_MODEL), f32),
            jax.ShapeDtypeStruct((T * ROW_TILES, LANES), u32),
            jax.ShapeDtypeStruct((T, LANES), i32),
            jax.ShapeDtypeStruct((T, LANES), f32),
        ],
        compiler_params=_cparams(("parallel",)),
        name="post_mix",
    )(x, o_attn, o_f, o_b, proj, proj, proj, dnw, wa, wd, wo, n2, rw, rb)


def _scatter_kernel(poff_ref, pn_ref, pos_ref, h_ref, xs_ref, zero_ref, sem, zsem):
    R = h_ref.shape[0] // ROW_TILES
    nbits = MOE_TM.bit_length() - 1

    @pl.when(pl.program_id(0) == 0)
    def _():
        zero_ref[...] = jnp.zeros(zero_ref.shape, u32)

        def per_expert(e, carry):
            off = poff_ref[e]
            n = pn_ref[e]
            for bit in reversed(range(nbits)):
                size = 1 << bit
                take = ((n >> bit) & 1) == 1

                @pl.when(take)
                def _():
                    cp = pltpu.make_async_copy(zero_ref.at[_row_span(0, size)],
                                               xs_ref.at[_row_span(off, size)], zsem)
                    cp.start()
                    cp.wait()

                off = off + jnp.where(take, size, 0)
            return carry

        lax.fori_loop(0, N_EXPERTS, per_expert, 0)

        zrows = zero_ref.shape[0] // ROW_TILES

        def tail(j, carry):
            row = poff_ref[N_EXPERTS] + j * zrows
            cp = pltpu.make_async_copy(zero_ref, xs_ref.at[_row_span(row, zrows)], zsem)
            cp.start()
            cp.wait()
            return carry

        lax.fori_loop(0, pn_ref[N_EXPERTS], tail, 0)

    def body(a, carry):
        p = pos_ref[0, 0, a]
        pltpu.make_async_copy(h_ref.at[_row_span(a // TOP_K)], xs_ref.at[_row_span(p)], sem).start()
        return carry

    lax.fori_loop(0, TOP_K * R, body, 0, unroll=8)
    done = xs_ref.at[_row_span(0, TOP_K * R)]
    pltpu.make_async_copy(done, done, sem).wait()


def _scatter_rows(pad_off, pad_n, pos, hp, n_slots):
    T = hp.shape[0] // ROW_TILES
    R = min(SCATTER_R, T)
    grid_spec = pltpu.PrefetchScalarGridSpec(
        num_scalar_prefetch=2,
        grid=(T // R,),
        in_specs=[
            pl.BlockSpec((1, 1, TOP_K * R), lambda i, po, pn: (i, 0, 0), memory_space=pltpu.SMEM),
            pl.BlockSpec((R * ROW_TILES, LANES), lambda i, po, pn: (i, 0)),
        ],
        out_specs=pl.BlockSpec(memory_space=pl.ANY),
        scratch_shapes=[pltpu.VMEM((MOE_TM // 2 * ROW_TILES, LANES), u32),
                        pltpu.SemaphoreType.DMA(()), pltpu.SemaphoreType.DMA(())],
    )
    return pl.pallas_call(
        _scatter_kernel,
        grid_spec=grid_spec,
        out_shape=jax.ShapeDtypeStruct((n_slots * ROW_TILES, LANES), u32),
        compiler_params=_cparams(("arbitrary",)),
        name="moe_scatter",
    )(pad_off, pad_n, pos.reshape(T // R, 1, TOP_K * R), hp)


def _moe_kernel(te_ref, nu_ref, x_ref, wg_ref, wu_ref, bg_ref, bu_ref, wd_ref, bd_ref,
                o_ref, xb_ref, acc_ref):
    i = pl.program_id(0)
    f = pl.program_id(1)
    nf = pl.num_programs(1)
    half = D_MODEL // 2
    used = i < nu_ref[0]

    @pl.when(used & (f == 0))
    def _():
        for s in range(ROW_TILES):
            lo, hi = _unpack_halves(x_ref[_row_tile(0, xb_ref.shape[0], s), :])
            xb_ref[:, s * LANES:(s + 1) * LANES] = lo.astype(bf16)
            xb_ref[:, half + s * LANES:half + (s + 1) * LANES] = hi.astype(bf16)

    @pl.when(used)
    def _():
        xb = xb_ref[...]
        g = jnp.dot(xb, wg_ref[0], preferred_element_type=f32) + bg_ref[0]
        u = jnp.dot(xb, wu_ref[0], preferred_element_type=f32) + bu_ref[0]
        gate = jnp.minimum(g, SWIGLU_LIMIT)
        up = jnp.clip(u, -SWIGLU_LIMIT, SWIGLU_LIMIT)
        act = (up + 1.0) * (gate * _sigmoid(SWIGLU_ALPHA * gate))
        part = jnp.dot(act.astype(bf16), wd_ref[0], preferred_element_type=f32)

        @pl.when(f == 0)
        def _():
            acc_ref[...] = part

        @pl.when(f > 0)
        def _():
            acc_ref[...] += part

    @pl.when(used & (f == nf - 1))
    def _():
        y = acc_ref[...] + bd_ref[0]
        _store_row_tiles(o_ref, _pack_halves(y[:, :half], y[:, half:]))

    @pl.when(jnp.logical_not(used) & (f == nf - 1))
    def _():
        o_ref[...] = jnp.zeros(o_ref.shape, u32)


def _moe(tile_expert, n_used, xs, wg, wu, bg, bu, wd, bd):
    n_slots = xs.shape[0] // ROW_TILES
    tm, tf = MOE_TM, MOE_TF
    n_tiles = n_slots // tm
    nf = D_FF // tf

    def fidx(i, f, te, nu):
        return jnp.where(i < nu[0], f, nf - 1)

    grid_spec = pltpu.PrefetchScalarGridSpec(
        num_scalar_prefetch=2,
        grid=(n_tiles, nf),
        in_specs=[
            pl.BlockSpec((tm * ROW_TILES, LANES), lambda i, f, te, nu: (i, 0)),
            pl.BlockSpec((1, D_MODEL, tf), lambda i, f, te, nu: (te[i], 0, fidx(i, f, te, nu))),
            pl.BlockSpec((1, D_MODEL, tf), lambda i, f, te, nu: (te[i], 0, fidx(i, f, te, nu))),
            pl.BlockSpec((1, 1, tf), lambda i, f, te, nu: (te[i], 0, fidx(i, f, te, nu))),
            pl.BlockSpec((1, 1, tf), lambda i, f, te, nu: (te[i], 0, fidx(i, f, te, nu))),
            pl.BlockSpec((1, tf, D_MODEL), lambda i, f, te, nu: (te[i], fidx(i, f, te, nu), 0)),
            pl.BlockSpec((1, 1, D_MODEL), lambda i, f, te, nu: (te[i], 0, 0)),
        ],
        out_specs=pl.BlockSpec((tm * ROW_TILES, LANES), lambda i, f, te, nu: (i, 0)),
        scratch_shapes=[pltpu.VMEM((tm, D_MODEL), bf16), pltpu.VMEM((tm, D_MODEL), f32)],
    )
    return pl.pallas_call(
        _moe_kernel,
        grid_spec=grid_spec,
        out_shape=jax.ShapeDtypeStruct((n_slots * ROW_TILES, LANES), u32),
        compiler_params=_cparams(("arbitrary", "arbitrary")),
        name="moe_experts",
    )(tile_expert, n_used, xs, wg, wu, bg, bu, wd, bd)


def _combine_kernel(pos_ref, x2_ref, gate_ref, fw_ref, ys_ref, o_ref, buf_ref, sem):
    R = x2_ref.shape[0]
    half = D_MODEL // 2

    def body(a, carry):
        p = pos_ref[0, 0, a]
        pltpu.make_async_copy(ys_ref.at[_row_span(p)], buf_ref.at[_row_span(a)], sem).start()
        return carry

    lax.fori_loop(0, TOP_K * R, body, 0, unroll=8)
    pltpu.make_async_copy(ys_ref.at[_row_span(0, TOP_K * R)], buf_ref, sem).wait()

    gates = gate_ref[...]
    gk = [gates[:, k:k + 1] for k in range(TOP_K)]
    ss = jnp.zeros((R, 1), f32)
    for s in range(ROW_TILES):
        cl = slice(s * LANES, (s + 1) * LANES)
        ch = slice(half + s * LANES, half + (s + 1) * LANES)
        lo_acc = x2_ref[:, cl]
        hi_acc = x2_ref[:, ch]
        for k in range(TOP_K):
            lo, hi = _unpack_halves(buf_ref[_row_tile(k * R, R, s), :])
            lo_acc = lo_acc + gk[k] * lo
            hi_acc = hi_acc + gk[k] * hi
        ss = ss + jnp.sum(lo_acc * lo_acc, axis=-1, keepdims=True) + jnp.sum(
            hi_acc * hi_acc, axis=-1, keepdims=True)
        o_ref[:, cl] = lo_acc
        o_ref[:, ch] = hi_acc
    inv = lax.rsqrt(ss / D_MODEL + NORM_EPS)
    o_ref[...] = o_ref[...] * inv * fw_ref[...]


def _combine(pos, x2, gates, fw, ys):
    T = x2.shape[0]
    R = min(COMB_R, T)
    return pl.pallas_call(
        _combine_kernel,
        grid=(T // R,),
        in_specs=[
            pl.BlockSpec((1, 1, TOP_K * R), lambda i: (i, 0, 0), memory_space=pltpu.SMEM),
            pl.BlockSpec((R, D_MODEL), lambda i: (i, 0)),
            pl.BlockSpec((R, LANES), lambda i: (i, 0)),
            pl.BlockSpec((1, D_MODEL), lambda i: (0, 0)),
            pl.BlockSpec(memory_space=pl.ANY),
        ],
        out_specs=pl.BlockSpec((R, D_MODEL), lambda i: (i, 0)),
        out_shape=jax.ShapeDtypeStruct((T, D_MODEL), f32),
        scratch_shapes=[pltpu.VMEM((TOP_K * R * ROW_TILES, LANES), u32), pltpu.SemaphoreType.DMA(())],
        compiler_params=_cparams(("arbitrary",)),
        name="moe_combine",
    )(pos, x2, gates, fw, ys)


def _dn_channels(t):
    lead = t.shape[:-1]
    t = t.reshape(lead + (2, 2, DN_QK_HEADS, 2))
    perm = tuple(range(len(lead))) + tuple(len(lead) + a for a in (2, 0, 1, 3))
    return t.transpose(perm).reshape(lead + (4 * DN_V_HEADS,))


def _pad_lanes(t):
    pad = [(0, 0)] * (t.ndim - 1) + [(0, LANES - t.shape[-1])]
    return jnp.pad(t, pad)


def _deint_kernel(w_ref, p_ref, g_ref, u_ref):
    for c in range(w_ref.shape[2] // (2 * LANES)):
        w = w_ref[0, :, c * 2 * LANES:(c + 1) * 2 * LANES].astype(bf16)
        r = jnp.dot(w, p_ref[...], preferred_element_type=f32)
        g_ref[0, :, c * LANES:(c + 1) * LANES] = r[:, :LANES].astype(bf16)
        u_ref[0, :, c * LANES:(c + 1) * LANES] = r[:, LANES:].astype(bf16)


def _deinterleave(w):
    E, D, F2 = w.shape
    rb, cb = 1024, 1024
    i = jnp.arange(2 * LANES)
    src = jnp.where(i < LANES, 2 * i, 2 * (i - LANES) + 1)
    perm = (jnp.arange(2 * LANES)[:, None] == src[None, :]).astype(bf16)
    out = jax.ShapeDtypeStruct((E, D, F2 // 2), bf16)
    return pl.pallas_call(
        _deint_kernel,
        grid=(E, D // rb, F2 // cb),
        in_specs=[pl.BlockSpec((1, rb, cb), lambda e, r, c: (e, r, c)),
                  pl.BlockSpec((2 * LANES, 2 * LANES), lambda e, r, c: (0, 0))],
        out_specs=[pl.BlockSpec((1, rb, cb // 2), lambda e, r, c: (e, r, c))] * 2,
        out_shape=[out, out],
        compiler_params=_cparams(("parallel", "parallel", "parallel")),
        name="moe_deinterleave",
    )(w, perm)


def _routing(top_idx, T, tm):
    A = T * TOP_K
    n_tiles = A // tm + N_EXPERTS
    n_slots = n_tiles * tm
    flat_e = top_idx.reshape(-1)
    onehot = (flat_e[:, None] == jnp.arange(N_EXPERTS, dtype=i32)[None, :]).astype(i32)
    csum = jnp.cumsum(onehot, axis=0)
    counts = csum[-1]
    padded = ((counts + tm - 1) // tm) * tm
    padded_end = jnp.cumsum(padded)
    padded_start = padded_end - padded
    pos = jnp.sum(onehot * (csum - 1 + padded_start[None, :]), axis=1).astype(i32)
    tile_start = jnp.arange(n_tiles, dtype=i32) * tm
    tile_expert = jnp.minimum(
        jnp.sum(padded_end[None, :] <= tile_start[:, None], axis=1), N_EXPERTS - 1).astype(i32)
    n_used = (padded_end[-1] // tm).astype(i32).reshape(1)
    pad_off = jnp.concatenate([padded_start + counts, padded_end[-1:]]).astype(i32)
    pad_n = jnp.concatenate([padded - counts, (n_slots - padded_end[-1:]) // (tm // 2)]).astype(i32)
    return pos, pad_off, pad_n, tile_expert, n_used, n_slots


def _layer(x, nseq, seq, norm1_w, w_in, b_in, attn_sinks, dn_conv_w, dn_a_log, dn_dt_bias,
           dn_norm_w, w_attn_o, w_dn_o, w_out, norm2_w, router_w, router_b, w_gate_up, b_gate_up,
           w_down, b_down, final_norm_w):
    T = nseq * seq

    order = ([_OFF_GA, D_MODEL], [_OFF_GD, D_MODEL], [_OFF_DNQKV, DN_CONV_W], [_OFF_AQ, ATTN_Q_W],
             [_OFF_DNZ, DN_V_W], [_OFF_AK, ATTN_KV_W], [_OFF_AV, ATTN_KV_W])
    w_main = jnp.concatenate([w_in[:, o:o + n] for o, n in order], axis=1).astype(bf16)
    b_main = jnp.concatenate([b_in[o:o + n] for o, n in order])[None, :]
    nsm = 4 * DN_V_HEADS
    w_small = _pad_lanes(_dn_channels(w_in[:, _OFF_DNB:_OFF_DNB + nsm])).astype(bf16)
    b_small = _pad_lanes(_dn_channels(b_in[_OFF_DNB:_OFF_DNB + nsm]))
    zeros_b = jnp.zeros((2 * DN_V_HEADS,), f32)
    a_ch = _pad_lanes(_dn_channels(jnp.concatenate([zeros_b, dn_a_log.reshape(-1)])))
    d_ch = _pad_lanes(_dn_channels(jnp.concatenate([zeros_b, dn_dt_bias.reshape(-1)])))

    proj, small, small_t = _inproj(x, norm1_w[None, :], w_main, b_main, w_small, b_small[None, :],
                                   w_small.T, b_small[:, None])

    pos = jnp.arange(seq, dtype=f32)
    inv_freq = 1.0 / (ROPE_THETA ** (jnp.arange(0, ROPE_DIM, 2, dtype=f32) / ROPE_DIM))
    ang = pos[:, None] * inv_freq[None, :]
    cos, sin = jnp.cos(ang), jnp.sin(ang)
    half = ROPE_DIM // 2
    ones_tail = jnp.ones((seq, ATTN_HEAD_DIM - ROPE_DIM), f32)
    cos_head = jnp.concatenate([cos, cos, ones_tail], axis=1)
    sin_head = jnp.concatenate([-sin, sin, 0.0 * ones_tail], axis=1)
    cos_t = jnp.concatenate([cos_head, cos_head], axis=1)
    sin_t = jnp.concatenate([sin_head, sin_head], axis=1)

    o_attn = _attention(proj, attn_sinks, cos_t, sin_t, nseq, seq)

    qkv = _dnconv(proj, dn_conv_w, nseq, seq)
    u, w, qd, kd, aqk, egl = _dnprep(qkv, small, small_t, a_ch[None, :], d_ch[None, :],
                                     a_ch[:, None], d_ch[:, None])
    o_f, o_b = _dnscan(u, w, qd, kd, aqk, egl, nseq, seq)

    rw = jnp.zeros((D_MODEL, LANES), f32).at[:, :N_EXPERTS].set(router_w)
    rb = jnp.full((LANES,), NEG_INF, f32).at[:N_EXPERTS].set(router_b)
    x2, hp, top_idx, top_gate = _post(
        x, o_attn, o_f, o_b, proj, dn_norm_w[None, :], w_attn_o.astype(bf16), w_dn_o.astype(bf16),
        w_out.astype(bf16), norm2_w[None, :], rw, rb[None, :])

    pos_a, pad_off, pad_n, tile_expert, n_used, n_slots = _routing(top_idx[:, :TOP_K], T, MOE_TM)
    xs = _scatter_rows(pad_off, pad_n, pos_a, hp, n_slots)
    wg, wu = _deinterleave(w_gate_up)
    b_pairs = b_gate_up.reshape(N_EXPERTS, 1, D_FF, 2)
    bg, bu = b_pairs[..., 0], b_pairs[..., 1]
    ys = _moe(tile_expert, n_used, xs, wg, wu, bg, bu, w_down.astype(bf16), b_down[:, None, :])
    R = min(COMB_R, T)
    pos_tiles = pos_a.reshape(T // R, R, TOP_K).transpose(0, 2, 1).reshape(T // R, 1, TOP_K * R)
    return _combine(pos_tiles, x2, top_gate, final_norm_w[None, :], ys)


def kernel(x_prompt, x_sample, norm1_w, w_in, b_in, attn_sinks, dn_conv_w, dn_a_log, dn_dt_bias,
           dn_norm_w, w_attn_o, w_dn_o, w_out, norm2_w, router_w, router_b, w_gate_up, b_gate_up,
           w_down, b_down, final_norm_w):
    bp, seq, d = x_prompt.shape
    bs = x_sample.shape[0]
    assert x_sample.shape[1] == seq and d == D_MODEL
    nseq = bp + bs
    x = jnp.concatenate([x_prompt.reshape(bp * seq, d), x_sample.reshape(bs * seq, d)], axis=0)
    y = _layer(x, nseq, seq, norm1_w[0], w_in[0], b_in[0], attn_sinks[0], dn_conv_w[0], dn_a_log[0],
               dn_dt_bias[0], dn_norm_w[0], w_attn_o[0], w_dn_o[0], w_out[0], norm2_w[0],
               router_w[0], router_b[0], w_gate_up[0], b_gate_up[0], w_down[0], b_down[0],
               final_norm_w)
    return (y[:bp * seq].reshape(bp, seq, d), y[bp * seq:].reshape(bs, seq, d))
```

```python
import functools
import math

import jax
import jax.numpy as jnp
from jax import lax
from jax.experimental import pallas as pl
from jax.experimental.pallas import tpu as pltpu

f32 = jnp.float32
bf16 = jnp.bfloat16
u32 = jnp.uint32
i32 = jnp.int32

D_MODEL = 2048
ATTN_HEADS = 16
ATTN_KV_HEADS = 4
ATTN_HEAD_DIM = 64
ATTN_BLOCK = 128
ROPE_THETA = 500000.0
ROPE_DIM = ATTN_HEAD_DIM // 4
DN_QK_HEADS = 4
DN_V_HEADS = 8
DN_K_DIM = 128
DN_V_DIM = 128
DN_CONV = 5
DN_CHUNK = 64
N_EXPERTS = 32
TOP_K = 4
D_FF = D_MODEL
SWIGLU_LIMIT = 7.0
SWIGLU_ALPHA = 1.702
NORM_EPS = 1e-6
NEG_INF = -1e30

ATTN_Q_W = ATTN_HEADS * ATTN_HEAD_DIM
ATTN_KV_W = ATTN_KV_HEADS * ATTN_HEAD_DIM
DN_QK_W = DN_QK_HEADS * DN_K_DIM
DN_V_W = DN_V_HEADS * DN_V_DIM
DN_CONV_W = 2 * DN_QK_W + DN_V_W

_OFF_AQ = 0
_OFF_AK = _OFF_AQ + ATTN_Q_W
_OFF_AV = _OFF_AK + ATTN_KV_W
_OFF_DNQKV = _OFF_AV + ATTN_KV_W
_OFF_DNZ = _OFF_DNQKV + DN_CONV_W
_OFF_DNB = _OFF_DNZ + DN_V_W
_OFF_DNA = _OFF_DNB + 2 * DN_V_HEADS
_OFF_GA = _OFF_DNA + 2 * DN_V_HEADS
_OFF_GD = _OFF_GA + D_MODEL

P_GA = 0
P_GD = P_GA + D_MODEL
P_DNQKV = P_GD + D_MODEL
P_AQ = P_DNQKV + DN_CONV_W
P_DNZ = P_AQ + ATTN_Q_W
P_AK = P_DNZ + DN_V_W
P_AV = P_AK + ATTN_KV_W
P_W = P_AV + ATTN_KV_W

LANES = 128
VMEM_LIMIT = 56 * 1024 * 1024

IN_TM = 1024
IN_TN = 512
DN_TB = 256
POST_TM = 256
MOE_TM = 512
MOE_TF = 1024
SCATTER_R = 256
COMB_R = 128


def _cparams(sem):
    return pltpu.CompilerParams(dimension_semantics=sem, vmem_limit_bytes=VMEM_LIMIT)


def _sigmoid(x):
    return 1.0 / (1.0 + jnp.exp(-x))


def _softplus(x):
    return jnp.maximum(x, 0.0) + jnp.log(1.0 + jnp.exp(-jnp.abs(x)))


def _pack_halves(lo, hi):
    a = lax.bitcast_convert_type(lo.astype(bf16).astype(f32), u32)
    b = lax.bitcast_convert_type(hi.astype(bf16).astype(f32), u32)
    return (a >> 16) | (b & jnp.uint32(0xFFFF0000))


def _unpack_halves(w):
    lo = lax.bitcast_convert_type(w << 16, f32)
    hi = lax.bitcast_convert_type(w & jnp.uint32(0xFFFF0000), f32)
    return lo, hi


ROW_TILES = D_MODEL // 2 // LANES


def _row_tile(r0, n, s):
    start = r0 * ROW_TILES
    if not isinstance(r0, int):
        start = pl.multiple_of(start, ROW_TILES)
    return pl.ds(start + s, n, stride=ROW_TILES)


def _row_span(r, n=1):
    start = r * ROW_TILES
    if not isinstance(r, int):
        start = pl.multiple_of(start, ROW_TILES)
    return pl.ds(start, n * ROW_TILES)


def _store_row_tiles(ref, packed):
    for s in range(ROW_TILES):
        ref[_row_tile(0, packed.shape[0], s), :] = packed[:, s * LANES:(s + 1) * LANES]


def _group_row_specs(tm, width, npb):
    first = pl.BlockSpec((tm, width), lambda i, *_: (jnp.minimum(i, npb - 1), 0))
    second = pl.BlockSpec((tm, width), lambda i, *_: (jnp.maximum(i - npb, 0), 0))
    return [first, second]


def _inproj_kernel(xa_ref, xb_ref, nw_ref, w_ref, b_ref, ws_ref, bs_ref, wst_ref, bst_ref,
                   o_ref, os_ref, ost_ref, h_ref, *, npb):
    @pl.when(pl.program_id(1) == 0)
    def _():
        x = jnp.where(pl.program_id(0) < npb, xa_ref[...], xb_ref[...])
        ms = jnp.mean(x * x, axis=-1, keepdims=True)
        hb = (x * lax.rsqrt(ms + NORM_EPS) * nw_ref[...]).astype(bf16)
        h_ref[...] = hb
        os_ref[...] = jnp.dot(hb, ws_ref[...], preferred_element_type=f32) + bs_ref[...]
        ost_ref[...] = lax.dot_general(wst_ref[...], hb, (((1,), (1,)), ((), ())),
                                       preferred_element_type=f32) + bst_ref[...]

    acc = jnp.dot(h_ref[...], w_ref[...], preferred_element_type=f32)
    o_ref[...] = (acc + b_ref[...]).astype(bf16)


def _inproj(xa, xb, nw, w, b, ws, bs, wst, bst):
    T = xa.shape[0] + xb.shape[0]
    tm = min(IN_TM, xa.shape[0], xb.shape[0])
    assert xa.shape[0] % tm == 0 and xb.shape[0] % tm == 0
    npb = xa.shape[0] // tm
    grid = (T // tm, P_W // IN_TN)
    return pl.pallas_call(
        functools.partial(_inproj_kernel, npb=npb),
        grid=grid,
        in_specs=_group_row_specs(tm, D_MODEL, npb) + [
            pl.BlockSpec((1, D_MODEL), lambda i, j: (0, 0)),
            pl.BlockSpec((D_MODEL, IN_TN), lambda i, j: (0, j)),
            pl.BlockSpec((1, IN_TN), lambda i, j: (0, j)),
            pl.BlockSpec((D_MODEL, LANES), lambda i, j: (0, 0)),
            pl.BlockSpec((1, LANES), lambda i, j: (0, 0)),
            pl.BlockSpec((LANES, D_MODEL), lambda i, j: (0, 0)),
            pl.BlockSpec((LANES, 1), lambda i, j: (0, 0)),
        ],
        out_specs=[
            pl.BlockSpec((tm, IN_TN), lambda i, j: (i, j)),
            pl.BlockSpec((tm, LANES), lambda i, j: (i, 0)),
            pl.BlockSpec((LANES, tm), lambda i, j: (0, i)),
        ],
        out_shape=[
            jax.ShapeDtypeStruct((T, P_W), bf16),
            jax.ShapeDtypeStruct((T, LANES), f32),
            jax.ShapeDtypeStruct((LANES, T), f32),
        ],
        scratch_shapes=[pltpu.VMEM((tm, D_MODEL), bf16)],
        compiler_params=_cparams(("parallel", "arbitrary")),
        name="inproj",
    )(xa, xb, nw, w, b, ws, bs, wst, bst)


def _rope(x, cos, sin):
    lane = lax.broadcasted_iota(i32, x.shape, 1) % ATTN_HEAD_DIM
    half = ROPE_DIM // 2
    partner = jnp.where((lane >= half) & (lane < ROPE_DIM),
                        pltpu.roll(x, half, 1), pltpu.roll(x, LANES - half, 1))
    return x * cos + partner * sin


def _attn_kernel(sink_ref, q_ref, kp_ref, kc_ref, kn_ref, vp_ref, vc_ref, vn_ref,
                 cos_ref, sin_ref, o_ref, *, nb):
    n = pl.program_id(1)
    W = ATTN_BLOCK
    HD = ATTN_HEAD_DIM
    G = ATTN_HEADS // ATTN_KV_HEADS

    q0 = pl.multiple_of(n * W, W)
    cq, sq = cos_ref[pl.ds(q0, W), :], sin_ref[pl.ds(q0, W), :]
    qs = []
    for c in range(ATTN_Q_W // LANES):
        xq = q_ref[:, c * LANES:(c + 1) * LANES].astype(f32)
        qs.append((_rope(xq, cq, sq) * (HD ** -0.5)).astype(bf16))

    kblocks = (kp_ref, kc_ref, kn_ref)
    ks = [[], []]
    for bi in range(3):
        p0 = pl.multiple_of(jnp.clip(n + bi - 1, 0, nb - 1) * W, W)
        ck, sk = cos_ref[pl.ds(p0, W), :], sin_ref[pl.ds(p0, W), :]
        for c in range(ATTN_KV_W // LANES):
            xk = kblocks[bi][:, c * LANES:(c + 1) * LANES].astype(f32)
            ks[c].append(_rope(xk, ck, sk).astype(bf16))
    kcat = [jnp.concatenate(ks[c], axis=0) for c in range(2)]
    vcat = jnp.concatenate([vp_ref[...], vc_ref[...], vn_ref[...]], axis=0)

    r = lax.broadcasted_iota(i32, (W, 3 * W), 0)
    c = lax.broadcasted_iota(i32, (W, 3 * W), 1)
    valid = (c >= r) & (c <= r + 2 * W)
    valid = valid & ((n > 0) | (c >= W)) & ((n < nb - 1) | (c < 2 * W))
    valid4 = jnp.concatenate([valid] * G, axis=0)

    scores, sinks = [], []
    for kh in range(ATTN_KV_HEADS):
        kc_, ko = divmod(kh * HD, LANES)
        k_h = kcat[kc_][:, ko:ko + HD]
        parts = []
        sink_parts = []
        for g in range(G):
            h = kh * G + g
            qc, qo = divmod(h * HD, LANES)
            parts.append(qs[qc][:, qo:qo + HD])
            sink_parts.append(jnp.full((W, 1), sink_ref[h], f32))
        qg = jnp.concatenate(parts, axis=0)
        sinks.append(jnp.concatenate(sink_parts, axis=0))
        scores.append(lax.dot_general(qg, k_h, (((1,), (1,)), ((), ())), preferred_element_type=f32))
    probs, rdenoms = [], []
    for s, sink in zip(scores, sinks):
        s = jnp.where(valid4, s, NEG_INF)
        m = jnp.maximum(jnp.max(s, axis=-1, keepdims=True), sink)
        p = jnp.exp(s - m)
        rdenoms.append(1.0 / (jnp.sum(p, axis=-1, keepdims=True) + jnp.exp(sink - m)))
        probs.append(p.astype(bf16))
    outs = [jnp.dot(p, vcat[:, kh * HD:(kh + 1) * HD], preferred_element_type=f32)
            for kh, p in enumerate(probs)]
    for kh in range(ATTN_KV_HEADS):
        o = outs[kh] * rdenoms[kh]
        for g in range(G):
            h = kh * G + g
            o_ref[:, h * HD:(h + 1) * HD] = o[g * W:(g + 1) * W].astype(bf16)


def _attention(proj, sinks, cos_t, sin_t, nseq, seq):
    W = ATTN_BLOCK
    nb = seq // W
    T = nseq * seq
    qcol = P_AQ // ATTN_Q_W
    kcol = P_AK // ATTN_KV_W
    vcol = P_AV // ATTN_KV_W

    def kv_spec(col, d):
        return pl.BlockSpec(
            (W, ATTN_KV_W), lambda b, n: (b * nb + jnp.clip(n + d, 0, nb - 1), col))

    return pl.pallas_call(
        functools.partial(_attn_kernel, nb=nb),
        grid=(nseq, nb),
        in_specs=[
            pl.BlockSpec(memory_space=pltpu.SMEM),
            pl.BlockSpec((W, ATTN_Q_W), lambda b, n: (b * nb + n, qcol)),
            kv_spec(kcol, -1), kv_spec(kcol, 0), kv_spec(kcol, 1),
            kv_spec(vcol, -1), kv_spec(vcol, 0), kv_spec(vcol, 1),
            pl.BlockSpec(memory_space=pltpu.VMEM),
            pl.BlockSpec(memory_space=pltpu.VMEM),
        ],
        out_specs=pl.BlockSpec((W, ATTN_Q_W), lambda b, n: (b * nb + n, 0)),
        out_shape=jax.ShapeDtypeStruct((T, ATTN_Q_W), bf16),
        compiler_params=_cparams(("parallel", "parallel")),
        name="window_attn",
    )(sinks, proj, proj, proj, proj, proj, proj, proj, cos_t, sin_t)


def _dnconv_kernel(x_ref, w_ref, o_ref, pad_ref, *, seq, rt):
    j = pl.program_id(1)
    half = DN_CONV // 2
    cw = x_ref.shape[1]
    pad_ref[0:8, :] = jnp.zeros((8, cw), f32)
    pad_ref[seq + 8:seq + 16, :] = jnp.zeros((8, cw), f32)
    pad_ref[8:seq + 8, :] = x_ref[...].astype(f32)
    w = w_ref[...]
    qk_blocks = 2 * DN_QK_W // cw
    q_blocks = DN_QK_W // cw
    for r0 in range(0, seq, rt):
        acc = jnp.zeros((rt, cw), f32)
        for t in range(DN_CONV):
            s0 = r0 + 8 - half + t
            acc = acc + pad_ref[s0:s0 + rt, :] * w[t:t + 1, :]
        y = acc * _sigmoid(acc)
        outs = []
        for hh in range(cw // DN_K_DIM):
            seg = y[:, hh * DN_K_DIM:(hh + 1) * DN_K_DIM]
            inv = lax.rsqrt(jnp.sum(seg * seg, axis=-1, keepdims=True) + NORM_EPS)
            fac = jnp.where(j < q_blocks, inv * (DN_K_DIM ** -0.5),
                            jnp.where(j < qk_blocks, inv, jnp.ones_like(inv)))
            outs.append(seg * fac)
        o_ref[r0:r0 + rt, :] = jnp.concatenate(outs, axis=1).astype(bf16)


def _dnconv(proj, conv_w, nseq, seq):
    cw = 256
    T = nseq * seq
    col0 = P_DNQKV // cw
    rt = min(512, seq)
    return pl.pallas_call(
        functools.partial(_dnconv_kernel, seq=seq, rt=rt),
        grid=(nseq, DN_CONV_W // cw),
        in_specs=[
            pl.BlockSpec((seq, cw), lambda b, j: (b, col0 + j)),
            pl.BlockSpec((DN_CONV, cw), lambda b, j: (0, j)),
        ],
        out_specs=pl.BlockSpec((seq, cw), lambda b, j: (b, j)),
        out_shape=jax.ShapeDtypeStruct((T, DN_CONV_W), bf16),
        scratch_shapes=[pltpu.VMEM((seq + 16, cw), f32)],
        compiler_params=_cparams(("parallel", "parallel")),
        name="dn_conv",
    )(proj, conv_w)


def _mm(a, b):
    return jnp.dot(a.astype(bf16), b.astype(bf16), preferred_element_type=f32)


def _unit_tri_inverse_all(lmats, eye, blk_i, blk_j):
    same8 = (blk_i // 8) == (blk_j // 8)
    n1 = [jnp.where(same8, -l, 0.0) for l in lmats]
    p = [eye + n for n in n1]
    n2 = [_mm(n, n) for n in n1]
    p = [pp + _mm(pp, nn) for pp, nn in zip(p, n2)]
    n4 = [_mm(n, n) for n in n2]
    inv = [pp + _mm(pp, nn) for pp, nn in zip(p, n4)]
    for s in (8, 16, 32):
        coupling = ((blk_i // (2 * s)) == (blk_j // (2 * s))) & ((blk_i // s) != (blk_j // s))
        t = [_mm(jnp.where(coupling, l, 0.0), iv) for l, iv in zip(lmats, inv)]
        inv = [iv - _mm(iv, tt) for iv, tt in zip(inv, t)]
    return inv


def _dnprep_kernel(q_ref, k_ref, v_ref, sm_ref, smt_ref, ar_ref, dr_ref, ac_ref, dc_ref,
                   u_ref, w_ref, qd_ref, kd_ref, aqk_ref, egl_ref):
    hq = pl.program_id(1)
    C = DN_CHUNK
    P = 2 * C
    TB = q_ref.shape[0]
    nch = TB // C

    sm = sm_ref[...]
    beta_full = _sigmoid(sm)
    g_full = -jnp.exp(ar_ref[...]) * _softplus(sm + dr_ref[...])
    shift = (LANES - hq * 8) % LANES
    beta_c = pltpu.roll(beta_full, shift, 1)
    g_c = pltpu.roll(g_full, shift, 1)
    smt = smt_ref[...]
    g_r = -jnp.exp(ac_ref[...]) * _softplus(smt + dc_ref[...])

    ri = lax.broadcasted_iota(i32, (TB, TB), 0)
    ci = lax.broadcasted_iota(i32, (TB, TB), 1)
    same = (ri // C) == (ci // C)
    bi = lax.broadcasted_iota(i32, (P, P), 0)
    bj = lax.broadcasted_iota(i32, (P, P), 1)
    eye = (bi == bj).astype(f32)
    same_head = (bi // C) == (bj // C)

    hp = lax.Precision.HIGHEST
    gcum_c, gcum_r = [], []
    for d in range(2):
        tri = (same & ((ci <= ri) if d == 0 else (ci >= ri))).astype(f32)
        gcum_c.append(jnp.dot(tri, g_c, precision=hp, preferred_element_type=f32))
        gcum_r.append(lax.dot_general(g_r, tri, (((1,), (1,)), ((), ())), precision=hp,
                                      preferred_element_type=f32))

    k2, q2, kk2, qk2, v2 = [], [], [], [], []
    for c in range(nch):
        rows = slice(c * C, (c + 1) * C)
        k = k_ref[rows, :]
        q = q_ref[rows, :]
        kp = jnp.concatenate([k, k], axis=0)
        qp = jnp.concatenate([q, q], axis=0)
        k2.append(kp.astype(f32))
        q2.append(qp.astype(f32))
        kk2.append(lax.dot_general(kp, kp, (((1,), (1,)), ((), ())), preferred_element_type=f32))
        qk2.append(lax.dot_general(qp, kp, (((1,), (1,)), ((), ())), preferred_element_type=f32))
        v2.append(jnp.concatenate([v_ref[rows, :DN_V_DIM], v_ref[rows, DN_V_DIM:]], axis=0).astype(f32))

    inst = [(d, c) for d in range(2) for c in range(nch)]
    gcol, bcol, decay, lmats = [], [], [], []
    for d, c in inst:
        rows = slice(c * C, (c + 1) * C)
        b0, g0 = d * 2, 4 + d * 2
        gc = jnp.concatenate([gcum_c[d][rows, g0:g0 + 1], gcum_c[d][rows, g0 + 1:g0 + 2]], axis=0)
        gr = jnp.concatenate([gcum_r[d][g0:g0 + 1, rows], gcum_r[d][g0 + 1:g0 + 2, rows]], axis=1)
        bc = jnp.concatenate([beta_c[rows, b0:b0 + 1], beta_c[rows, b0 + 1:b0 + 2]], axis=0)
        incl = same_head & ((bj <= bi) if d == 0 else (bj >= bi))
        strict = same_head & ((bj < bi) if d == 0 else (bj > bi))
        dec = jnp.where(incl, jnp.exp(jnp.where(incl, gc - gr, 0.0)), 0.0)
        gcol.append(gc)
        bcol.append(bc)
        decay.append(dec)
        lmats.append(jnp.where(strict, kk2[c] * bc * dec, 0.0))

    invs = _unit_tri_inverse_all(lmats, eye, bi, bj)

    for n, (d, c) in enumerate(inst):
        rows = slice(c * C, (c + 1) * C)
        last = C - 1 if d == 0 else 0
        gc, bc = gcol[n], bcol[n]
        eg = jnp.exp(gc)
        rhs = jnp.concatenate([v2[c] * bc, k2[c] * (bc * eg)], axis=1)
        sol = _mm(invs[n], rhs)
        gl0 = gc[last:last + 1, :]
        gl1 = gc[C + last:C + last + 1, :]
        glast = jnp.concatenate([jnp.broadcast_to(gl0, (C, 1)), jnp.broadcast_to(gl1, (C, 1))], axis=0)
        qd = q2[c] * eg
        kd = k2[c] * jnp.exp(glast - gc)
        aqk = qk2[c] * decay[n]
        for hvl in range(2):
            pr = slice(hvl * C, (hvl + 1) * C)
            cols = slice(hvl * DN_V_DIM, (hvl + 1) * DN_V_DIM)
            u_ref[d, rows, cols] = sol[pr, :DN_V_DIM].astype(bf16)
            w_ref[d, rows, cols] = sol[pr, DN_V_DIM:].astype(bf16)
            qd_ref[d, rows, cols] = qd[pr].astype(bf16)
            kd_ref[d, rows, cols] = kd[pr].astype(bf16)
            aqk_ref[d, rows, pr] = aqk[pr, pr].astype(bf16)
        egl_ref[d, 0, c, 0:1, :] = jnp.broadcast_to(jnp.exp(gl0), (1, LANES))
        egl_ref[d, 0, c, 1:2, :] = jnp.broadcast_to(jnp.exp(gl1), (1, LANES))


def _dnprep(qkv, small, small_t, ar, dr, ac, dc):
    T = qkv.shape[0]
    TB = DN_TB
    nblk = T // TB
    nch = TB // DN_CHUNK
    kcol = DN_QK_W // DN_K_DIM
    vcol = 2 * DN_QK_W // (2 * DN_V_DIM)
    wide = lambda: pl.BlockSpec((2, TB, 2 * DN_V_DIM), lambda i, h: (0, i, h))
    big = jax.ShapeDtypeStruct((2, T, DN_V_W), bf16)
    return pl.pallas_call(
        _dnprep_kernel,
        grid=(nblk, DN_QK_HEADS),
        in_specs=[
            pl.BlockSpec((TB, DN_K_DIM), lambda i, h: (i, h)),
            pl.BlockSpec((TB, DN_K_DIM), lambda i, h: (i, kcol + h)),
            pl.BlockSpec((TB, 2 * DN_V_DIM), lambda i, h: (i, vcol + h)),
            pl.BlockSpec((TB, LANES), lambda i, h: (i, 0)),
            pl.BlockSpec((8, TB), lambda i, h: (h, i)),
            pl.BlockSpec((1, LANES), lambda i, h: (0, 0)),
            pl.BlockSpec((1, LANES), lambda i, h: (0, 0)),
            pl.BlockSpec((8, 1), lambda i, h: (h, 0)),
            pl.BlockSpec((8, 1), lambda i, h: (h, 0)),
        ],
        out_specs=[
            wide(), wide(), wide(), wide(),
            pl.BlockSpec((2, TB, 2 * DN_CHUNK), lambda i, h: (0, i, h)),
            pl.BlockSpec((2, 1, nch, 2, LANES), lambda i, h: (0, h, i, 0, 0)),
        ],
        out_shape=[
            big, big, big, big,
            jax.ShapeDtypeStruct((2, T, DN_QK_HEADS * 2 * DN_CHUNK), bf16),
            jax.ShapeDtypeStruct((2, DN_QK_HEADS, T // DN_CHUNK, 2, LANES), f32),
        ],
        compiler_params=_cparams(("parallel", "parallel")),
        name="dn_prep",
    )(qkv, qkv, qkv, small, small_t, ar, dr, ac, dc)


def _dnscan_kernel(uf, wf, qf, kf, af, ef, ub, wb, qb, kb, ab, eb, of_ref, ob_ref, s_ref):
    t = pl.program_id(1)
    C = DN_CHUNK
    nch = uf.shape[1] // C

    @pl.when(t == 0)
    def _():
        s_ref[...] = jnp.zeros(s_ref.shape, f32)

    dirs = ((uf, wf, qf, kf, af, ef, of_ref), (ub, wb, qb, kb, ab, eb, ob_ref))
    chains = [(d, hv) for d in range(2) for hv in range(DN_V_HEADS)]
    for step in range(nch):
        def rows(d):
            c = step if d == 0 else nch - 1 - step
            return c, slice(c * C, (c + 1) * C)

        def cols(hv):
            return slice(hv * DN_V_DIM, (hv + 1) * DN_V_DIM)

        s_old = [s_ref[d * DN_V_HEADS + hv] for d, hv in chains]
        s_bf = [s.astype(bf16) for s in s_old]
        w_s = [jnp.dot(dirs[d][1][0, rows(d)[1], cols(hv)], sb, preferred_element_type=f32)
               for (d, hv), sb in zip(chains, s_bf)]
        q_s = [jnp.dot(dirs[d][2][0, rows(d)[1], cols(hv)], sb, preferred_element_type=f32)
               for (d, hv), sb in zip(chains, s_bf)]
        v_bf = [(dirs[d][0][0, rows(d)[1], cols(hv)].astype(f32) - ws).astype(bf16)
                for (d, hv), ws in zip(chains, w_s)]
        a_v = [jnp.dot(dirs[d][4][0, rows(d)[1], hv * C:(hv + 1) * C], vb, preferred_element_type=f32)
               for (d, hv), vb in zip(chains, v_bf)]
        upd = [lax.dot_general(dirs[d][3][0, rows(d)[1], cols(hv)], vb, (((0,), (0,)), ((), ())),
                               preferred_element_type=f32)
               for (d, hv), vb in zip(chains, v_bf)]
        for n, (d, hv) in enumerate(chains):
            c, rs = rows(d)
            e = dirs[d][5][0, hv // 2, c, hv % 2:hv % 2 + 1, :]
            s_ref[d * DN_V_HEADS + hv] = s_old[n] * e + upd[n]
            dirs[d][6][rs, cols(hv)] = q_s[n] + a_v[n]


def _dnscan(u, w, qd, kd, aqk, egl, nseq, seq):
    TB = DN_TB
    nt = seq // TB
    nch = TB // DN_CHUNK
    T = nseq * seq

    def blk(d):
        if d == 0:
            return lambda b, t: b * nt + t
        return lambda b, t: b * nt + nt - 1 - t

    def specs(d):
        wide = lambda: pl.BlockSpec((1, TB, DN_V_W), lambda b, t: (d, blk(d)(b, t), 0))
        return [wide(), wide(), wide(), wide(),
                pl.BlockSpec((1, TB, DN_V_HEADS * DN_CHUNK), lambda b, t: (d, blk(d)(b, t), 0)),
                pl.BlockSpec((1, DN_QK_HEADS, nch, 2, LANES), lambda b, t: (d, 0, blk(d)(b, t), 0, 0))]

    def ospec(d):
        return pl.BlockSpec((TB, DN_V_W), lambda b, t: (blk(d)(b, t), 0))

    args = (u, w, qd, kd, aqk, egl)
    return pl.pallas_call(
        _dnscan_kernel,
        grid=(nseq, nt),
        in_specs=specs(0) + specs(1),
        out_specs=[ospec(0), ospec(1)],
        out_shape=[jax.ShapeDtypeStruct((T, DN_V_W), f32)] * 2,
        scratch_shapes=[pltpu.VMEM((2 * DN_V_HEADS, DN_K_DIM, DN_V_DIM), f32)],
        compiler_params=_cparams(("parallel", "arbitrary")),
        name="dn_scan",
    )(*args, *args)


def _post_kernel(xa_ref, xb_ref, oa_ref, of_ref, ob_ref, z_ref, ga_ref, gd_ref, dnw_ref, wa_ref, wd_ref,
                 wo_ref, n2_ref, rw_ref, rb_ref, x2_ref, hp_ref, idx_ref, gate_ref, *, npb):
    od = of_ref[...] + ob_ref[...]
    z = z_ref[...].astype(f32)
    parts = []
    for h in range(DN_V_HEADS):
        cols = slice(h * DN_V_DIM, (h + 1) * DN_V_DIM)
        seg = od[:, cols]
        seg = seg * lax.rsqrt(jnp.mean(seg * seg, axis=-1, keepdims=True) + NORM_EPS) * dnw_ref[...]
        zz = z[:, cols]
        parts.append((seg * (zz * _sigmoid(zz))).astype(bf16))
    odn = jnp.concatenate(parts, axis=1)
    ya = jnp.dot(oa_ref[...], wa_ref[...], preferred_element_type=f32)
    yd = jnp.dot(odn, wd_ref[...], preferred_element_type=f32)
    merged = _sigmoid(ga_ref[...].astype(f32)) * ya + _sigmoid(gd_ref[...].astype(f32)) * yd
    x = jnp.where(pl.program_id(0) < npb, xa_ref[...], xb_ref[...])
    x2 = x + jnp.dot(merged.astype(bf16), wo_ref[...], preferred_element_type=f32)
    x2_ref[...] = x2
    h2 = x2 * lax.rsqrt(jnp.mean(x2 * x2, axis=-1, keepdims=True) + NORM_EPS) * n2_ref[...]
    half = D_MODEL // 2
    _store_row_tiles(hp_ref, _pack_halves(h2[:, :half], h2[:, half:]))

    logits = jnp.dot(h2, rw_ref[...], precision=lax.Precision.HIGHEST,
                     preferred_element_type=f32) + rb_ref[...]
    lane = lax.broadcasted_iota(i32, logits.shape, 1).astype(f32)
    vals, idxs = [], []
    cur = logits
    for _ in range(TOP_K):
        m = jnp.max(cur, axis=-1, keepdims=True)
        idx = jnp.min(jnp.where(cur == m, lane, float(LANES)), axis=-1, keepdims=True)
        vals.append(m)
        idxs.append(idx)
        cur = jnp.where(lane == idx, -jnp.inf, cur)
    exps = [jnp.exp(v - vals[0]) for v in vals]
    tot = exps[0] + exps[1] + exps[2] + exps[3]
    idx_out = jnp.zeros(logits.shape, f32)
    gate_out = jnp.zeros(logits.shape, f32)
    for kk in range(TOP_K):
        idx_out = jnp.where(lane == float(kk), idxs[kk], idx_out)
        gate_out = jnp.where(lane == float(kk), exps[kk] / tot, gate_out)
    idx_ref[...] = idx_out.astype(i32)
    gate_ref[...] = gate_out


def _post(xa, xb, o_attn, o_f, o_b, proj, dnw, wa, wd, wo, n2, rw, rb):
    T = xa.shape[0] + xb.shape[0]
    tm = min(POST_TM, xa.shape[0], xb.shape[0])
    assert xa.shape[0] % tm == 0 and xb.shape[0] % tm == 0
    npb = xa.shape[0] // tm
    zcol = P_DNZ // DN_V_W
    const = lambda shape: pl.BlockSpec(shape, lambda i: (0, 0))
    row = lambda width, col=0: pl.BlockSpec((tm, width), lambda i: (i, col))
    return pl.pallas_call(
        functools.partial(_post_kernel, npb=npb),
        grid=(T // tm,),
        in_specs=_group_row_specs(tm, D_MODEL, npb) + [
            row(ATTN_Q_W), row(DN_V_W), row(DN_V_W),
            row(DN_V_W, zcol), row(D_MODEL, P_GA // D_MODEL), row(D_MODEL, P_GD // D_MODEL),
            const((1, DN_V_DIM)), const((ATTN_Q_W, D_MODEL)), const((DN_V_W, D_MODEL)),
            const((D_MODEL, D_MODEL)), const((1, D_MODEL)), const((D_MODEL, LANES)),
            const((1, LANES)),
        ],
        out_specs=[row(D_MODEL), pl.BlockSpec((tm * ROW_TILES, LANES), lambda i: (i, 0)),
                   row(LANES), row(LANES)],
        out_shape=[
            jax.ShapeDtypeStruct((T, D_MODEL), f32),
            jax.ShapeDtypeStruct((T * ROW_TILES, LANES), u32),
            jax.ShapeDtypeStruct((T, LANES), i32),
            jax.ShapeDtypeStruct((T, LANES), f32),
        ],
        compiler_params=_cparams(("parallel",)),
        name="post_mix",
    )(xa, xb, o_attn, o_f, o_b, proj, proj, proj, dnw, wa, wd, wo, n2, rw, rb)


def _scatter_kernel(poff_ref, pn_ref, pos_ref, h_ref, xs_ref, zero_ref, sem, zsem):
    R = pos_ref.shape[2] // TOP_K
    nbits = MOE_TM.bit_length() - 1

    @pl.when(pl.program_id(0) == 0)
    def _():
        zero_ref[...] = jnp.zeros(zero_ref.shape, u32)

        def per_expert(e, carry):
            off = poff_ref[e]
            n = pn_ref[e]
            for bit in reversed(range(nbits)):
                size = 1 << bit
                take = ((n >> bit) & 1) == 1

                @pl.when(take)
                def _():
                    cp = pltpu.make_async_copy(zero_ref.at[_row_span(0, size)],
                                               xs_ref.at[_row_span(off, size)], zsem)
                    cp.start()
                    cp.wait()

                off = off + jnp.where(take, size, 0)
            return carry

        lax.fori_loop(0, N_EXPERTS, per_expert, 0)

        zrows = zero_ref.shape[0] // ROW_TILES

        def tail(j, carry):
            row = poff_ref[N_EXPERTS] + j * zrows
            cp = pltpu.make_async_copy(zero_ref, xs_ref.at[_row_span(row, zrows)], zsem)
            cp.start()
            cp.wait()
            return carry

        lax.fori_loop(0, pn_ref[N_EXPERTS], tail, 0)

    def body(a, carry):
        p = pos_ref[0, 0, a]
        pltpu.make_async_copy(h_ref.at[_row_span(a // TOP_K)], xs_ref.at[_row_span(p)], sem).start()
        return carry

    lax.fori_loop(0, TOP_K * R, body, 0, unroll=8)
    done = xs_ref.at[_row_span(0, TOP_K * R)]
    pltpu.make_async_copy(done, done, sem).wait()


def _scatter_rows(pad_off, pad_n, pos, hp, n_slots):
    T = hp.shape[0] // ROW_TILES
    R = min(SCATTER_R, T)
    grid_spec = pltpu.PrefetchScalarGridSpec(
        num_scalar_prefetch=2,
        grid=(T // R,),
        in_specs=[
            pl.BlockSpec((1, 1, TOP_K * R), lambda i, po, pn: (i, 0, 0), memory_space=pltpu.SMEM),
            pl.BlockSpec((R * ROW_TILES, LANES), lambda i, po, pn: (i, 0)),
        ],
        out_specs=pl.BlockSpec(memory_space=pl.ANY),
        scratch_shapes=[pltpu.VMEM((MOE_TM // 2 * ROW_TILES, LANES), u32),
                        pltpu.SemaphoreType.DMA(()), pltpu.SemaphoreType.DMA(())],
    )
    return pl.pallas_call(
        _scatter_kernel,
        grid_spec=grid_spec,
        out_shape=jax.ShapeDtypeStruct((n_slots * ROW_TILES, LANES), u32),
        compiler_params=_cparams(("arbitrary",)),
        name="moe_scatter",
    )(pad_off, pad_n, pos.reshape(T // R, 1, TOP_K * R), hp)


def _moe_kernel(te_ref, nu_ref, x_ref, wg_ref, wu_ref, bg_ref, bu_ref, wd_ref, bd_ref,
                o_ref, xb_ref, acc_ref):
    i = pl.program_id(0)
    f = pl.program_id(1)
    nf = pl.num_programs(1)
    half = D_MODEL // 2
    used = i < nu_ref[0]

    @pl.when(used & (f == 0))
    def _():
        for s in range(ROW_TILES):
            lo, hi = _unpack_halves(x_ref[_row_tile(0, xb_ref.shape[0], s), :])
            xb_ref[:, s * LANES:(s + 1) * LANES] = lo.astype(bf16)
            xb_ref[:, half + s * LANES:half + (s + 1) * LANES] = hi.astype(bf16)

    @pl.when(used)
    def _():
        xb = xb_ref[...]
        g = jnp.dot(xb, wg_ref[0, 0], preferred_element_type=f32) + bg_ref[0]
        u = jnp.dot(xb, wu_ref[0, 0], preferred_element_type=f32) + bu_ref[0]
        gate = jnp.minimum(g, SWIGLU_LIMIT)
        up = jnp.clip(u, -SWIGLU_LIMIT, SWIGLU_LIMIT)
        act = (up + 1.0) * (gate * _sigmoid(SWIGLU_ALPHA * gate))
        part = jnp.dot(act.astype(bf16), wd_ref[0], preferred_element_type=f32)

        @pl.when(f == 0)
        def _():
            acc_ref[...] = part

        @pl.when(f > 0)
        def _():
            acc_ref[...] += part

    @pl.when(used & (f == nf - 1))
    def _():
        y = acc_ref[...] + bd_ref[0]
        _store_row_tiles(o_ref, _pack_halves(y[:, :half], y[:, half:]))

    @pl.when(jnp.logical_not(used) & (f == nf - 1))
    def _():
        o_ref[...] = jnp.zeros(o_ref.shape, u32)


def _moe(tile_expert, n_used, xs, wg, wu, bg, bu, wd, bd):
    n_slots = xs.shape[0] // ROW_TILES
    tm, tf = MOE_TM, MOE_TF
    n_tiles = n_slots // tm
    nf = D_FF // tf

    def fidx(i, f, te, nu):
        return jnp.where(i < nu[0], f, nf - 1)

    grid_spec = pltpu.PrefetchScalarGridSpec(
        num_scalar_prefetch=2,
        grid=(n_tiles, nf),
        in_specs=[
            pl.BlockSpec((tm * ROW_TILES, LANES), lambda i, f, te, nu: (i, 0)),
            pl.BlockSpec((1, 1, D_MODEL, tf), lambda i, f, te, nu: (te[i], fidx(i, f, te, nu), 0, 0)),
            pl.BlockSpec((1, 1, D_MODEL, tf), lambda i, f, te, nu: (te[i], fidx(i, f, te, nu), 0, 0)),
            pl.BlockSpec((1, 1, tf), lambda i, f, te, nu: (te[i], 0, fidx(i, f, te, nu))),
            pl.BlockSpec((1, 1, tf), lambda i, f, te, nu: (te[i], 0, fidx(i, f, te, nu))),
            pl.BlockSpec((1, tf, D_MODEL), lambda i, f, te, nu: (te[i], fidx(i, f, te, nu), 0)),
            pl.BlockSpec((1, 1, D_MODEL), lambda i, f, te, nu: (te[i], 0, 0)),
        ],
        out_specs=pl.BlockSpec((tm * ROW_TILES, LANES), lambda i, f, te, nu: (i, 0)),
        scratch_shapes=[pltpu.VMEM((tm, D_MODEL), bf16), pltpu.VMEM((tm, D_MODEL), f32)],
    )
    return pl.pallas_call(
        _moe_kernel,
        grid_spec=grid_spec,
        out_shape=jax.ShapeDtypeStruct((n_slots * ROW_TILES, LANES), u32),
        compiler_params=_cparams(("arbitrary", "arbitrary")),
        name="moe_experts",
    )(tile_expert, n_used, xs, wg, wu, bg, bu, wd, bd)


def _combine_kernel(pos_ref, posn_ref, x2_ref, gate_ref, fw_ref, ys_ref, oa_ref, ob_ref,
                    buf_ref, acc_ref, sem, *, npb):
    R = x2_ref.shape[0]
    half = D_MODEL // 2
    nrows = TOP_K * R
    i = pl.program_id(0)
    slot = i % 2

    def issue(p_ref, dst_slot):
        def body(a, carry):
            p = p_ref[0, 0, a]
            pltpu.make_async_copy(ys_ref.at[_row_span(p)],
                                  buf_ref.at[_row_span(dst_slot * nrows + a)], sem.at[dst_slot]).start()
            return carry

        lax.fori_loop(0, nrows, body, 0, unroll=8)

    @pl.when(i == 0)
    def _():
        issue(pos_ref, 0)

    @pl.when(i + 1 < pl.num_programs(0))
    def _():
        issue(posn_ref, 1 - slot)

    pltpu.make_async_copy(ys_ref.at[_row_span(0, nrows)], buf_ref.at[_row_span(slot * nrows, nrows)],
                          sem.at[slot]).wait()
    row0 = slot * nrows

    gates = gate_ref[...]
    gk = [gates[:, k:k + 1] for k in range(TOP_K)]
    ss = jnp.zeros((R, 1), f32)
    for s in range(ROW_TILES):
        cl = slice(s * LANES, (s + 1) * LANES)
        ch = slice(half + s * LANES, half + (s + 1) * LANES)
        lo_acc = x2_ref[:, cl]
        hi_acc = x2_ref[:, ch]
        for k in range(TOP_K):
            lo, hi = _unpack_halves(buf_ref[_row_tile(row0 + k * R, R, s), :])
            lo_acc = lo_acc + gk[k] * lo
            hi_acc = hi_acc + gk[k] * hi
        ss = ss + jnp.sum(lo_acc * lo_acc, axis=-1, keepdims=True) + jnp.sum(
            hi_acc * hi_acc, axis=-1, keepdims=True)
        acc_ref[:, cl] = lo_acc
        acc_ref[:, ch] = hi_acc
    inv = lax.rsqrt(ss / D_MODEL + NORM_EPS)

    @pl.when(i < npb)
    def _():
        oa_ref[...] = acc_ref[...] * inv * fw_ref[...]

    @pl.when(i >= npb)
    def _():
        ob_ref[...] = acc_ref[...] * inv * fw_ref[...]


def _combine(pos, x2, gates, fw, ys, rows_a):
    T = x2.shape[0]
    R = min(COMB_R, rows_a, T - rows_a)
    assert rows_a % R == 0 and T % R == 0
    npb = rows_a // R
    nsteps = T // R
    return pl.pallas_call(
        functools.partial(_combine_kernel, npb=npb),
        grid=(nsteps,),
        in_specs=[
            pl.BlockSpec((1, 1, TOP_K * R), lambda i: (i, 0, 0), memory_space=pltpu.SMEM),
            pl.BlockSpec((1, 1, TOP_K * R), lambda i: (jnp.minimum(i + 1, nsteps - 1), 0, 0),
                         memory_space=pltpu.SMEM),
            pl.BlockSpec((R, D_MODEL), lambda i: (i, 0)),
            pl.BlockSpec((R, LANES), lambda i: (i, 0)),
            pl.BlockSpec((1, D_MODEL), lambda i: (0, 0)),
            pl.BlockSpec(memory_space=pl.ANY),
        ],
        out_specs=_group_row_specs(R, D_MODEL, npb),
        out_shape=[jax.ShapeDtypeStruct((rows_a, D_MODEL), f32),
                   jax.ShapeDtypeStruct((T - rows_a, D_MODEL), f32)],
        scratch_shapes=[pltpu.VMEM((2 * TOP_K * R * ROW_TILES, LANES), u32),
                        pltpu.VMEM((R, D_MODEL), f32),
                        pltpu.SemaphoreType.DMA((2,))],
        compiler_params=_cparams(("arbitrary",)),
        name="moe_combine",
    )(pos, pos, x2, gates, fw, ys)


def _dn_channels(t):
    lead = t.shape[:-1]
    t = t.reshape(lead + (2, 2, DN_QK_HEADS, 2))
    perm = tuple(range(len(lead))) + tuple(len(lead) + a for a in (2, 0, 1, 3))
    return t.transpose(perm).reshape(lead + (4 * DN_V_HEADS,))


def _pad_lanes(t):
    pad = [(0, 0)] * (t.ndim - 1) + [(0, LANES - t.shape[-1])]
    return jnp.pad(t, pad)


def _deint_kernel(w_ref, p_ref, g_ref, u_ref):
    for c in range(w_ref.shape[2] // (2 * LANES)):
        w = w_ref[0, :, c * 2 * LANES:(c + 1) * 2 * LANES].astype(bf16)
        r = jnp.dot(w, p_ref[...], preferred_element_type=f32)
        g_ref[0, 0, :, c * LANES:(c + 1) * LANES] = r[:, :LANES].astype(bf16)
        u_ref[0, 0, :, c * LANES:(c + 1) * LANES] = r[:, LANES:].astype(bf16)


def _deinterleave(w):
    E, D, F2 = w.shape
    rb, cb = 1024, 2 * MOE_TF
    i = jnp.arange(2 * LANES)
    src = jnp.where(i < LANES, 2 * i, 2 * (i - LANES) + 1)
    perm = (jnp.arange(2 * LANES)[:, None] == src[None, :]).astype(bf16)
    out = jax.ShapeDtypeStruct((E, F2 // cb, D, MOE_TF), bf16)
    return pl.pallas_call(
        _deint_kernel,
        grid=(E, D // rb, F2 // cb),
        in_specs=[pl.BlockSpec((1, rb, cb), lambda e, r, c: (e, r, c)),
                  pl.BlockSpec((2 * LANES, 2 * LANES), lambda e, r, c: (0, 0))],
        out_specs=[pl.BlockSpec((1, 1, rb, MOE_TF), lambda e, r, c: (e, c, r, 0))] * 2,
        out_shape=[out, out],
        compiler_params=_cparams(("parallel", "parallel", "parallel")),
        name="moe_deinterleave",
    )(w, perm)


def _routing(top_idx, T, tm):
    A = T * TOP_K
    n_tiles = A // tm + N_EXPERTS
    n_slots = n_tiles * tm
    flat_e = top_idx.reshape(-1)
    onehot = (flat_e[:, None] == jnp.arange(N_EXPERTS, dtype=i32)[None, :]).astype(i32)
    csum = jnp.cumsum(onehot, axis=0)
    counts = csum[-1]
    padded = ((counts + tm - 1) // tm) * tm
    padded_end = jnp.cumsum(padded)
    padded_start = padded_end - padded
    pos = jnp.sum(onehot * (csum - 1 + padded_start[None, :]), axis=1).astype(i32)
    tile_start = jnp.arange(n_tiles, dtype=i32) * tm
    tile_expert = jnp.minimum(
        jnp.sum(padded_end[None, :] <= tile_start[:, None], axis=1), N_EXPERTS - 1).astype(i32)
    n_used = (padded_end[-1] // tm).astype(i32).reshape(1)
    pad_off = jnp.concatenate([padded_start + counts, padded_end[-1:]]).astype(i32)
    pad_n = jnp.concatenate([padded - counts, (n_slots - padded_end[-1:]) // (tm // 2)]).astype(i32)
    return pos, pad_off, pad_n, tile_expert, n_used, n_slots


def _layer(xa, xb, seq, norm1_w, w_in, b_in, attn_sinks, dn_conv_w, dn_a_log, dn_dt_bias,
           dn_norm_w, w_attn_o, w_dn_o, w_out, norm2_w, router_w, router_b, w_gate_up, b_gate_up,
           w_down, b_down, final_norm_w):
    T = xa.shape[0] + xb.shape[0]
    nseq = T // seq

    order = ([_OFF_GA, D_MODEL], [_OFF_GD, D_MODEL], [_OFF_DNQKV, DN_CONV_W], [_OFF_AQ, ATTN_Q_W],
             [_OFF_DNZ, DN_V_W], [_OFF_AK, ATTN_KV_W], [_OFF_AV, ATTN_KV_W])
    w_main = jnp.concatenate([w_in[:, o:o + n] for o, n in order], axis=1).astype(bf16)
    b_main = jnp.concatenate([b_in[o:o + n] for o, n in order])[None, :]
    nsm = 4 * DN_V_HEADS
    w_small = _pad_lanes(_dn_channels(w_in[:, _OFF_DNB:_OFF_DNB + nsm])).astype(bf16)
    b_small = _pad_lanes(_dn_channels(b_in[_OFF_DNB:_OFF_DNB + nsm]))
    zeros_b = jnp.zeros((2 * DN_V_HEADS,), f32)
    a_ch = _pad_lanes(_dn_channels(jnp.concatenate([zeros_b, dn_a_log.reshape(-1)])))
    d_ch = _pad_lanes(_dn_channels(jnp.concatenate([zeros_b, dn_dt_bias.reshape(-1)])))

    proj, small, small_t = _inproj(xa, xb, norm1_w[None, :], w_main, b_main, w_small, b_small[None, :],
                                   w_small.T, b_small[:, None])

    pos = jnp.arange(seq, dtype=f32)
    inv_freq = 1.0 / (ROPE_THETA ** (jnp.arange(0, ROPE_DIM, 2, dtype=f32) / ROPE_DIM))
    ang = pos[:, None] * inv_freq[None, :]
    cos, sin = jnp.cos(ang), jnp.sin(ang)
    half = ROPE_DIM // 2
    ones_tail = jnp.ones((seq, ATTN_HEAD_DIM - ROPE_DIM), f32)
    cos_head = jnp.concatenate([cos, cos, ones_tail], axis=1)
    sin_head = jnp.concatenate([-sin, sin, 0.0 * ones_tail], axis=1)
    cos_t = jnp.concatenate([cos_head, cos_head], axis=1)
    sin_t = jnp.concatenate([sin_head, sin_head], axis=1)

    o_attn = _attention(proj, attn_sinks, cos_t, sin_t, nseq, seq)

    qkv = _dnconv(proj, dn_conv_w, nseq, seq)
    u, w, qd, kd, aqk, egl = _dnprep(qkv, small, small_t, a_ch[None, :], d_ch[None, :],
                                     a_ch[:, None], d_ch[:, None])
    o_f, o_b = _dnscan(u, w, qd, kd, aqk, egl, nseq, seq)

    rw = jnp.zeros((D_MODEL, LANES), f32).at[:, :N_EXPERTS].set(router_w)
    rb = jnp.full((LANES,), NEG_INF, f32).at[:N_EXPERTS].set(router_b)
    x2, hp, top_idx, top_gate = _post(
        xa, xb, o_attn, o_f, o_b, proj, dn_norm_w[None, :], w_attn_o.astype(bf16), w_dn_o.astype(bf16),
        w_out.astype(bf16), norm2_w[None, :], rw, rb[None, :])

    pos_a, pad_off, pad_n, tile_expert, n_used, n_slots = _routing(top_idx[:, :TOP_K], T, MOE_TM)
    xs = _scatter_rows(pad_off, pad_n, pos_a, hp, n_slots)
    wg, wu = _deinterleave(w_gate_up)
    b_pairs = b_gate_up.reshape(N_EXPERTS, 1, D_FF, 2)
    bg, bu = b_pairs[..., 0], b_pairs[..., 1]
    ys = _moe(tile_expert, n_used, xs, wg, wu, bg, bu, w_down.astype(bf16), b_down[:, None, :])
    R = min(COMB_R, xa.shape[0], xb.shape[0])
    pos_tiles = pos_a.reshape(T // R, R, TOP_K).transpose(0, 2, 1).reshape(T // R, 1, TOP_K * R)
    return _combine(pos_tiles, x2, top_gate, final_norm_w[None, :], ys, xa.shape[0])


def kernel(x_prompt, x_sample, norm1_w, w_in, b_in, attn_sinks, dn_conv_w, dn_a_log, dn_dt_bias,
           dn_norm_w, w_attn_o, w_dn_o, w_out, norm2_w, router_w, router_b, w_gate_up, b_gate_up,
           w_down, b_down, final_norm_w):
    bp, seq, d = x_prompt.shape
    bs = x_sample.shape[0]
    assert x_sample.shape[1] == seq and d == D_MODEL
    ya, yb = _layer(x_prompt.reshape(bp * seq, d), x_sample.reshape(bs * seq, d), seq,
                    norm1_w[0], w_in[0], b_in[0], attn_sinks[0], dn_conv_w[0], dn_a_log[0],
                    dn_dt_bias[0], dn_norm_w[0], w_attn_o[0], w_dn_o[0], w_out[0], norm2_w[0],
                    router_w[0], router_b[0], w_gate_up[0], b_gate_up[0], w_down[0], b_down[0],
                    final_norm_w)
    return (ya.reshape(bp, seq, d), yb.reshape(bs, seq, d))
```

```python
import functools
import math

import jax
import jax.numpy as jnp
from jax import lax
from jax.experimental import pallas as pl
from jax.experimental.pallas import tpu as pltpu

f32 = jnp.float32
bf16 = jnp.bfloat16
u32 = jnp.uint32
i32 = jnp.int32

D_MODEL = 2048
ATTN_HEADS = 16
ATTN_KV_HEADS = 4
ATTN_HEAD_DIM = 64
ATTN_BLOCK = 128
ROPE_THETA = 500000.0
ROPE_DIM = ATTN_HEAD_DIM // 4
DN_QK_HEADS = 4
DN_V_HEADS = 8
DN_K_DIM = 128
DN_V_DIM = 128
DN_CONV = 5
DN_CHUNK = 64
N_EXPERTS = 32
TOP_K = 4
D_FF = D_MODEL
SWIGLU_LIMIT = 7.0
SWIGLU_ALPHA = 1.702
NORM_EPS = 1e-6
NEG_INF = -1e30

ATTN_Q_W = ATTN_HEADS * ATTN_HEAD_DIM
ATTN_KV_W = ATTN_KV_HEADS * ATTN_HEAD_DIM
DN_QK_W = DN_QK_HEADS * DN_K_DIM
DN_V_W = DN_V_HEADS * DN_V_DIM
DN_CONV_W = 2 * DN_QK_W + DN_V_W

_OFF_AQ = 0
_OFF_AK = _OFF_AQ + ATTN_Q_W
_OFF_AV = _OFF_AK + ATTN_KV_W
_OFF_DNQKV = _OFF_AV + ATTN_KV_W
_OFF_DNZ = _OFF_DNQKV + DN_CONV_W
_OFF_DNB = _OFF_DNZ + DN_V_W
_OFF_DNA = _OFF_DNB + 2 * DN_V_HEADS
_OFF_GA = _OFF_DNA + 2 * DN_V_HEADS
_OFF_GD = _OFF_GA + D_MODEL

P_GA = 0
P_GD = P_GA + D_MODEL
P_DNQKV = P_GD + D_MODEL
P_AQ = P_DNQKV + DN_CONV_W
P_DNZ = P_AQ + ATTN_Q_W
P_AK = P_DNZ + DN_V_W
P_AV = P_AK + ATTN_KV_W
P_W = P_AV + ATTN_KV_W

LANES = 128
VMEM_LIMIT = 56 * 1024 * 1024

IN_TM = 1024
IN_TN = 512
DN_TB = 256
POST_TM = 256
POST_SUB = 128
MOE_TM = 512
MOE_TF = 1024
SCATTER_R = 1024
COMB_R = 128


def _cparams(sem):
    return pltpu.CompilerParams(dimension_semantics=sem, vmem_limit_bytes=VMEM_LIMIT)


def _sigmoid(x):
    return 1.0 / (1.0 + jnp.exp(-x))


def _softplus(x):
    return jnp.maximum(x, 0.0) + jnp.log(1.0 + jnp.exp(-jnp.abs(x)))


def _pack_halves(lo, hi):
    a = lax.bitcast_convert_type(lo.astype(bf16).astype(f32), u32)
    b = lax.bitcast_convert_type(hi.astype(bf16).astype(f32), u32)
    return (a >> 16) | (b & jnp.uint32(0xFFFF0000))


def _unpack_halves(w):
    lo = lax.bitcast_convert_type(w << 16, f32)
    hi = lax.bitcast_convert_type(w & jnp.uint32(0xFFFF0000), f32)
    return lo, hi


ROW_TILES = D_MODEL // 2 // LANES


def _row_tile(r0, n, s):
    start = r0 * ROW_TILES
    if not isinstance(r0, int):
        start = pl.multiple_of(start, ROW_TILES)
    return pl.ds(start + s, n, stride=ROW_TILES)


def _row_span(r, n=1):
    start = r * ROW_TILES
    if not isinstance(r, int):
        start = pl.multiple_of(start, ROW_TILES)
    return pl.ds(start, n * ROW_TILES)


def _store_row_tiles(ref, packed):
    for s in range(ROW_TILES):
        ref[_row_tile(0, packed.shape[0], s), :] = packed[:, s * LANES:(s + 1) * LANES]


def _group_row_specs(tm, width, npb):
    first = pl.BlockSpec((tm, width), lambda i, *_: (jnp.minimum(i, npb - 1), 0))
    second = pl.BlockSpec((tm, width), lambda i, *_: (jnp.maximum(i - npb, 0), 0))
    return [first, second]


def _inproj_kernel(xa_ref, xb_ref, nw_ref, w_ref, b_ref, ws_ref, bs_ref, wst_ref, bst_ref,
                   o_ref, os_ref, ost_ref, h_ref, *, npb):
    @pl.when(pl.program_id(1) == 0)
    def _():
        x = jnp.where(pl.program_id(0) < npb, xa_ref[...], xb_ref[...])
        ms = jnp.mean(x * x, axis=-1, keepdims=True)
        hb = (x * lax.rsqrt(ms + NORM_EPS) * nw_ref[...]).astype(bf16)
        h_ref[...] = hb
        os_ref[...] = jnp.dot(hb, ws_ref[...], preferred_element_type=f32) + bs_ref[...]
        ost_ref[...] = lax.dot_general(wst_ref[...], hb, (((1,), (1,)), ((), ())),
                                       preferred_element_type=f32) + bst_ref[...]

    acc = jnp.dot(h_ref[...], w_ref[0], preferred_element_type=f32)
    o_ref[...] = (acc + b_ref[...]).astype(bf16)


def _inproj(xa, xb, nw, w, b, ws, bs, wst, bst):
    T = xa.shape[0] + xb.shape[0]
    tm = min(IN_TM, xa.shape[0], xb.shape[0])
    assert xa.shape[0] % tm == 0 and xb.shape[0] % tm == 0
    npb = xa.shape[0] // tm
    grid = (T // tm, P_W // IN_TN)
    return pl.pallas_call(
        functools.partial(_inproj_kernel, npb=npb),
        grid=grid,
        in_specs=_group_row_specs(tm, D_MODEL, npb) + [
            pl.BlockSpec((1, D_MODEL), lambda i, j: (0, 0)),
            pl.BlockSpec((1, D_MODEL, IN_TN), lambda i, j: (j, 0, 0)),
            pl.BlockSpec((1, IN_TN), lambda i, j: (0, j)),
            pl.BlockSpec((D_MODEL, LANES), lambda i, j: (0, 0)),
            pl.BlockSpec((1, LANES), lambda i, j: (0, 0)),
            pl.BlockSpec((LANES, D_MODEL), lambda i, j: (0, 0)),
            pl.BlockSpec((LANES, 1), lambda i, j: (0, 0)),
        ],
        out_specs=[
            pl.BlockSpec((tm, IN_TN), lambda i, j: (i, j)),
            pl.BlockSpec((tm, LANES), lambda i, j: (i, 0)),
            pl.BlockSpec((LANES, tm), lambda i, j: (0, i)),
        ],
        out_shape=[
            jax.ShapeDtypeStruct((T, P_W), bf16),
            jax.ShapeDtypeStruct((T, LANES), f32),
            jax.ShapeDtypeStruct((LANES, T), f32),
        ],
        scratch_shapes=[pltpu.VMEM((tm, D_MODEL), bf16)],
        compiler_params=_cparams(("parallel", "arbitrary")),
        name="inproj",
    )(xa, xb, nw, w, b, ws, bs, wst, bst)


def _rope(x, cos, sin):
    lane = lax.broadcasted_iota(i32, x.shape, 1) % ATTN_HEAD_DIM
    half = ROPE_DIM // 2
    partner = jnp.where((lane >= half) & (lane < ROPE_DIM),
                        pltpu.roll(x, half, 1), pltpu.roll(x, LANES - half, 1))
    return x * cos + partner * sin


def _attn_kernel(sink_ref, q_ref, kp_ref, kc_ref, kn_ref, vp_ref, vc_ref, vn_ref,
                 cos_ref, sin_ref, o_ref, *, nb):
    n = pl.program_id(1)
    W = ATTN_BLOCK
    HD = ATTN_HEAD_DIM
    G = ATTN_HEADS // ATTN_KV_HEADS

    q0 = pl.multiple_of(n * W, W)
    cq, sq = cos_ref[pl.ds(q0, W), :], sin_ref[pl.ds(q0, W), :]
    qs = []
    for c in range(ATTN_Q_W // LANES):
        xq = q_ref[:, c * LANES:(c + 1) * LANES].astype(f32)
        qs.append((_rope(xq, cq, sq) * (HD ** -0.5)).astype(bf16))

    kblocks = (kp_ref, kc_ref, kn_ref)
    ks = [[], []]
    for bi in range(3):
        p0 = pl.multiple_of(jnp.clip(n + bi - 1, 0, nb - 1) * W, W)
        ck, sk = cos_ref[pl.ds(p0, W), :], sin_ref[pl.ds(p0, W), :]
        for c in range(ATTN_KV_W // LANES):
            xk = kblocks[bi][:, c * LANES:(c + 1) * LANES].astype(f32)
            ks[c].append(_rope(xk, ck, sk).astype(bf16))
    kcat = [jnp.concatenate(ks[c], axis=0) for c in range(2)]
    vcat = jnp.concatenate([vp_ref[...], vc_ref[...], vn_ref[...]], axis=0)

    r = lax.broadcasted_iota(i32, (W, 3 * W), 0)
    c = lax.broadcasted_iota(i32, (W, 3 * W), 1)
    valid = (c >= r) & (c <= r + 2 * W)
    valid = valid & ((n > 0) | (c >= W)) & ((n < nb - 1) | (c < 2 * W))
    valid4 = jnp.concatenate([valid] * G, axis=0)

    scores, sinks = [], []
    for kh in range(ATTN_KV_HEADS):
        kc_, ko = divmod(kh * HD, LANES)
        k_h = kcat[kc_][:, ko:ko + HD]
        parts = []
        sink_parts = []
        for g in range(G):
            h = kh * G + g
            qc, qo = divmod(h * HD, LANES)
            parts.append(qs[qc][:, qo:qo + HD])
            sink_parts.append(jnp.full((W, 1), sink_ref[h], f32))
        qg = jnp.concatenate(parts, axis=0)
        sinks.append(jnp.concatenate(sink_parts, axis=0))
        scores.append(lax.dot_general(qg, k_h, (((1,), (1,)), ((), ())), preferred_element_type=f32))
    probs, rdenoms = [], []
    for s, sink in zip(scores, sinks):
        s = jnp.where(valid4, s, NEG_INF)
        m = jnp.maximum(jnp.max(s, axis=-1, keepdims=True), sink)
        p = jnp.exp(s - m)
        rdenoms.append(1.0 / (jnp.sum(p, axis=-1, keepdims=True) + jnp.exp(sink - m)))
        probs.append(p.astype(bf16))
    outs = [jnp.dot(p, vcat[:, kh * HD:(kh + 1) * HD], preferred_element_type=f32)
            for kh, p in enumerate(probs)]
    for kh in range(ATTN_KV_HEADS):
        o = outs[kh] * rdenoms[kh]
        for g in range(G):
            h = kh * G + g
            o_ref[:, h * HD:(h + 1) * HD] = o[g * W:(g + 1) * W].astype(bf16)


def _attention(proj, sinks, cos_t, sin_t, nseq, seq):
    W = ATTN_BLOCK
    nb = seq // W
    T = nseq * seq
    qcol = P_AQ // ATTN_Q_W
    kcol = P_AK // ATTN_KV_W
    vcol = P_AV // ATTN_KV_W

    def kv_spec(col, d):
        return pl.BlockSpec(
            (W, ATTN_KV_W), lambda b, n: (b * nb + jnp.clip(n + d, 0, nb - 1), col))

    return pl.pallas_call(
        functools.partial(_attn_kernel, nb=nb),
        grid=(nseq, nb),
        in_specs=[
            pl.BlockSpec(memory_space=pltpu.SMEM),
            pl.BlockSpec((W, ATTN_Q_W), lambda b, n: (b * nb + n, qcol)),
            kv_spec(kcol, -1), kv_spec(kcol, 0), kv_spec(kcol, 1),
            kv_spec(vcol, -1), kv_spec(vcol, 0), kv_spec(vcol, 1),
            pl.BlockSpec(memory_space=pltpu.VMEM),
            pl.BlockSpec(memory_space=pltpu.VMEM),
        ],
        out_specs=pl.BlockSpec((W, ATTN_Q_W), lambda b, n: (b * nb + n, 0)),
        out_shape=jax.ShapeDtypeStruct((T, ATTN_Q_W), bf16),
        compiler_params=_cparams(("parallel", "parallel")),
        name="window_attn",
    )(sinks, proj, proj, proj, proj, proj, proj, proj, cos_t, sin_t)


def _dnconv_kernel(x_ref, w_ref, o_ref, pad_ref, *, seq, rt):
    j = pl.program_id(1)
    half = DN_CONV // 2
    cw = x_ref.shape[1]
    pad_ref[0:8, :] = jnp.zeros((8, cw), f32)
    pad_ref[seq + 8:seq + 16, :] = jnp.zeros((8, cw), f32)
    pad_ref[8:seq + 8, :] = x_ref[...].astype(f32)
    w = w_ref[...]
    qk_blocks = 2 * DN_QK_W // cw
    q_blocks = DN_QK_W // cw
    for r0 in range(0, seq, rt):
        acc = jnp.zeros((rt, cw), f32)
        for t in range(DN_CONV):
            s0 = r0 + 8 - half + t
            acc = acc + pad_ref[s0:s0 + rt, :] * w[t:t + 1, :]
        y = acc * _sigmoid(acc)
        outs = []
        for hh in range(cw // DN_K_DIM):
            seg = y[:, hh * DN_K_DIM:(hh + 1) * DN_K_DIM]
            inv = lax.rsqrt(jnp.sum(seg * seg, axis=-1, keepdims=True) + NORM_EPS)
            fac = jnp.where(j < q_blocks, inv * (DN_K_DIM ** -0.5),
                            jnp.where(j < qk_blocks, inv, jnp.ones_like(inv)))
            outs.append(seg * fac)
        o_ref[r0:r0 + rt, :] = jnp.concatenate(outs, axis=1).astype(bf16)


def _dnconv(proj, conv_w, nseq, seq):
    cw = 256
    T = nseq * seq
    col0 = P_DNQKV // cw
    rt = min(512, seq)
    return pl.pallas_call(
        functools.partial(_dnconv_kernel, seq=seq, rt=rt),
        grid=(nseq, DN_CONV_W // cw),
        in_specs=[
            pl.BlockSpec((seq, cw), lambda b, j: (b, col0 + j)),
            pl.BlockSpec((DN_CONV, cw), lambda b, j: (0, j)),
        ],
        out_specs=pl.BlockSpec((seq, cw), lambda b, j: (b, j)),
        out_shape=jax.ShapeDtypeStruct((T, DN_CONV_W), bf16),
        scratch_shapes=[pltpu.VMEM((seq + 16, cw), f32)],
        compiler_params=_cparams(("parallel", "parallel")),
        name="dn_conv",
    )(proj, conv_w)


def _mm(a, b):
    return jnp.dot(a.astype(bf16), b.astype(bf16), preferred_element_type=f32)


def _unit_tri_inverse_all(lmats, eye, blk_i, blk_j):
    same8 = (blk_i // 8) == (blk_j // 8)
    n1 = [jnp.where(same8, -l, 0.0) for l in lmats]
    p = [eye + n for n in n1]
    n2 = [_mm(n, n) for n in n1]
    p = [pp + _mm(pp, nn) for pp, nn in zip(p, n2)]
    n4 = [_mm(n, n) for n in n2]
    inv = [pp + _mm(pp, nn) for pp, nn in zip(p, n4)]
    for s in (8, 16, 32):
        coupling = ((blk_i // (2 * s)) == (blk_j // (2 * s))) & ((blk_i // s) != (blk_j // s))
        t = [_mm(jnp.where(coupling, l, 0.0), iv) for l, iv in zip(lmats, inv)]
        inv = [iv - _mm(iv, tt) for iv, tt in zip(inv, t)]
    return inv


def _chunk_cumsum(x, axis, reverse):
    n = x.shape[axis]
    pos = lax.broadcasted_iota(i32, x.shape, axis) % DN_CHUNK
    shift = 1
    while shift < DN_CHUNK:
        if reverse:
            moved = pltpu.roll(x, n - shift, axis)
            x = x + jnp.where(pos < DN_CHUNK - shift, moved, 0.0)
        else:
            moved = pltpu.roll(x, shift, axis)
            x = x + jnp.where(pos >= shift, moved, 0.0)
        shift *= 2
    return x


def _dnprep_kernel(q_ref, k_ref, v_ref, sm_ref, smt_ref, ar_ref, dr_ref, ac_ref, dc_ref,
                   u_ref, w_ref, qd_ref, kd_ref, aqk_ref, egl_ref):
    hq = pl.program_id(1)
    C = DN_CHUNK
    P = 2 * C
    TB = q_ref.shape[0]
    nch = TB // C

    sm = sm_ref[...]
    beta_full = _sigmoid(sm)
    g_full = -jnp.exp(ar_ref[...]) * _softplus(sm + dr_ref[...])
    shift = (LANES - hq * 8) % LANES
    beta_c = pltpu.roll(beta_full, shift, 1)
    g_c = pltpu.roll(g_full, shift, 1)
    smt = smt_ref[...]
    g_r = -jnp.exp(ac_ref[...]) * _softplus(smt + dc_ref[...])

    bi = lax.broadcasted_iota(i32, (P, P), 0)
    bj = lax.broadcasted_iota(i32, (P, P), 1)
    eye = (bi == bj).astype(f32)
    same_head = (bi // C) == (bj // C)

    gcum_c = [_chunk_cumsum(g_c, 0, d == 1) for d in range(2)]
    gcum_r = [_chunk_cumsum(g_r, 1, d == 1) for d in range(2)]

    k2, q2, kk2, qk2, v2 = [], [], [], [], []
    for c in range(nch):
        rows = slice(c * C, (c + 1) * C)
        k = k_ref[rows, :]
        q = q_ref[rows, :]
        kp = jnp.concatenate([k, k], axis=0)
        qp = jnp.concatenate([q, q], axis=0)
        k2.append(kp.astype(f32))
        q2.append(qp.astype(f32))
        kk2.append(lax.dot_general(kp, kp, (((1,), (1,)), ((), ())), preferred_element_type=f32))
        qk2.append(lax.dot_general(qp, kp, (((1,), (1,)), ((), ())), preferred_element_type=f32))
        v2.append(jnp.concatenate([v_ref[rows, :DN_V_DIM], v_ref[rows, DN_V_DIM:]], axis=0).astype(f32))

    inst = [(d, c) for d in range(2) for c in range(nch)]
    gcol, bcol, decay, lmats = [], [], [], []
    for d, c in inst:
        rows = slice(c * C, (c + 1) * C)
        b0, g0 = d * 2, 4 + d * 2
        gc = jnp.concatenate([gcum_c[d][rows, g0:g0 + 1], gcum_c[d][rows, g0 + 1:g0 + 2]], axis=0)
        gr = jnp.concatenate([gcum_r[d][g0:g0 + 1, rows], gcum_r[d][g0 + 1:g0 + 2, rows]], axis=1)
        bc = jnp.concatenate([beta_c[rows, b0:b0 + 1], beta_c[rows, b0 + 1:b0 + 2]], axis=0)
        incl = same_head & ((bj <= bi) if d == 0 else (bj >= bi))
        strict = same_head & ((bj < bi) if d == 0 else (bj > bi))
        dec = jnp.where(incl, jnp.exp(jnp.where(incl, gc - gr, 0.0)), 0.0)
        gcol.append(gc)
        bcol.append(bc)
        decay.append(dec)
        lmats.append(jnp.where(strict, kk2[c] * bc * dec, 0.0))

    invs = _unit_tri_inverse_all(lmats, eye, bi, bj)

    for n, (d, c) in enumerate(inst):
        rows = slice(c * C, (c + 1) * C)
        last = C - 1 if d == 0 else 0
        gc, bc = gcol[n], bcol[n]
        eg = jnp.exp(gc)
        rhs = jnp.concatenate([v2[c] * bc, k2[c] * (bc * eg)], axis=1)
        sol = _mm(invs[n], rhs)
        gl0 = gc[last:last + 1, :]
        gl1 = gc[C + last:C + last + 1, :]
        glast = jnp.concatenate([jnp.broadcast_to(gl0, (C, 1)), jnp.broadcast_to(gl1, (C, 1))], axis=0)
        qd = q2[c] * eg
        kd = k2[c] * jnp.exp(glast - gc)
        aqk = qk2[c] * decay[n]
        for hvl in range(2):
            pr = slice(hvl * C, (hvl + 1) * C)
            cols = slice(hvl * DN_V_DIM, (hvl + 1) * DN_V_DIM)
            u_ref[d, rows, cols] = sol[pr, :DN_V_DIM].astype(bf16)
            w_ref[d, rows, cols] = sol[pr, DN_V_DIM:].astype(bf16)
            qd_ref[d, rows, cols] = qd[pr].astype(bf16)
            kd_ref[d, rows, cols] = kd[pr].astype(bf16)
            aqk_ref[d, rows, pr] = aqk[pr, pr].astype(bf16)
        egl_ref[d, 0, c, 0:1, :] = jnp.broadcast_to(jnp.exp(gl0), (1, LANES))
        egl_ref[d, 0, c, 1:2, :] = jnp.broadcast_to(jnp.exp(gl1), (1, LANES))


def _dnprep(qkv, small, small_t, ar, dr, ac, dc):
    T = qkv.shape[0]
    TB = DN_TB
    nblk = T // TB
    nch = TB // DN_CHUNK
    kcol = DN_QK_W // DN_K_DIM
    vcol = 2 * DN_QK_W // (2 * DN_V_DIM)
    wide = lambda: pl.BlockSpec((2, TB, 2 * DN_V_DIM), lambda i, h: (0, i, h))
    big = jax.ShapeDtypeStruct((2, T, DN_V_W), bf16)
    return pl.pallas_call(
        _dnprep_kernel,
        grid=(nblk, DN_QK_HEADS),
        in_specs=[
            pl.BlockSpec((TB, DN_K_DIM), lambda i, h: (i, h)),
            pl.BlockSpec((TB, DN_K_DIM), lambda i, h: (i, kcol + h)),
            pl.BlockSpec((TB, 2 * DN_V_DIM), lambda i, h: (i, vcol + h)),
            pl.BlockSpec((TB, LANES), lambda i, h: (i, 0)),
            pl.BlockSpec((8, TB), lambda i, h: (h, i)),
            pl.BlockSpec((1, LANES), lambda i, h: (0, 0)),
            pl.BlockSpec((1, LANES), lambda i, h: (0, 0)),
            pl.BlockSpec((8, 1), lambda i, h: (h, 0)),
            pl.BlockSpec((8, 1), lambda i, h: (h, 0)),
        ],
        out_specs=[
            wide(), wide(), wide(), wide(),
            pl.BlockSpec((2, TB, 2 * DN_CHUNK), lambda i, h: (0, i, h)),
            pl.BlockSpec((2, 1, nch, 2, LANES), lambda i, h: (0, h, i, 0, 0)),
        ],
        out_shape=[
            big, big, big, big,
            jax.ShapeDtypeStruct((2, T, DN_QK_HEADS * 2 * DN_CHUNK), bf16),
            jax.ShapeDtypeStruct((2, DN_QK_HEADS, T // DN_CHUNK, 2, LANES), f32),
        ],
        compiler_params=_cparams(("parallel", "parallel")),
        name="dn_prep",
    )(qkv, qkv, qkv, small, small_t, ar, dr, ac, dc)


def _dnscan_kernel(uf, wf, qf, kf, af, ef, ub, wb, qb, kb, ab, eb, of_ref, ob_ref, s_ref):
    t = pl.program_id(1)
    C = DN_CHUNK
    nch = uf.shape[1] // C

    @pl.when(t == 0)
    def _():
        s_ref[...] = jnp.zeros(s_ref.shape, f32)

    dirs = ((uf, wf, qf, kf, af, ef, of_ref), (ub, wb, qb, kb, ab, eb, ob_ref))
    chains = [(d, hv) for d in range(2) for hv in range(DN_V_HEADS)]
    for step in range(nch):
        def rows(d):
            c = step if d == 0 else nch - 1 - step
            return c, slice(c * C, (c + 1) * C)

        def cols(hv):
            return slice(hv * DN_V_DIM, (hv + 1) * DN_V_DIM)

        s_old = [s_ref[d * DN_V_HEADS + hv] for d, hv in chains]
        s_bf = [s.astype(bf16) for s in s_old]
        w_s = [jnp.dot(dirs[d][1][0, rows(d)[1], cols(hv)], sb, preferred_element_type=f32)
               for (d, hv), sb in zip(chains, s_bf)]
        q_s = [jnp.dot(dirs[d][2][0, rows(d)[1], cols(hv)], sb, preferred_element_type=f32)
               for (d, hv), sb in zip(chains, s_bf)]
        v_bf = [(dirs[d][0][0, rows(d)[1], cols(hv)].astype(f32) - ws).astype(bf16)
                for (d, hv), ws in zip(chains, w_s)]
        a_v = [jnp.dot(dirs[d][4][0, rows(d)[1], hv * C:(hv + 1) * C], vb, preferred_element_type=f32)
               for (d, hv), vb in zip(chains, v_bf)]
        upd = [lax.dot_general(dirs[d][3][0, rows(d)[1], cols(hv)], vb, (((0,), (0,)), ((), ())),
                               preferred_element_type=f32)
               for (d, hv), vb in zip(chains, v_bf)]
        for n, (d, hv) in enumerate(chains):
            c, rs = rows(d)
            e = dirs[d][5][0, hv // 2, c, hv % 2:hv % 2 + 1, :]
            s_ref[d * DN_V_HEADS + hv] = s_old[n] * e + upd[n]
            dirs[d][6][rs, cols(hv)] = q_s[n] + a_v[n]


def _dnscan(u, w, qd, kd, aqk, egl, nseq, seq):
    TB = DN_TB
    nt = seq // TB
    nch = TB // DN_CHUNK
    T = nseq * seq

    def blk(d):
        if d == 0:
            return lambda b, t: b * nt + t
        return lambda b, t: b * nt + nt - 1 - t

    def specs(d):
        wide = lambda: pl.BlockSpec((1, TB, DN_V_W), lambda b, t: (d, blk(d)(b, t), 0))
        return [wide(), wide(), wide(), wide(),
                pl.BlockSpec((1, TB, DN_V_HEADS * DN_CHUNK), lambda b, t: (d, blk(d)(b, t), 0)),
                pl.BlockSpec((1, DN_QK_HEADS, nch, 2, LANES), lambda b, t: (d, 0, blk(d)(b, t), 0, 0))]

    def ospec(d):
        return pl.BlockSpec((TB, DN_V_W), lambda b, t: (blk(d)(b, t), 0))

    args = (u, w, qd, kd, aqk, egl)
    return pl.pallas_call(
        _dnscan_kernel,
        grid=(nseq, nt),
        in_specs=specs(0) + specs(1),
        out_specs=[ospec(0), ospec(1)],
        out_shape=[jax.ShapeDtypeStruct((T, DN_V_W), f32)] * 2,
        scratch_shapes=[pltpu.VMEM((2 * DN_V_HEADS, DN_K_DIM, DN_V_DIM), f32)],
        compiler_params=_cparams(("parallel", "arbitrary")),
        name="dn_scan",
    )(*args, *args)


def _post_kernel(xa_ref, xb_ref, oa_ref, of_ref, ob_ref, z_ref, ga_ref, gd_ref, dnw_ref, wa_ref, wd_ref,
                 wo_ref, n2_ref, rw_ref, rb_ref, x2_ref, hp_ref, idx_ref, gate_ref, *, npb):
    sub = POST_SUB
    subs = [slice(s * sub, (s + 1) * sub) for s in range(x2_ref.shape[0] // sub)]
    first_group = pl.program_id(0) < npb
    half = D_MODEL // 2

    odn = []
    for r in subs:
        od = of_ref[r, :] + ob_ref[r, :]
        z = z_ref[r, :].astype(f32)
        parts = []
        for h in range(DN_V_HEADS):
            cols = slice(h * DN_V_DIM, (h + 1) * DN_V_DIM)
            seg = od[:, cols]
            seg = seg * lax.rsqrt(jnp.mean(seg * seg, axis=-1, keepdims=True) + NORM_EPS) * dnw_ref[...]
            zz = z[:, cols]
            parts.append((seg * (zz * _sigmoid(zz))).astype(bf16))
        odn.append(jnp.concatenate(parts, axis=1))
    ya = [jnp.dot(oa_ref[r, :], wa_ref[...], preferred_element_type=f32) for r in subs]
    yd = [jnp.dot(o, wd_ref[...], preferred_element_type=f32) for o in odn]
    merged = [(_sigmoid(ga_ref[r, :].astype(f32)) * a + _sigmoid(gd_ref[r, :].astype(f32)) * d).astype(bf16)
              for r, a, d in zip(subs, ya, yd)]
    mo = [jnp.dot(m, wo_ref[...], preferred_element_type=f32) for m in merged]
    h2s = []
    for r, m in zip(subs, mo):
        x2 = jnp.where(first_group, xa_ref[r, :], xb_ref[r, :]) + m
        x2_ref[r, :] = x2
        h2s.append(x2 * lax.rsqrt(jnp.mean(x2 * x2, axis=-1, keepdims=True) + NORM_EPS) * n2_ref[...])
    logits = [lax.dot_general(rw_ref[...], h2, (((1,), (1,)), ((), ())), precision=lax.Precision.HIGHEST,
                              preferred_element_type=f32) + rb_ref[...] for h2 in h2s]
    for n, (r, h2) in enumerate(zip(subs, h2s)):
        packed = _pack_halves(h2[:, :half], h2[:, half:])
        for s in range(ROW_TILES):
            hp_ref[_row_tile(n * sub, sub, s), :] = packed[:, s * LANES:(s + 1) * LANES]
        cur = logits[n]
        erow = lax.broadcasted_iota(i32, cur.shape, 0).astype(f32)
        vals, idxs = [], []
        for _ in range(TOP_K):
            m = jnp.max(cur, axis=0, keepdims=True)
            idx = jnp.min(jnp.where(cur == m, erow, float(N_EXPERTS)), axis=0, keepdims=True)
            vals.append(m)
            idxs.append(idx)
            cur = jnp.where(erow == idx, -jnp.inf, cur)
        exps = [jnp.exp(v - vals[0]) for v in vals]
        rtot = 1.0 / (exps[0] + exps[1] + exps[2] + exps[3])
        out_row = lax.broadcasted_iota(i32, (8, sub), 0)
        idx_out = jnp.zeros((8, sub), f32)
        gate_out = jnp.zeros((8, sub), f32)
        for kk in range(TOP_K):
            idx_out = jnp.where(out_row == kk, idxs[kk], idx_out)
            gate_out = jnp.where(out_row == kk, exps[kk] * rtot, gate_out)
        idx_ref[:, r] = idx_out.astype(i32)
        gate_ref[:, r] = gate_out


def _post(xa, xb, o_attn, o_f, o_b, proj, dnw, wa, wd, wo, n2, rw, rb):
    T = xa.shape[0] + xb.shape[0]
    tm = min(POST_TM, xa.shape[0], xb.shape[0])
    assert xa.shape[0] % tm == 0 and xb.shape[0] % tm == 0
    npb = xa.shape[0] // tm
    zcol = P_DNZ // DN_V_W
    const = lambda shape: pl.BlockSpec(shape, lambda i: (0, 0))
    row = lambda width, col=0: pl.BlockSpec((tm, width), lambda i: (i, col))
    return pl.pallas_call(
        functools.partial(_post_kernel, npb=npb),
        grid=(T // tm,),
        in_specs=_group_row_specs(tm, D_MODEL, npb) + [
            row(ATTN_Q_W), row(DN_V_W), row(DN_V_W),
            row(DN_V_W, zcol), row(D_MODEL, P_GA // D_MODEL), row(D_MODEL, P_GD // D_MODEL),
            const((1, DN_V_DIM)), const((ATTN_Q_W, D_MODEL)), const((DN_V_W, D_MODEL)),
            const((D_MODEL, D_MODEL)), const((1, D_MODEL)), const((N_EXPERTS, D_MODEL)),
            const((N_EXPERTS, 1)),
        ],
        out_specs=[row(D_MODEL), pl.BlockSpec((tm * ROW_TILES, LANES), lambda i: (i, 0)),
                   pl.BlockSpec((8, tm), lambda i: (0, i)), pl.BlockSpec((8, tm), lambda i: (0, i))],
        out_shape=[
            jax.ShapeDtypeStruct((T, D_MODEL), f32),
            jax.ShapeDtypeStruct((T * ROW_TILES, LANES), u32),
            jax.ShapeDtypeStruct((8, T), i32),
            jax.ShapeDtypeStruct((8, T), f32),
        ],
        compiler_params=_cparams(("parallel",)),
        name="post_mix",
    )(xa, xb, o_attn, o_f, o_b, proj, proj, proj, dnw, wa, wd, wo, n2, rw, rb)


def _scatter_kernel(poff_ref, pn_ref, pos_ref, h_ref, xs_ref, zero_ref, sem, zsem):
    R = pos_ref.shape[2] // TOP_K
    nbits = MOE_TM.bit_length() - 1

    @pl.when(pl.program_id(0) == 0)
    def _():
        zero_ref[...] = jnp.zeros(zero_ref.shape, u32)

        def per_expert(e, carry):
            off = poff_ref[e]
            n = pn_ref[e]
            for bit in reversed(range(nbits)):
                size = 1 << bit
                take = ((n >> bit) & 1) == 1

                @pl.when(take)
                def _():
                    cp = pltpu.make_async_copy(zero_ref.at[_row_span(0, size)],
                                               xs_ref.at[_row_span(off, size)], zsem)
                    cp.start()
                    cp.wait()

                off = off + jnp.where(take, size, 0)
            return carry

        lax.fori_loop(0, N_EXPERTS, per_expert, 0)

        zrows = zero_ref.shape[0] // ROW_TILES

        def tail(j, carry):
            row = poff_ref[N_EXPERTS] + j * zrows
            cp = pltpu.make_async_copy(zero_ref, xs_ref.at[_row_span(row, zrows)], zsem)
            cp.start()
            cp.wait()
            return carry

        lax.fori_loop(0, pn_ref[N_EXPERTS], tail, 0)

    def body(a, carry):
        p = pos_ref[0, 0, a]
        pltpu.make_async_copy(h_ref.at[_row_span(a // TOP_K)], xs_ref.at[_row_span(p)], sem).start()
        return carry

    lax.fori_loop(0, TOP_K * R, body, 0, unroll=8)
    done = xs_ref.at[_row_span(0, TOP_K * R)]
    pltpu.make_async_copy(done, done, sem).wait()


def _scatter_rows(pad_off, pad_n, pos, hp, n_slots):
    T = hp.shape[0] // ROW_TILES
    R = min(SCATTER_R, T)
    grid_spec = pltpu.PrefetchScalarGridSpec(
        num_scalar_prefetch=2,
        grid=(T // R,),
        in_specs=[
            pl.BlockSpec((1, 1, TOP_K * R), lambda i, po, pn: (i, 0, 0), memory_space=pltpu.SMEM),
            pl.BlockSpec((R * ROW_TILES, LANES), lambda i, po, pn: (i, 0)),
        ],
        out_specs=pl.BlockSpec(memory_space=pl.ANY),
        scratch_shapes=[pltpu.VMEM((MOE_TM // 2 * ROW_TILES, LANES), u32),
                        pltpu.SemaphoreType.DMA(()), pltpu.SemaphoreType.DMA(())],
    )
    return pl.pallas_call(
        _scatter_kernel,
        grid_spec=grid_spec,
        out_shape=jax.ShapeDtypeStruct((n_slots * ROW_TILES, LANES), u32),
        compiler_params=_cparams(("arbitrary",)),
        name="moe_scatter",
    )(pad_off, pad_n, pos.reshape(T // R, 1, TOP_K * R), hp)


def _moe_kernel(te_ref, nu_ref, x_ref, wg_ref, wu_ref, bg_ref, bu_ref, wd_ref, bd_ref,
                o_ref, xb_ref, acc_ref):
    i = pl.program_id(0)
    f = pl.program_id(1)
    nf = pl.num_programs(1)
    half = D_MODEL // 2
    used = i < nu_ref[0]

    @pl.when(used & (f == 0))
    def _():
        for s in range(ROW_TILES):
            lo, hi = _unpack_halves(x_ref[_row_tile(0, xb_ref.shape[0], s), :])
            xb_ref[:, s * LANES:(s + 1) * LANES] = lo.astype(bf16)
            xb_ref[:, half + s * LANES:half + (s + 1) * LANES] = hi.astype(bf16)

    @pl.when(used)
    def _():
        xb = xb_ref[...]
        g = jnp.dot(xb, wg_ref[0, 0], preferred_element_type=f32) + bg_ref[0]
        u = jnp.dot(xb, wu_ref[0, 0], preferred_element_type=f32) + bu_ref[0]
        gate = jnp.minimum(g, SWIGLU_LIMIT)
        up = jnp.clip(u, -SWIGLU_LIMIT, SWIGLU_LIMIT)
        act = (up + 1.0) * (gate * _sigmoid(SWIGLU_ALPHA * gate))
        part = jnp.dot(act.astype(bf16), wd_ref[0], preferred_element_type=f32)

        @pl.when(f == 0)
        def _():
            acc_ref[...] = part

        @pl.when(f > 0)
        def _():
            acc_ref[...] += part

    @pl.when(used & (f == nf - 1))
    def _():
        y = acc_ref[...] + bd_ref[0]
        _store_row_tiles(o_ref, _pack_halves(y[:, :half], y[:, half:]))

    @pl.when(jnp.logical_not(used) & (f == nf - 1))
    def _():
        o_ref[...] = jnp.zeros(o_ref.shape, u32)


def _moe(tile_expert, n_used, xs, wg, wu, bg, bu, wd, bd):
    n_slots = xs.shape[0] // ROW_TILES
    tm, tf = MOE_TM, MOE_TF
    n_tiles = n_slots // tm
    nf = D_FF // tf

    def fidx(i, f, te, nu):
        return jnp.where(i < nu[0], f, nf - 1)

    grid_spec = pltpu.PrefetchScalarGridSpec(
        num_scalar_prefetch=2,
        grid=(n_tiles, nf),
        in_specs=[
            pl.BlockSpec((tm * ROW_TILES, LANES), lambda i, f, te, nu: (i, 0)),
            pl.BlockSpec((1, 1, D_MODEL, tf), lambda i, f, te, nu: (te[i], fidx(i, f, te, nu), 0, 0)),
            pl.BlockSpec((1, 1, D_MODEL, tf), lambda i, f, te, nu: (te[i], fidx(i, f, te, nu), 0, 0)),
            pl.BlockSpec((1, 1, tf), lambda i, f, te, nu: (te[i], 0, fidx(i, f, te, nu))),
            pl.BlockSpec((1, 1, tf), lambda i, f, te, nu: (te[i], 0, fidx(i, f, te, nu))),
            pl.BlockSpec((1, tf, D_MODEL), lambda i, f, te, nu: (te[i], fidx(i, f, te, nu), 0)),
            pl.BlockSpec((1, 1, D_MODEL), lambda i, f, te, nu: (te[i], 0, 0)),
        ],
        out_specs=pl.BlockSpec((tm * ROW_TILES, LANES), lambda i, f, te, nu: (i, 0)),
        scratch_shapes=[pltpu.VMEM((tm, D_MODEL), bf16), pltpu.VMEM((tm, D_MODEL), f32)],
    )
    return pl.pallas_call(
        _moe_kernel,
        grid_spec=grid_spec,
        out_shape=jax.ShapeDtypeStruct((n_slots * ROW_TILES, LANES), u32),
        compiler_params=_cparams(("arbitrary", "arbitrary")),
        name="moe_experts",
    )(tile_expert, n_used, xs, wg, wu, bg, bu, wd, bd)


def _combine_kernel(pos_ref, posn_ref, x2_ref, gate_ref, fw_ref, ys_ref, oa_ref, ob_ref,
                    buf_ref, acc_ref, sem, *, npb):
    R = x2_ref.shape[0]
    half = D_MODEL // 2
    nrows = TOP_K * R
    i = pl.program_id(0)
    slot = i % 2

    def issue(p_ref, dst_slot):
        def body(a, carry):
            p = p_ref[0, 0, a]
            pltpu.make_async_copy(ys_ref.at[_row_span(p)],
                                  buf_ref.at[_row_span(dst_slot * nrows + a)], sem.at[dst_slot]).start()
            return carry

        lax.fori_loop(0, nrows, body, 0, unroll=8)

    @pl.when(i == 0)
    def _():
        issue(pos_ref, 0)

    @pl.when(i + 1 < pl.num_programs(0))
    def _():
        issue(posn_ref, 1 - slot)

    pltpu.make_async_copy(ys_ref.at[_row_span(0, nrows)], buf_ref.at[_row_span(slot * nrows, nrows)],
                          sem.at[slot]).wait()
    row0 = slot * nrows

    gates = gate_ref[...]
    gk = [gates[:, k:k + 1] for k in range(TOP_K)]
    ss = jnp.zeros((R, 1), f32)
    for s in range(ROW_TILES):
        cl = slice(s * LANES, (s + 1) * LANES)
        ch = slice(half + s * LANES, half + (s + 1) * LANES)
        lo_acc = x2_ref[:, cl]
        hi_acc = x2_ref[:, ch]
        for k in range(TOP_K):
            lo, hi = _unpack_halves(buf_ref[_row_tile(row0 + k * R, R, s), :])
            lo_acc = lo_acc + gk[k] * lo
            hi_acc = hi_acc + gk[k] * hi
        ss = ss + jnp.sum(lo_acc * lo_acc, axis=-1, keepdims=True) + jnp.sum(
            hi_acc * hi_acc, axis=-1, keepdims=True)
        acc_ref[:, cl] = lo_acc
        acc_ref[:, ch] = hi_acc
    inv = lax.rsqrt(ss / D_MODEL + NORM_EPS)

    @pl.when(i < npb)
    def _():
        oa_ref[...] = acc_ref[...] * inv * fw_ref[...]

    @pl.when(i >= npb)
    def _():
        ob_ref[...] = acc_ref[...] * inv * fw_ref[...]


def _combine(pos, x2, gates, fw, ys, rows_a):
    T = x2.shape[0]
    R = min(COMB_R, rows_a, T - rows_a)
    assert rows_a % R == 0 and T % R == 0
    npb = rows_a // R
    nsteps = T // R
    return pl.pallas_call(
        functools.partial(_combine_kernel, npb=npb),
        grid=(nsteps,),
        in_specs=[
            pl.BlockSpec((1, 1, TOP_K * R), lambda i: (i, 0, 0), memory_space=pltpu.SMEM),
            pl.BlockSpec((1, 1, TOP_K * R), lambda i: (jnp.minimum(i + 1, nsteps - 1), 0, 0),
                         memory_space=pltpu.SMEM),
            pl.BlockSpec((R, D_MODEL), lambda i: (i, 0)),
            pl.BlockSpec((R, TOP_K), lambda i: (i, 0)),
            pl.BlockSpec((1, D_MODEL), lambda i: (0, 0)),
            pl.BlockSpec(memory_space=pl.ANY),
        ],
        out_specs=_group_row_specs(R, D_MODEL, npb),
        out_shape=[jax.ShapeDtypeStruct((rows_a, D_MODEL), f32),
                   jax.ShapeDtypeStruct((T - rows_a, D_MODEL), f32)],
        scratch_shapes=[pltpu.VMEM((2 * TOP_K * R * ROW_TILES, LANES), u32),
                        pltpu.VMEM((R, D_MODEL), f32),
                        pltpu.SemaphoreType.DMA((2,))],
        compiler_params=_cparams(("arbitrary",)),
        name="moe_combine",
    )(pos, pos, x2, gates, fw, ys)


def _dn_channels(t):
    lead = t.shape[:-1]
    t = t.reshape(lead + (2, 2, DN_QK_HEADS, 2))
    perm = tuple(range(len(lead))) + tuple(len(lead) + a for a in (2, 0, 1, 3))
    return t.transpose(perm).reshape(lead + (4 * DN_V_HEADS,))


def _pad_lanes(t):
    pad = [(0, 0)] * (t.ndim - 1) + [(0, LANES - t.shape[-1])]
    return jnp.pad(t, pad)


def _deint_kernel(w_ref, p_ref, g_ref, u_ref):
    for c in range(w_ref.shape[2] // (2 * LANES)):
        w = w_ref[0, :, c * 2 * LANES:(c + 1) * 2 * LANES].astype(bf16)
        r = jnp.dot(w, p_ref[...], preferred_element_type=f32)
        g_ref[0, 0, :, c * LANES:(c + 1) * LANES] = r[:, :LANES].astype(bf16)
        u_ref[0, 0, :, c * LANES:(c + 1) * LANES] = r[:, LANES:].astype(bf16)


def _deinterleave(w):
    E, D, F2 = w.shape
    rb, cb = 1024, 2 * MOE_TF
    i = jnp.arange(2 * LANES)
    src = jnp.where(i < LANES, 2 * i, 2 * (i - LANES) + 1)
    perm = (jnp.arange(2 * LANES)[:, None] == src[None, :]).astype(bf16)
    out = jax.ShapeDtypeStruct((E, F2 // cb, D, MOE_TF), bf16)
    return pl.pallas_call(
        _deint_kernel,
        grid=(E, D // rb, F2 // cb),
        in_specs=[pl.BlockSpec((1, rb, cb), lambda e, r, c: (e, r, c)),
                  pl.BlockSpec((2 * LANES, 2 * LANES), lambda e, r, c: (0, 0))],
        out_specs=[pl.BlockSpec((1, 1, rb, MOE_TF), lambda e, r, c: (e, c, r, 0))] * 2,
        out_shape=[out, out],
        compiler_params=_cparams(("parallel", "parallel", "parallel")),
        name="moe_deinterleave",
    )(w, perm)


def _routing(top_idx, T, tm):
    A = T * TOP_K
    n_tiles = A // tm + N_EXPERTS
    n_slots = n_tiles * tm
    flat_e = top_idx.reshape(-1)
    onehot = (flat_e[:, None] == jnp.arange(N_EXPERTS, dtype=i32)[None, :]).astype(i32)
    csum = jnp.cumsum(onehot, axis=0)
    counts = csum[-1]
    padded = ((counts + tm - 1) // tm) * tm
    padded_end = jnp.cumsum(padded)
    padded_start = padded_end - padded
    pos = jnp.sum(onehot * (csum - 1 + padded_start[None, :]), axis=1).astype(i32)
    tile_start = jnp.arange(n_tiles, dtype=i32) * tm
    tile_expert = jnp.minimum(
        jnp.sum(padded_end[None, :] <= tile_start[:, None], axis=1), N_EXPERTS - 1).astype(i32)
    n_used = (padded_end[-1] // tm).astype(i32).reshape(1)
    pad_off = jnp.concatenate([padded_start + counts, padded_end[-1:]]).astype(i32)
    pad_n = jnp.concatenate([padded - counts, (n_slots - padded_end[-1:]) // (tm // 2)]).astype(i32)
    return pos, pad_off, pad_n, tile_expert, n_used, n_slots


def _layer(xa, xb, seq, norm1_w, w_in, b_in, attn_sinks, dn_conv_w, dn_a_log, dn_dt_bias,
           dn_norm_w, w_attn_o, w_dn_o, w_out, norm2_w, router_w, router_b, w_gate_up, b_gate_up,
           w_down, b_down, final_norm_w):
    T = xa.shape[0] + xb.shape[0]
    nseq = T // seq

    order = ([_OFF_GA, D_MODEL], [_OFF_GD, D_MODEL], [_OFF_DNQKV, DN_CONV_W], [_OFF_AQ, ATTN_Q_W],
             [_OFF_DNZ, DN_V_W], [_OFF_AK, ATTN_KV_W], [_OFF_AV, ATTN_KV_W])
    w_main = jnp.concatenate([w_in[:, o:o + n] for o, n in order], axis=1).astype(bf16)
    w_main = w_main.reshape(D_MODEL, P_W // IN_TN, IN_TN).transpose(1, 0, 2)
    b_main = jnp.concatenate([b_in[o:o + n] for o, n in order])[None, :]
    nsm = 4 * DN_V_HEADS
    w_small = _pad_lanes(_dn_channels(w_in[:, _OFF_DNB:_OFF_DNB + nsm])).astype(bf16)
    b_small = _pad_lanes(_dn_channels(b_in[_OFF_DNB:_OFF_DNB + nsm]))
    zeros_b = jnp.zeros((2 * DN_V_HEADS,), f32)
    a_ch = _pad_lanes(_dn_channels(jnp.concatenate([zeros_b, dn_a_log.reshape(-1)])))
    d_ch = _pad_lanes(_dn_channels(jnp.concatenate([zeros_b, dn_dt_bias.reshape(-1)])))

    proj, small, small_t = _inproj(xa, xb, norm1_w[None, :], w_main, b_main, w_small, b_small[None, :],
                                   w_small.T, b_small[:, None])

    pos = jnp.arange(seq, dtype=f32)
    inv_freq = 1.0 / (ROPE_THETA ** (jnp.arange(0, ROPE_DIM, 2, dtype=f32) / ROPE_DIM))
    ang = pos[:, None] * inv_freq[None, :]
    cos, sin = jnp.cos(ang), jnp.sin(ang)
    half = ROPE_DIM // 2
    ones_tail = jnp.ones((seq, ATTN_HEAD_DIM - ROPE_DIM), f32)
    cos_head = jnp.concatenate([cos, cos, ones_tail], axis=1)
    sin_head = jnp.concatenate([-sin, sin, 0.0 * ones_tail], axis=1)
    cos_t = jnp.concatenate([cos_head, cos_head], axis=1)
    sin_t = jnp.concatenate([sin_head, sin_head], axis=1)

    o_attn = _attention(proj, attn_sinks, cos_t, sin_t, nseq, seq)

    qkv = _dnconv(proj, dn_conv_w, nseq, seq)
    u, w, qd, kd, aqk, egl = _dnprep(qkv, small, small_t, a_ch[None, :], d_ch[None, :],
                                     a_ch[:, None], d_ch[:, None])
    o_f, o_b = _dnscan(u, w, qd, kd, aqk, egl, nseq, seq)

    x2, hp, top_idx_t, top_gate_t = _post(
        xa, xb, o_attn, o_f, o_b, proj, dn_norm_w[None, :], w_attn_o.astype(bf16), w_dn_o.astype(bf16),
        w_out.astype(bf16), norm2_w[None, :], router_w.T, router_b[:, None])
    top_idx = top_idx_t[:TOP_K].T
    top_gate = top_gate_t[:TOP_K].T

    pos_a, pad_off, pad_n, tile_expert, n_used, n_slots = _routing(top_idx, T, MOE_TM)
    xs = _scatter_rows(pad_off, pad_n, pos_a, hp, n_slots)
    wg, wu = _deinterleave(w_gate_up)
    b_pairs = b_gate_up.reshape(N_EXPERTS, 1, D_FF, 2)
    bg, bu = b_pairs[..., 0], b_pairs[..., 1]
    ys = _moe(tile_expert, n_used, xs, wg, wu, bg, bu, w_down.astype(bf16), b_down[:, None, :])
    R = min(COMB_R, xa.shape[0], xb.shape[0])
    pos_tiles = pos_a.reshape(T // R, R, TOP_K).transpose(0, 2, 1).reshape(T // R, 1, TOP_K * R)
    return _combine(pos_tiles, x2, top_gate, final_norm_w[None, :], ys, xa.shape[0])


def kernel(x_prompt, x_sample, norm1_w, w_in, b_in, attn_sinks, dn_conv_w, dn_a_log, dn_dt_bias,
           dn_norm_w, w_attn_o, w_dn_o, w_out, norm2_w, router_w, router_b, w_gate_up, b_gate_up,
           w_down, b_down, final_norm_w):
    bp, seq, d = x_prompt.shape
    bs = x_sample.shape[0]
    assert x_sample.shape[1] == seq and d == D_MODEL
    ya, yb = _layer(x_prompt.reshape(bp * seq, d), x_sample.reshape(bs * seq, d), seq,
                    norm1_w[0], w_in[0], b_in[0], attn_sinks[0], dn_conv_w[0], dn_a_log[0],
                    dn_dt_bias[0], dn_norm_w[0], w_attn_o[0], w_dn_o[0], w_out[0], norm2_w[0],
                    router_w[0], router_b[0], w_gate_up[0], b_gate_up[0], w_down[0], b_down[0],
                    final_norm_w)
    return (ya.reshape(bp, seq, d), yb.reshape(bs, seq, d))
```

```python
import functools
import math

import jax
import jax.numpy as jnp
from jax import lax
from jax.experimental import pallas as pl
from jax.experimental.pallas import tpu as pltpu

f32 = jnp.float32
bf16 = jnp.bfloat16
u32 = jnp.uint32
i32 = jnp.int32

D_MODEL = 2048
ATTN_HEADS = 16
ATTN_KV_HEADS = 4
ATTN_HEAD_DIM = 64
ATTN_BLOCK = 128
ROPE_THETA = 500000.0
ROPE_DIM = ATTN_HEAD_DIM // 4
DN_QK_HEADS = 4
DN_V_HEADS = 8
DN_K_DIM = 128
DN_V_DIM = 128
DN_CONV = 5
DN_CHUNK = 64
N_EXPERTS = 32
TOP_K = 4
D_FF = D_MODEL
SWIGLU_LIMIT = 7.0
SWIGLU_ALPHA = 1.702
NORM_EPS = 1e-6
NEG_INF = -1e30

ATTN_Q_W = ATTN_HEADS * ATTN_HEAD_DIM
ATTN_KV_W = ATTN_KV_HEADS * ATTN_HEAD_DIM
DN_QK_W = DN_QK_HEADS * DN_K_DIM
DN_V_W = DN_V_HEADS * DN_V_DIM
DN_CONV_W = 2 * DN_QK_W + DN_V_W

_OFF_AQ = 0
_OFF_AK = _OFF_AQ + ATTN_Q_W
_OFF_AV = _OFF_AK + ATTN_KV_W
_OFF_DNQKV = _OFF_AV + ATTN_KV_W
_OFF_DNZ = _OFF_DNQKV + DN_CONV_W
_OFF_DNB = _OFF_DNZ + DN_V_W
_OFF_DNA = _OFF_DNB + 2 * DN_V_HEADS
_OFF_GA = _OFF_DNA + 2 * DN_V_HEADS
_OFF_GD = _OFF_GA + D_MODEL

P_GA = 0
P_GD = P_GA + D_MODEL
P_DNQKV = P_GD + D_MODEL
P_AQ = P_DNQKV + DN_CONV_W
P_DNZ = P_AQ + ATTN_Q_W
P_AK = P_DNZ + DN_V_W
P_AV = P_AK + ATTN_KV_W
P_W = P_AV + ATTN_KV_W

LANES = 128
VMEM_LIMIT = 56 * 1024 * 1024

IN_TM = 1024
IN_TN = 512
DN_TB = 256
POST_TM = 256
POST_SUB = 128
MOE_TM = 512
MOE_TF = 1024
SCATTER_R = 1024
COMB_R = 128


def _cparams(sem):
    return pltpu.CompilerParams(dimension_semantics=sem, vmem_limit_bytes=VMEM_LIMIT)


def _sigmoid(x):
    return 1.0 / (1.0 + jnp.exp(-x))


def _softplus(x):
    return jnp.maximum(x, 0.0) + jnp.log(1.0 + jnp.exp(-jnp.abs(x)))


def _pack_halves(lo, hi):
    a = lax.bitcast_convert_type(lo.astype(bf16).astype(f32), u32)
    b = lax.bitcast_convert_type(hi.astype(bf16).astype(f32), u32)
    return (a >> 16) | (b & jnp.uint32(0xFFFF0000))


def _unpack_halves(w):
    lo = lax.bitcast_convert_type(w << 16, f32)
    hi = lax.bitcast_convert_type(w & jnp.uint32(0xFFFF0000), f32)
    return lo, hi


ROW_TILES = D_MODEL // 2 // LANES


def _row_tile(r0, n, s):
    start = r0 * ROW_TILES
    if not isinstance(r0, int):
        start = pl.multiple_of(start, ROW_TILES)
    return pl.ds(start + s, n, stride=ROW_TILES)


def _row_span(r, n=1):
    start = r * ROW_TILES
    if not isinstance(r, int):
        start = pl.multiple_of(start, ROW_TILES)
    return pl.ds(start, n * ROW_TILES)


def _store_row_tiles(ref, packed):
    for s in range(ROW_TILES):
        ref[_row_tile(0, packed.shape[0], s), :] = packed[:, s * LANES:(s + 1) * LANES]


def _group_row_specs(tm, width, npb):
    first = pl.BlockSpec((tm, width), lambda i, *_: (jnp.minimum(i, npb - 1), 0))
    second = pl.BlockSpec((tm, width), lambda i, *_: (jnp.maximum(i - npb, 0), 0))
    return [first, second]


def _inproj_kernel(xa_ref, xb_ref, nw_ref, w_ref, b_ref, ws_ref, bs_ref, wst_ref, bst_ref,
                   o_ref, os_ref, ost_ref, h_ref, *, npb):
    @pl.when(pl.program_id(1) == 0)
    def _():
        x = jnp.where(pl.program_id(0) < npb, xa_ref[...], xb_ref[...])
        ms = jnp.mean(x * x, axis=-1, keepdims=True)
        hb = (x * lax.rsqrt(ms + NORM_EPS) * nw_ref[...]).astype(bf16)
        h_ref[...] = hb
        os_ref[...] = jnp.dot(hb, ws_ref[...], preferred_element_type=f32) + bs_ref[...]
        ost_ref[...] = lax.dot_general(wst_ref[...], hb, (((1,), (1,)), ((), ())),
                                       preferred_element_type=f32) + bst_ref[...]

    acc = jnp.dot(h_ref[...], w_ref[0], preferred_element_type=f32)
    o_ref[...] = (acc + b_ref[...]).astype(bf16)


def _inproj(xa, xb, nw, w, b, ws, bs, wst, bst):
    T = xa.shape[0] + xb.shape[0]
    tm = min(IN_TM, xa.shape[0], xb.shape[0])
    assert xa.shape[0] % tm == 0 and xb.shape[0] % tm == 0
    npb = xa.shape[0] // tm
    grid = (T // tm, P_W // IN_TN)
    return pl.pallas_call(
        functools.partial(_inproj_kernel, npb=npb),
        grid=grid,
        in_specs=_group_row_specs(tm, D_MODEL, npb) + [
            pl.BlockSpec((1, D_MODEL), lambda i, j: (0, 0)),
            pl.BlockSpec((1, D_MODEL, IN_TN), lambda i, j: (j, 0, 0)),
            pl.BlockSpec((1, IN_TN), lambda i, j: (0, j)),
            pl.BlockSpec((D_MODEL, LANES), lambda i, j: (0, 0)),
            pl.BlockSpec((1, LANES), lambda i, j: (0, 0)),
            pl.BlockSpec((LANES, D_MODEL), lambda i, j: (0, 0)),
            pl.BlockSpec((LANES, 1), lambda i, j: (0, 0)),
        ],
        out_specs=[
            pl.BlockSpec((tm, IN_TN), lambda i, j: (i, j)),
            pl.BlockSpec((tm, LANES), lambda i, j: (i, 0)),
            pl.BlockSpec((LANES, tm), lambda i, j: (0, i)),
        ],
        out_shape=[
            jax.ShapeDtypeStruct((T, P_W), bf16),
            jax.ShapeDtypeStruct((T, LANES), f32),
            jax.ShapeDtypeStruct((LANES, T), f32),
        ],
        scratch_shapes=[pltpu.VMEM((tm, D_MODEL), bf16)],
        compiler_params=_cparams(("parallel", "arbitrary")),
        name="inproj",
    )(xa, xb, nw, w, b, ws, bs, wst, bst)


def _rope(x, cos, sin):
    lane = lax.broadcasted_iota(i32, x.shape, 1) % ATTN_HEAD_DIM
    half = ROPE_DIM // 2
    partner = jnp.where((lane >= half) & (lane < ROPE_DIM),
                        pltpu.roll(x, half, 1), pltpu.roll(x, LANES - half, 1))
    return x * cos + partner * sin


def _attn_kernel(sink_ref, q_ref, kp_ref, kc_ref, kn_ref, vp_ref, vc_ref, vn_ref,
                 cos_ref, sin_ref, o_ref, *, nb):
    n = pl.program_id(1)
    W = ATTN_BLOCK
    HD = ATTN_HEAD_DIM
    G = ATTN_HEADS // ATTN_KV_HEADS

    q0 = pl.multiple_of(n * W, W)
    cq, sq = cos_ref[pl.ds(q0, W), :], sin_ref[pl.ds(q0, W), :]
    qs = []
    for c in range(ATTN_Q_W // LANES):
        xq = q_ref[:, c * LANES:(c + 1) * LANES].astype(f32)
        qs.append((_rope(xq, cq, sq) * (HD ** -0.5)).astype(bf16))

    kblocks = (kp_ref, kc_ref, kn_ref)
    ks = [[], []]
    for bi in range(3):
        p0 = pl.multiple_of(jnp.clip(n + bi - 1, 0, nb - 1) * W, W)
        ck, sk = cos_ref[pl.ds(p0, W), :], sin_ref[pl.ds(p0, W), :]
        for c in range(ATTN_KV_W // LANES):
            xk = kblocks[bi][:, c * LANES:(c + 1) * LANES].astype(f32)
            ks[c].append(_rope(xk, ck, sk).astype(bf16))
    kcat = [jnp.concatenate(ks[c], axis=0) for c in range(2)]
    vcat = jnp.concatenate([vp_ref[...], vc_ref[...], vn_ref[...]], axis=0)

    r = lax.broadcasted_iota(i32, (W, 3 * W), 0)
    c = lax.broadcasted_iota(i32, (W, 3 * W), 1)
    valid = (c >= r) & (c <= r + 2 * W)
    valid = valid & ((n > 0) | (c >= W)) & ((n < nb - 1) | (c < 2 * W))
    valid4 = jnp.concatenate([valid] * G, axis=0)

    scores, sinks = [], []
    for kh in range(ATTN_KV_HEADS):
        kc_, ko = divmod(kh * HD, LANES)
        k_h = kcat[kc_][:, ko:ko + HD]
        parts = []
        sink_parts = []
        for g in range(G):
            h = kh * G + g
            qc, qo = divmod(h * HD, LANES)
            parts.append(qs[qc][:, qo:qo + HD])
            sink_parts.append(jnp.full((W, 1), sink_ref[h], f32))
        qg = jnp.concatenate(parts, axis=0)
        sinks.append(jnp.concatenate(sink_parts, axis=0))
        scores.append(lax.dot_general(qg, k_h, (((1,), (1,)), ((), ())), preferred_element_type=f32))
    probs, rdenoms = [], []
    for s, sink in zip(scores, sinks):
        s = jnp.where(valid4, s, NEG_INF)
        m = jnp.maximum(jnp.max(s, axis=-1, keepdims=True), sink)
        p = jnp.exp(s - m)
        rdenoms.append(1.0 / (jnp.sum(p, axis=-1, keepdims=True) + jnp.exp(sink - m)))
        probs.append(p.astype(bf16))
    outs = [jnp.dot(p, vcat[:, kh * HD:(kh + 1) * HD], preferred_element_type=f32)
            for kh, p in enumerate(probs)]
    for kh in range(ATTN_KV_HEADS):
        o = outs[kh] * rdenoms[kh]
        for g in range(G):
            h = kh * G + g
            o_ref[:, h * HD:(h + 1) * HD] = o[g * W:(g + 1) * W].astype(bf16)


def _attention(proj, sinks, cos_t, sin_t, nseq, seq):
    W = ATTN_BLOCK
    nb = seq // W
    T = nseq * seq
    qcol = P_AQ // ATTN_Q_W
    kcol = P_AK // ATTN_KV_W
    vcol = P_AV // ATTN_KV_W

    def kv_spec(col, d):
        return pl.BlockSpec(
            (W, ATTN_KV_W), lambda b, n: (b * nb + jnp.clip(n + d, 0, nb - 1), col))

    return pl.pallas_call(
        functools.partial(_attn_kernel, nb=nb),
        grid=(nseq, nb),
        in_specs=[
            pl.BlockSpec(memory_space=pltpu.SMEM),
            pl.BlockSpec((W, ATTN_Q_W), lambda b, n: (b * nb + n, qcol)),
            kv_spec(kcol, -1), kv_spec(kcol, 0), kv_spec(kcol, 1),
            kv_spec(vcol, -1), kv_spec(vcol, 0), kv_spec(vcol, 1),
            pl.BlockSpec(memory_space=pltpu.VMEM),
            pl.BlockSpec(memory_space=pltpu.VMEM),
        ],
        out_specs=pl.BlockSpec((W, ATTN_Q_W), lambda b, n: (b * nb + n, 0)),
        out_shape=jax.ShapeDtypeStruct((T, ATTN_Q_W), bf16),
        compiler_params=_cparams(("parallel", "parallel")),
        name="window_attn",
    )(sinks, proj, proj, proj, proj, proj, proj, proj, cos_t, sin_t)


def _dnconv_kernel(x_ref, w_ref, o_ref, pad_ref, *, seq, rt):
    j = pl.program_id(1)
    half = DN_CONV // 2
    cw = x_ref.shape[1]
    pad_ref[0:8, :] = jnp.zeros((8, cw), f32)
    pad_ref[seq + 8:seq + 16, :] = jnp.zeros((8, cw), f32)
    pad_ref[8:seq + 8, :] = x_ref[...].astype(f32)
    w = w_ref[...]
    qk_blocks = 2 * DN_QK_W // cw
    q_blocks = DN_QK_W // cw
    for r0 in range(0, seq, rt):
        acc = jnp.zeros((rt, cw), f32)
        for t in range(DN_CONV):
            s0 = r0 + 8 - half + t
            acc = acc + pad_ref[s0:s0 + rt, :] * w[t:t + 1, :]
        y = acc * _sigmoid(acc)
        outs = []
        for hh in range(cw // DN_K_DIM):
            seg = y[:, hh * DN_K_DIM:(hh + 1) * DN_K_DIM]
            inv = lax.rsqrt(jnp.sum(seg * seg, axis=-1, keepdims=True) + NORM_EPS)
            fac = jnp.where(j < q_blocks, inv * (DN_K_DIM ** -0.5),
                            jnp.where(j < qk_blocks, inv, jnp.ones_like(inv)))
            outs.append(seg * fac)
        o_ref[r0:r0 + rt, :] = jnp.concatenate(outs, axis=1).astype(bf16)


def _dnconv(proj, conv_w, nseq, seq):
    cw = 256
    T = nseq * seq
    col0 = P_DNQKV // cw
    rt = min(512, seq)
    return pl.pallas_call(
        functools.partial(_dnconv_kernel, seq=seq, rt=rt),
        grid=(nseq, DN_CONV_W // cw),
        in_specs=[
            pl.BlockSpec((seq, cw), lambda b, j: (b, col0 + j)),
            pl.BlockSpec((DN_CONV, cw), lambda b, j: (0, j)),
        ],
        out_specs=pl.BlockSpec((seq, cw), lambda b, j: (b, j)),
        out_shape=jax.ShapeDtypeStruct((T, DN_CONV_W), bf16),
        scratch_shapes=[pltpu.VMEM((seq + 16, cw), f32)],
        compiler_params=_cparams(("parallel", "parallel")),
        name="dn_conv",
    )(proj, conv_w)


def _mm(a, b):
    return jnp.dot(a.astype(bf16), b.astype(bf16), preferred_element_type=f32)


def _unit_tri_inverse_all(lmats, eye, blk_i, blk_j):
    same8 = (blk_i // 8) == (blk_j // 8)
    n1 = [jnp.where(same8, -l, 0.0) for l in lmats]
    p = [eye + n for n in n1]
    n2 = [_mm(n, n) for n in n1]
    p = [pp + _mm(pp, nn) for pp, nn in zip(p, n2)]
    n4 = [_mm(n, n) for n in n2]
    inv = [pp + _mm(pp, nn) for pp, nn in zip(p, n4)]
    for s in (8, 16, 32):
        coupling = ((blk_i // (2 * s)) == (blk_j // (2 * s))) & ((blk_i // s) != (blk_j // s))
        t = [_mm(jnp.where(coupling, l, 0.0), iv) for l, iv in zip(lmats, inv)]
        inv = [iv - _mm(iv, tt) for iv, tt in zip(inv, t)]
    return inv


def _chunk_cumsum(x, axis, reverse):
    n = x.shape[axis]
    pos = lax.broadcasted_iota(i32, x.shape, axis) % DN_CHUNK
    shift = 1
    while shift < DN_CHUNK:
        if reverse:
            moved = pltpu.roll(x, n - shift, axis)
            x = x + jnp.where(pos < DN_CHUNK - shift, moved, 0.0)
        else:
            moved = pltpu.roll(x, shift, axis)
            x = x + jnp.where(pos >= shift, moved, 0.0)
        shift *= 2
    return x


def _dnprep_kernel(q_ref, k_ref, v_ref, sm_ref, smt_ref, ar_ref, dr_ref, ac_ref, dc_ref,
                   u_ref, w_ref, qd_ref, kd_ref, aqk_ref, egl_ref):
    hq = pl.program_id(1)
    C = DN_CHUNK
    P = 2 * C
    TB = q_ref.shape[0]
    nch = TB // C

    sm = sm_ref[...]
    beta_full = _sigmoid(sm)
    g_full = -jnp.exp(ar_ref[...]) * _softplus(sm + dr_ref[...])
    shift = (LANES - hq * 8) % LANES
    beta_c = pltpu.roll(beta_full, shift, 1)
    g_c = pltpu.roll(g_full, shift, 1)
    smt = smt_ref[...]
    g_r = -jnp.exp(ac_ref[...]) * _softplus(smt + dc_ref[...])

    bi = lax.broadcasted_iota(i32, (P, P), 0)
    bj = lax.broadcasted_iota(i32, (P, P), 1)
    eye = (bi == bj).astype(f32)
    same_head = (bi // C) == (bj // C)

    gcum_c = [_chunk_cumsum(g_c, 0, d == 1) for d in range(2)]
    gcum_r = [_chunk_cumsum(g_r, 1, d == 1) for d in range(2)]

    k2, q2, kk2, qk2, v2 = [], [], [], [], []
    for c in range(nch):
        rows = slice(c * C, (c + 1) * C)
        k = k_ref[rows, :]
        q = q_ref[rows, :]
        kp = jnp.concatenate([k, k], axis=0)
        qp = jnp.concatenate([q, q], axis=0)
        k2.append(kp.astype(f32))
        q2.append(qp.astype(f32))
        kk2.append(lax.dot_general(kp, kp, (((1,), (1,)), ((), ())), preferred_element_type=f32))
        qk2.append(lax.dot_general(qp, kp, (((1,), (1,)), ((), ())), preferred_element_type=f32))
        v2.append(jnp.concatenate([v_ref[rows, :DN_V_DIM], v_ref[rows, DN_V_DIM:]], axis=0).astype(f32))

    inst = [(d, c) for d in range(2) for c in range(nch)]
    gcol, bcol, decay, lmats = [], [], [], []
    for d, c in inst:
        rows = slice(c * C, (c + 1) * C)
        b0, g0 = d * 2, 4 + d * 2
        gc = jnp.concatenate([gcum_c[d][rows, g0:g0 + 1], gcum_c[d][rows, g0 + 1:g0 + 2]], axis=0)
        gr = jnp.concatenate([gcum_r[d][g0:g0 + 1, rows], gcum_r[d][g0 + 1:g0 + 2, rows]], axis=1)
        bc = jnp.concatenate([beta_c[rows, b0:b0 + 1], beta_c[rows, b0 + 1:b0 + 2]], axis=0)
        incl = same_head & ((bj <= bi) if d == 0 else (bj >= bi))
        strict = same_head & ((bj < bi) if d == 0 else (bj > bi))
        dec = jnp.where(incl, jnp.exp(jnp.where(incl, gc - gr, 0.0)), 0.0)
        gcol.append(gc)
        bcol.append(bc)
        decay.append(dec)
        lmats.append(jnp.where(strict, kk2[c] * bc * dec, 0.0))

    invs = _unit_tri_inverse_all(lmats, eye, bi, bj)

    for n, (d, c) in enumerate(inst):
        rows = slice(c * C, (c + 1) * C)
        last = C - 1 if d == 0 else 0
        gc, bc = gcol[n], bcol[n]
        eg = jnp.exp(gc)
        rhs = jnp.concatenate([v2[c] * bc, k2[c] * (bc * eg)], axis=1)
        sol = _mm(invs[n], rhs)
        gl0 = gc[last:last + 1, :]
        gl1 = gc[C + last:C + last + 1, :]
        glast = jnp.concatenate([jnp.broadcast_to(gl0, (C, 1)), jnp.broadcast_to(gl1, (C, 1))], axis=0)
        qd = q2[c] * eg
        kd = k2[c] * jnp.exp(glast - gc)
        aqk = qk2[c] * decay[n]
        for hvl in range(2):
            pr = slice(hvl * C, (hvl + 1) * C)
            cols = slice(hvl * DN_V_DIM, (hvl + 1) * DN_V_DIM)
            u_ref[d, rows, cols] = sol[pr, :DN_V_DIM].astype(bf16)
            w_ref[d, rows, cols] = sol[pr, DN_V_DIM:].astype(bf16)
            qd_ref[d, rows, cols] = qd[pr].astype(bf16)
            kd_ref[d, rows, cols] = kd[pr].astype(bf16)
            aqk_ref[d, rows, pr] = aqk[pr, pr].astype(bf16)
        egl_ref[d, 0, c, 0:1, :] = jnp.broadcast_to(jnp.exp(gl0), (1, LANES))
        egl_ref[d, 0, c, 1:2, :] = jnp.broadcast_to(jnp.exp(gl1), (1, LANES))


def _dnprep(qkv, small, small_t, ar, dr, ac, dc):
    T = qkv.shape[0]
    TB = DN_TB
    nblk = T // TB
    nch = TB // DN_CHUNK
    kcol = DN_QK_W // DN_K_DIM
    vcol = 2 * DN_QK_W // (2 * DN_V_DIM)
    wide = lambda: pl.BlockSpec((2, TB, 2 * DN_V_DIM), lambda i, h: (0, i, h))
    big = jax.ShapeDtypeStruct((2, T, DN_V_W), bf16)
    return pl.pallas_call(
        _dnprep_kernel,
        grid=(nblk, DN_QK_HEADS),
        in_specs=[
            pl.BlockSpec((TB, DN_K_DIM), lambda i, h: (i, h)),
            pl.BlockSpec((TB, DN_K_DIM), lambda i, h: (i, kcol + h)),
            pl.BlockSpec((TB, 2 * DN_V_DIM), lambda i, h: (i, vcol + h)),
            pl.BlockSpec((TB, LANES), lambda i, h: (i, 0)),
            pl.BlockSpec((8, TB), lambda i, h: (h, i)),
            pl.BlockSpec((1, LANES), lambda i, h: (0, 0)),
            pl.BlockSpec((1, LANES), lambda i, h: (0, 0)),
            pl.BlockSpec((8, 1), lambda i, h: (h, 0)),
            pl.BlockSpec((8, 1), lambda i, h: (h, 0)),
        ],
        out_specs=[
            wide(), wide(), wide(), wide(),
            pl.BlockSpec((2, TB, 2 * DN_CHUNK), lambda i, h: (0, i, h)),
            pl.BlockSpec((2, 1, nch, 2, LANES), lambda i, h: (0, h, i, 0, 0)),
        ],
        out_shape=[
            big, big, big, big,
            jax.ShapeDtypeStruct((2, T, DN_QK_HEADS * 2 * DN_CHUNK), bf16),
            jax.ShapeDtypeStruct((2, DN_QK_HEADS, T // DN_CHUNK, 2, LANES), f32),
        ],
        compiler_params=_cparams(("parallel", "parallel")),
        name="dn_prep",
    )(qkv, qkv, qkv, small, small_t, ar, dr, ac, dc)


def _dnscan_kernel(uf, wf, qf, kf, af, ef, ub, wb, qb, kb, ab, eb, of_ref, ob_ref, s_ref):
    t = pl.program_id(1)
    C = DN_CHUNK
    nch = uf.shape[1] // C

    @pl.when(t == 0)
    def _():
        s_ref[...] = jnp.zeros(s_ref.shape, f32)

    dirs = ((uf, wf, qf, kf, af, ef, of_ref), (ub, wb, qb, kb, ab, eb, ob_ref))
    chains = [(d, hv) for d in range(2) for hv in range(DN_V_HEADS)]
    for step in range(nch):
        def rows(d):
            c = step if d == 0 else nch - 1 - step
            return c, slice(c * C, (c + 1) * C)

        def cols(hv):
            return slice(hv * DN_V_DIM, (hv + 1) * DN_V_DIM)

        s_old = [s_ref[d * DN_V_HEADS + hv] for d, hv in chains]
        s_bf = [s.astype(bf16) for s in s_old]
        w_s = [jnp.dot(dirs[d][1][0, rows(d)[1], cols(hv)], sb, preferred_element_type=f32)
               for (d, hv), sb in zip(chains, s_bf)]
        q_s = [jnp.dot(dirs[d][2][0, rows(d)[1], cols(hv)], sb, preferred_element_type=f32)
               for (d, hv), sb in zip(chains, s_bf)]
        v_bf = [(dirs[d][0][0, rows(d)[1], cols(hv)].astype(f32) - ws).astype(bf16)
                for (d, hv), ws in zip(chains, w_s)]
        a_v = [jnp.dot(dirs[d][4][0, rows(d)[1], hv * C:(hv + 1) * C], vb, preferred_element_type=f32)
               for (d, hv), vb in zip(chains, v_bf)]
        upd = [lax.dot_general(dirs[d][3][0, rows(d)[1], cols(hv)], vb, (((0,), (0,)), ((), ())),
                               preferred_element_type=f32)
               for (d, hv), vb in zip(chains, v_bf)]
        for n, (d, hv) in enumerate(chains):
            c, rs = rows(d)
            e = dirs[d][5][0, hv // 2, c, hv % 2:hv % 2 + 1, :]
            s_ref[d * DN_V_HEADS + hv] = s_old[n] * e + upd[n]
            dirs[d][6][rs, cols(hv)] = q_s[n] + a_v[n]


def _dnscan(u, w, qd, kd, aqk, egl, nseq, seq):
    TB = DN_TB
    nt = seq // TB
    nch = TB // DN_CHUNK
    T = nseq * seq

    def blk(d):
        if d == 0:
            return lambda b, t: b * nt + t
        return lambda b, t: b * nt + nt - 1 - t

    def specs(d):
        wide = lambda: pl.BlockSpec((1, TB, DN_V_W), lambda b, t: (d, blk(d)(b, t), 0))
        return [wide(), wide(), wide(), wide(),
                pl.BlockSpec((1, TB, DN_V_HEADS * DN_CHUNK), lambda b, t: (d, blk(d)(b, t), 0)),
                pl.BlockSpec((1, DN_QK_HEADS, nch, 2, LANES), lambda b, t: (d, 0, blk(d)(b, t), 0, 0))]

    def ospec(d):
        return pl.BlockSpec((TB, DN_V_W), lambda b, t: (blk(d)(b, t), 0))

    args = (u, w, qd, kd, aqk, egl)
    return pl.pallas_call(
        _dnscan_kernel,
        grid=(nseq, nt),
        in_specs=specs(0) + specs(1),
        out_specs=[ospec(0), ospec(1)],
        out_shape=[jax.ShapeDtypeStruct((T, DN_V_W), f32)] * 2,
        scratch_shapes=[pltpu.VMEM((2 * DN_V_HEADS, DN_K_DIM, DN_V_DIM), f32)],
        compiler_params=_cparams(("parallel", "arbitrary")),
        name="dn_scan",
    )(*args, *args)


def _post_kernel(xa_ref, xb_ref, oa_ref, of_ref, ob_ref, z_ref, ga_ref, gd_ref, dnw_ref, wa_ref, wd_ref,
                 wo_ref, n2_ref, rw_ref, rb_ref, x2_ref, hp_ref, idx_ref, gate_ref, *, npb):
    sub = POST_SUB
    subs = [slice(s * sub, (s + 1) * sub) for s in range(x2_ref.shape[0] // sub)]
    first_group = pl.program_id(0) < npb
    half = D_MODEL // 2

    odn = []
    for r in subs:
        od = of_ref[r, :] + ob_ref[r, :]
        z = z_ref[r, :].astype(f32)
        parts = []
        for h in range(DN_V_HEADS):
            cols = slice(h * DN_V_DIM, (h + 1) * DN_V_DIM)
            seg = od[:, cols]
            seg = seg * lax.rsqrt(jnp.mean(seg * seg, axis=-1, keepdims=True) + NORM_EPS) * dnw_ref[...]
            zz = z[:, cols]
            parts.append((seg * (zz * _sigmoid(zz))).astype(bf16))
        odn.append(jnp.concatenate(parts, axis=1))
    ya = [jnp.dot(oa_ref[r, :], wa_ref[...], preferred_element_type=f32) for r in subs]
    yd = [jnp.dot(o, wd_ref[...], preferred_element_type=f32) for o in odn]
    merged = [(_sigmoid(ga_ref[r, :].astype(f32)) * a + _sigmoid(gd_ref[r, :].astype(f32)) * d).astype(bf16)
              for r, a, d in zip(subs, ya, yd)]
    mo = [jnp.dot(m, wo_ref[...], preferred_element_type=f32) for m in merged]
    h2s = []
    for r, m in zip(subs, mo):
        x2 = jnp.where(first_group, xa_ref[r, :], xb_ref[r, :]) + m
        x2_ref[r, :] = x2
        h2s.append(x2 * lax.rsqrt(jnp.mean(x2 * x2, axis=-1, keepdims=True) + NORM_EPS) * n2_ref[...])
    logits = [lax.dot_general(rw_ref[...], h2, (((1,), (1,)), ((), ())), precision=lax.Precision.HIGHEST,
                              preferred_element_type=f32) + rb_ref[...] for h2 in h2s]
    for n, (r, h2) in enumerate(zip(subs, h2s)):
        packed = _pack_halves(h2[:, :half], h2[:, half:])
        for s in range(ROW_TILES):
            hp_ref[_row_tile(n * sub, sub, s), :] = packed[:, s * LANES:(s + 1) * LANES]
        cur = logits[n]
        erow = lax.broadcasted_iota(i32, cur.shape, 0).astype(f32)
        vals, idxs = [], []
        for _ in range(TOP_K):
            m = jnp.max(cur, axis=0, keepdims=True)
            idx = jnp.min(jnp.where(cur == m, erow, float(N_EXPERTS)), axis=0, keepdims=True)
            vals.append(m)
            idxs.append(idx)
            cur = jnp.where(erow == idx, -jnp.inf, cur)
        exps = [jnp.exp(v - vals[0]) for v in vals]
        rtot = 1.0 / (exps[0] + exps[1] + exps[2] + exps[3])
        out_row = lax.broadcasted_iota(i32, (8, sub), 0)
        idx_out = jnp.zeros((8, sub), f32)
        gate_out = jnp.zeros((8, sub), f32)
        for kk in range(TOP_K):
            idx_out = jnp.where(out_row == kk, idxs[kk], idx_out)
            gate_out = jnp.where(out_row == kk, exps[kk] * rtot, gate_out)
        idx_ref[:, r] = idx_out.astype(i32)
        gate_ref[:, r] = gate_out


def _post(xa, xb, o_attn, o_f, o_b, proj, dnw, wa, wd, wo, n2, rw, rb):
    T = xa.shape[0] + xb.shape[0]
    tm = min(POST_TM, xa.shape[0], xb.shape[0])
    assert xa.shape[0] % tm == 0 and xb.shape[0] % tm == 0
    npb = xa.shape[0] // tm
    zcol = P_DNZ // DN_V_W
    const = lambda shape: pl.BlockSpec(shape, lambda i: (0, 0))
    row = lambda width, col=0: pl.BlockSpec((tm, width), lambda i: (i, col))
    return pl.pallas_call(
        functools.partial(_post_kernel, npb=npb),
        grid=(T // tm,),
        in_specs=_group_row_specs(tm, D_MODEL, npb) + [
            row(ATTN_Q_W), row(DN_V_W), row(DN_V_W),
            row(DN_V_W, zcol), row(D_MODEL, P_GA // D_MODEL), row(D_MODEL, P_GD // D_MODEL),
            const((1, DN_V_DIM)), const((ATTN_Q_W, D_MODEL)), const((DN_V_W, D_MODEL)),
            const((D_MODEL, D_MODEL)), const((1, D_MODEL)), const((N_EXPERTS, D_MODEL)),
            const((N_EXPERTS, 1)),
        ],
        out_specs=[row(D_MODEL), pl.BlockSpec((tm * ROW_TILES, LANES), lambda i: (i, 0)),
                   pl.BlockSpec((8, tm), lambda i: (0, i)), pl.BlockSpec((8, tm), lambda i: (0, i))],
        out_shape=[
            jax.ShapeDtypeStruct((T, D_MODEL), f32),
            jax.ShapeDtypeStruct((T * ROW_TILES, LANES), u32),
            jax.ShapeDtypeStruct((8, T), i32),
            jax.ShapeDtypeStruct((8, T), f32),
        ],
        compiler_params=_cparams(("parallel",)),
        name="post_mix",
    )(xa, xb, o_attn, o_f, o_b, proj, proj, proj, dnw, wa, wd, wo, n2, rw, rb)


def _scatter_kernel(poff_ref, pn_ref, pos_ref, h_ref, xs_ref, zero_ref, sem, zsem):
    R = pos_ref.shape[2] // TOP_K
    nbits = MOE_TM.bit_length() - 1

    @pl.when(pl.program_id(0) == 0)
    def _():
        zero_ref[...] = jnp.zeros(zero_ref.shape, u32)

        def per_expert(e, carry):
            off = poff_ref[e]
            n = pn_ref[e]
            for bit in reversed(range(nbits)):
                size = 1 << bit
                take = ((n >> bit) & 1) == 1

                @pl.when(take)
                def _():
                    cp = pltpu.make_async_copy(zero_ref.at[_row_span(0, size)],
                                               xs_ref.at[_row_span(off, size)], zsem)
                    cp.start()
                    cp.wait()

                off = off + jnp.where(take, size, 0)
            return carry

        lax.fori_loop(0, N_EXPERTS, per_expert, 0)

        zrows = zero_ref.shape[0] // ROW_TILES

        def tail(j, carry):
            row = poff_ref[N_EXPERTS] + j * zrows
            cp = pltpu.make_async_copy(zero_ref, xs_ref.at[_row_span(row, zrows)], zsem)
            cp.start()
            cp.wait()
            return carry

        lax.fori_loop(0, pn_ref[N_EXPERTS], tail, 0)

    def body(t, carry):
        for k in range(TOP_K):
            p = pos_ref[0, 0, t * TOP_K + k]
            pltpu.make_async_copy(h_ref.at[_row_span(t)], xs_ref.at[_row_span(p)], sem).start()
        return carry

    lax.fori_loop(0, R, body, 0, unroll=2)
    done = xs_ref.at[_row_span(0, TOP_K * R)]
    pltpu.make_async_copy(done, done, sem).wait()


def _scatter_rows(pad_off, pad_n, pos, hp, n_slots):
    T = hp.shape[0] // ROW_TILES
    R = min(SCATTER_R, T)
    grid_spec = pltpu.PrefetchScalarGridSpec(
        num_scalar_prefetch=2,
        grid=(T // R,),
        in_specs=[
            pl.BlockSpec((1, 1, TOP_K * R), lambda i, po, pn: (i, 0, 0), memory_space=pltpu.SMEM),
            pl.BlockSpec((R * ROW_TILES, LANES), lambda i, po, pn: (i, 0)),
        ],
        out_specs=pl.BlockSpec(memory_space=pl.ANY),
        scratch_shapes=[pltpu.VMEM((MOE_TM // 2 * ROW_TILES, LANES), u32),
                        pltpu.SemaphoreType.DMA(()), pltpu.SemaphoreType.DMA(())],
    )
    return pl.pallas_call(
        _scatter_kernel,
        grid_spec=grid_spec,
        out_shape=jax.ShapeDtypeStruct((n_slots * ROW_TILES, LANES), u32),
        compiler_params=_cparams(("arbitrary",)),
        name="moe_scatter",
    )(pad_off, pad_n, pos.reshape(T // R, 1, TOP_K * R), hp)


def _moe_kernel(te_ref, nu_ref, x_ref, wg_ref, wu_ref, bg_ref, bu_ref, wd_ref, bd_ref,
                o_ref, xb_ref, acc_ref, *, nf):
    i = pl.program_id(0)
    f = pl.program_id(1)
    half = D_MODEL // 2
    used = i < nu_ref[0]

    @pl.when(used & (f == 0))
    def _():
        for s in range(ROW_TILES):
            lo, hi = _unpack_halves(x_ref[_row_tile(0, xb_ref.shape[0], s), :])
            xb_ref[:, s * LANES:(s + 1) * LANES] = lo.astype(bf16)
            xb_ref[:, half + s * LANES:half + (s + 1) * LANES] = hi.astype(bf16)

    @pl.when(used)
    def _():
        xb = xb_ref[...]
        g = jnp.dot(xb, wg_ref[0, 0], preferred_element_type=f32) + bg_ref[0]
        u = jnp.dot(xb, wu_ref[0, 0], preferred_element_type=f32) + bu_ref[0]
        gate = jnp.minimum(g, SWIGLU_LIMIT)
        up = jnp.clip(u, -SWIGLU_LIMIT, SWIGLU_LIMIT)
        act = (up + 1.0) * (gate * _sigmoid(SWIGLU_ALPHA * gate))
        part = jnp.dot(act.astype(bf16), wd_ref[0].astype(bf16), preferred_element_type=f32)

        def finish(total):
            y = total + bd_ref[0]
            _store_row_tiles(o_ref, _pack_halves(y[:, :half], y[:, half:]))

        if nf == 1:
            finish(part)
        else:
            @pl.when(f == 0)
            def _():
                acc_ref[...] = part

            @pl.when((f > 0) & (f < nf - 1))
            def _():
                acc_ref[...] += part

            @pl.when(f == nf - 1)
            def _():
                finish(acc_ref[...] + part)

    @pl.when(jnp.logical_not(used) & (f == nf - 1))
    def _():
        o_ref[...] = jnp.zeros(o_ref.shape, u32)


def _moe(tile_expert, n_used, xs, wg, wu, bg, bu, wd, bd):
    n_slots = xs.shape[0] // ROW_TILES
    tm, tf = MOE_TM, MOE_TF
    n_tiles = n_slots // tm
    nf = D_FF // tf

    def fidx(i, f, te, nu):
        return jnp.where(i < nu[0], f, nf - 1)

    grid_spec = pltpu.PrefetchScalarGridSpec(
        num_scalar_prefetch=2,
        grid=(n_tiles, nf),
        in_specs=[
            pl.BlockSpec((tm * ROW_TILES, LANES), lambda i, f, te, nu: (i, 0)),
            pl.BlockSpec((1, 1, D_MODEL, tf), lambda i, f, te, nu: (te[i], fidx(i, f, te, nu), 0, 0)),
            pl.BlockSpec((1, 1, D_MODEL, tf), lambda i, f, te, nu: (te[i], fidx(i, f, te, nu), 0, 0)),
            pl.BlockSpec((1, 1, tf), lambda i, f, te, nu: (te[i], 0, fidx(i, f, te, nu))),
            pl.BlockSpec((1, 1, tf), lambda i, f, te, nu: (te[i], 0, fidx(i, f, te, nu))),
            pl.BlockSpec((1, tf, D_MODEL), lambda i, f, te, nu: (te[i], fidx(i, f, te, nu), 0)),
            pl.BlockSpec((1, 1, D_MODEL), lambda i, f, te, nu: (te[i], 0, 0)),
        ],
        out_specs=pl.BlockSpec((tm * ROW_TILES, LANES), lambda i, f, te, nu: (i, 0)),
        scratch_shapes=[pltpu.VMEM((tm, D_MODEL), bf16), pltpu.VMEM((tm, D_MODEL), f32)],
    )
    return pl.pallas_call(
        functools.partial(_moe_kernel, nf=nf),
        grid_spec=grid_spec,
        out_shape=jax.ShapeDtypeStruct((n_slots * ROW_TILES, LANES), u32),
        compiler_params=_cparams(("arbitrary", "arbitrary")),
        name="moe_experts",
    )(tile_expert, n_used, xs, wg, wu, bg, bu, wd, bd)


def _combine_kernel(pos_ref, posn_ref, x2_ref, gate_ref, fw_ref, ys_ref, oa_ref, ob_ref,
                    buf_ref, acc_ref, sem, *, npb):
    R = x2_ref.shape[0]
    half = D_MODEL // 2
    nrows = TOP_K * R
    i = pl.program_id(0)
    slot = i % 2

    def issue(p_ref, dst_slot):
        def body(a, carry):
            p = p_ref[0, 0, a]
            pltpu.make_async_copy(ys_ref.at[_row_span(p)],
                                  buf_ref.at[_row_span(dst_slot * nrows + a)], sem.at[dst_slot]).start()
            return carry

        lax.fori_loop(0, nrows, body, 0, unroll=8)

    @pl.when(i == 0)
    def _():
        issue(pos_ref, 0)

    @pl.when(i + 1 < pl.num_programs(0))
    def _():
        issue(posn_ref, 1 - slot)

    pltpu.make_async_copy(ys_ref.at[_row_span(0, nrows)], buf_ref.at[_row_span(slot * nrows, nrows)],
                          sem.at[slot]).wait()
    row0 = slot * nrows

    gates = gate_ref[...]
    gk = [gates[:, k:k + 1] for k in range(TOP_K)]
    ss = jnp.zeros((R, 1), f32)
    for s in range(ROW_TILES):
        cl = slice(s * LANES, (s + 1) * LANES)
        ch = slice(half + s * LANES, half + (s + 1) * LANES)
        lo_acc = x2_ref[:, cl]
        hi_acc = x2_ref[:, ch]
        for k in range(TOP_K):
            lo, hi = _unpack_halves(buf_ref[_row_tile(row0 + k * R, R, s), :])
            lo_acc = lo_acc + gk[k] * lo
            hi_acc = hi_acc + gk[k] * hi
        ss = ss + jnp.sum(lo_acc * lo_acc, axis=-1, keepdims=True) + jnp.sum(
            hi_acc * hi_acc, axis=-1, keepdims=True)
        acc_ref[:, cl] = lo_acc
        acc_ref[:, ch] = hi_acc
    inv = lax.rsqrt(ss / D_MODEL + NORM_EPS)

    @pl.when(i < npb)
    def _():
        oa_ref[...] = acc_ref[...] * inv * fw_ref[...]

    @pl.when(i >= npb)
    def _():
        ob_ref[...] = acc_ref[...] * inv * fw_ref[...]


def _combine(pos, x2, gates, fw, ys, rows_a):
    T = x2.shape[0]
    R = min(COMB_R, rows_a, T - rows_a)
    assert rows_a % R == 0 and T % R == 0
    npb = rows_a // R
    nsteps = T // R
    return pl.pallas_call(
        functools.partial(_combine_kernel, npb=npb),
        grid=(nsteps,),
        in_specs=[
            pl.BlockSpec((1, 1, TOP_K * R), lambda i: (i, 0, 0), memory_space=pltpu.SMEM),
            pl.BlockSpec((1, 1, TOP_K * R), lambda i: (jnp.minimum(i + 1, nsteps - 1), 0, 0),
                         memory_space=pltpu.SMEM),
            pl.BlockSpec((R, D_MODEL), lambda i: (i, 0)),
            pl.BlockSpec((R, TOP_K), lambda i: (i, 0)),
            pl.BlockSpec((1, D_MODEL), lambda i: (0, 0)),
            pl.BlockSpec(memory_space=pl.ANY),
        ],
        out_specs=_group_row_specs(R, D_MODEL, npb),
        out_shape=[jax.ShapeDtypeStruct((rows_a, D_MODEL), f32),
                   jax.ShapeDtypeStruct((T - rows_a, D_MODEL), f32)],
        scratch_shapes=[pltpu.VMEM((2 * TOP_K * R * ROW_TILES, LANES), u32),
                        pltpu.VMEM((R, D_MODEL), f32),
                        pltpu.SemaphoreType.DMA((2,))],
        compiler_params=_cparams(("arbitrary",)),
        name="moe_combine",
    )(pos, pos, x2, gates, fw, ys)


def _dn_channels(t):
    lead = t.shape[:-1]
    t = t.reshape(lead + (2, 2, DN_QK_HEADS, 2))
    perm = tuple(range(len(lead))) + tuple(len(lead) + a for a in (2, 0, 1, 3))
    return t.transpose(perm).reshape(lead + (4 * DN_V_HEADS,))


def _pad_lanes(t):
    pad = [(0, 0)] * (t.ndim - 1) + [(0, LANES - t.shape[-1])]
    return jnp.pad(t, pad)


def _deint_kernel(w_ref, p_ref, g_ref, u_ref):
    for c in range(w_ref.shape[2] // (2 * LANES)):
        w = w_ref[0, :, c * 2 * LANES:(c + 1) * 2 * LANES].astype(bf16)
        r = jnp.dot(w, p_ref[...], preferred_element_type=f32)
        g_ref[0, 0, :, c * LANES:(c + 1) * LANES] = r[:, :LANES].astype(bf16)
        u_ref[0, 0, :, c * LANES:(c + 1) * LANES] = r[:, LANES:].astype(bf16)


def _deinterleave(w):
    E, D, F2 = w.shape
    rb, cb = 1024, 2 * MOE_TF
    i = jnp.arange(2 * LANES)
    src = jnp.where(i < LANES, 2 * i, 2 * (i - LANES) + 1)
    perm = (jnp.arange(2 * LANES)[:, None] == src[None, :]).astype(bf16)
    out = jax.ShapeDtypeStruct((E, F2 // cb, D, MOE_TF), bf16)
    return pl.pallas_call(
        _deint_kernel,
        grid=(E, D // rb, F2 // cb),
        in_specs=[pl.BlockSpec((1, rb, cb), lambda e, r, c: (e, r, c)),
                  pl.BlockSpec((2 * LANES, 2 * LANES), lambda e, r, c: (0, 0))],
        out_specs=[pl.BlockSpec((1, 1, rb, MOE_TF), lambda e, r, c: (e, c, r, 0))] * 2,
        out_shape=[out, out],
        compiler_params=_cparams(("parallel", "parallel", "parallel")),
        name="moe_deinterleave",
    )(w, perm)


def _routing(top_idx, T, tm):
    A = T * TOP_K
    n_tiles = A // tm + N_EXPERTS
    n_slots = n_tiles * tm
    flat_e = top_idx.reshape(-1)
    onehot = (flat_e[:, None] == jnp.arange(N_EXPERTS, dtype=i32)[None, :]).astype(i32)
    csum = jnp.cumsum(onehot, axis=0)
    counts = csum[-1]
    padded = ((counts + tm - 1) // tm) * tm
    padded_end = jnp.cumsum(padded)
    padded_start = padded_end - padded
    pos = jnp.sum(onehot * (csum - 1 + padded_start[None, :]), axis=1).astype(i32)
    tile_start = jnp.arange(n_tiles, dtype=i32) * tm
    tile_expert = jnp.minimum(
        jnp.sum(padded_end[None, :] <= tile_start[:, None], axis=1), N_EXPERTS - 1).astype(i32)
    n_used = (padded_end[-1] // tm).astype(i32).reshape(1)
    pad_off = jnp.concatenate([padded_start + counts, padded_end[-1:]]).astype(i32)
    pad_n = jnp.concatenate([padded - counts, (n_slots - padded_end[-1:]) // (tm // 2)]).astype(i32)
    return pos, pad_off, pad_n, tile_expert, n_used, n_slots


def _layer(xa, xb, seq, norm1_w, w_in, b_in, attn_sinks, dn_conv_w, dn_a_log, dn_dt_bias,
           dn_norm_w, w_attn_o, w_dn_o, w_out, norm2_w, router_w, router_b, w_gate_up, b_gate_up,
           w_down, b_down, final_norm_w):
    T = xa.shape[0] + xb.shape[0]
    nseq = T // seq

    order = ([_OFF_GA, D_MODEL], [_OFF_GD, D_MODEL], [_OFF_DNQKV, DN_CONV_W], [_OFF_AQ, ATTN_Q_W],
             [_OFF_DNZ, DN_V_W], [_OFF_AK, ATTN_KV_W], [_OFF_AV, ATTN_KV_W])
    w_main = jnp.concatenate([w_in[:, o:o + n] for o, n in order], axis=1).astype(bf16)
    w_main = w_main.reshape(D_MODEL, P_W // IN_TN, IN_TN).transpose(1, 0, 2)
    b_main = jnp.concatenate([b_in[o:o + n] for o, n in order])[None, :]
    nsm = 4 * DN_V_HEADS
    w_small = _pad_lanes(_dn_channels(w_in[:, _OFF_DNB:_OFF_DNB + nsm])).astype(bf16)
    b_small = _pad_lanes(_dn_channels(b_in[_OFF_DNB:_OFF_DNB + nsm]))
    zeros_b = jnp.zeros((2 * DN_V_HEADS,), f32)
    a_ch = _pad_lanes(_dn_channels(jnp.concatenate([zeros_b, dn_a_log.reshape(-1)])))
    d_ch = _pad_lanes(_dn_channels(jnp.concatenate([zeros_b, dn_dt_bias.reshape(-1)])))

    proj, small, small_t = _inproj(xa, xb, norm1_w[None, :], w_main, b_main, w_small, b_small[None, :],
                                   w_small.T, b_small[:, None])

    pos = jnp.arange(seq, dtype=f32)
    inv_freq = 1.0 / (ROPE_THETA ** (jnp.arange(0, ROPE_DIM, 2, dtype=f32) / ROPE_DIM))
    ang = pos[:, None] * inv_freq[None, :]
    cos, sin = jnp.cos(ang), jnp.sin(ang)
    half = ROPE_DIM // 2
    ones_tail = jnp.ones((seq, ATTN_HEAD_DIM - ROPE_DIM), f32)
    cos_head = jnp.concatenate([cos, cos, ones_tail], axis=1)
    sin_head = jnp.concatenate([-sin, sin, 0.0 * ones_tail], axis=1)
    cos_t = jnp.concatenate([cos_head, cos_head], axis=1)
    sin_t = jnp.concatenate([sin_head, sin_head], axis=1)

    o_attn = _attention(proj, attn_sinks, cos_t, sin_t, nseq, seq)

    qkv = _dnconv(proj, dn_conv_w, nseq, seq)
    u, w, qd, kd, aqk, egl = _dnprep(qkv, small, small_t, a_ch[None, :], d_ch[None, :],
                                     a_ch[:, None], d_ch[:, None])
    o_f, o_b = _dnscan(u, w, qd, kd, aqk, egl, nseq, seq)

    x2, hp, top_idx_t, top_gate_t = _post(
        xa, xb, o_attn, o_f, o_b, proj, dn_norm_w[None, :], w_attn_o.astype(bf16), w_dn_o.astype(bf16),
        w_out.astype(bf16), norm2_w[None, :], router_w.T, router_b[:, None])
    top_idx = top_idx_t[:TOP_K].T
    top_gate = top_gate_t[:TOP_K].T

    pos_a, pad_off, pad_n, tile_expert, n_used, n_slots = _routing(top_idx, T, MOE_TM)
    xs = _scatter_rows(pad_off, pad_n, pos_a, hp, n_slots)
    wg, wu = _deinterleave(w_gate_up)
    b_pairs = b_gate_up.reshape(N_EXPERTS, 1, D_FF, 2)
    bg, bu = b_pairs[..., 0], b_pairs[..., 1]
    ys = _moe(tile_expert, n_used, xs, wg, wu, bg, bu, w_down, b_down[:, None, :])
    R = min(COMB_R, xa.shape[0], xb.shape[0])
    pos_tiles = pos_a.reshape(T // R, R, TOP_K).transpose(0, 2, 1).reshape(T // R, 1, TOP_K * R)
    return _combine(pos_tiles, x2, top_gate, final_norm_w[None, :], ys, xa.shape[0])


def kernel(x_prompt, x_sample, norm1_w, w_in, b_in, attn_sinks, dn_conv_w, dn_a_log, dn_dt_bias,
           dn_norm_w, w_attn_o, w_dn_o, w_out, norm2_w, router_w, router_b, w_gate_up, b_gate_up,
           w_down, b_down, final_norm_w):
    bp, seq, d = x_prompt.shape
    bs = x_sample.shape[0]
    assert x_sample.shape[1] == seq and d == D_MODEL
    ya, yb = _layer(x_prompt.reshape(bp * seq, d), x_sample.reshape(bs * seq, d), seq,
                    norm1_w[0], w_in[0], b_in[0], attn_sinks[0], dn_conv_w[0], dn_a_log[0],
                    dn_dt_bias[0], dn_norm_w[0], w_attn_o[0], w_dn_o[0], w_out[0], norm2_w[0],
                    router_w[0], router_b[0], w_gate_up[0], b_gate_up[0], w_down[0], b_down[0],
                    final_norm_w)
    return (ya.reshape(bp, seq, d), yb.reshape(bs, seq, d))
```

```python
import functools
import math

import jax
import jax.numpy as jnp
from jax import lax
from jax.experimental import pallas as pl
from jax.experimental.pallas import tpu as pltpu

f32 = jnp.float32
bf16 = jnp.bfloat16
u32 = jnp.uint32
i32 = jnp.int32

D_MODEL = 2048
ATTN_HEADS = 16
ATTN_KV_HEADS = 4
ATTN_HEAD_DIM = 64
ATTN_BLOCK = 128
ROPE_THETA = 500000.0
ROPE_DIM = ATTN_HEAD_DIM // 4
DN_QK_HEADS = 4
DN_V_HEADS = 8
DN_K_DIM = 128
DN_V_DIM = 128
DN_CONV = 5
DN_CHUNK = 64
N_EXPERTS = 32
TOP_K = 4
D_FF = D_MODEL
SWIGLU_LIMIT = 7.0
SWIGLU_ALPHA = 1.702
NORM_EPS = 1e-6
NEG_INF = -1e30

ATTN_Q_W = ATTN_HEADS * ATTN_HEAD_DIM
ATTN_KV_W = ATTN_KV_HEADS * ATTN_HEAD_DIM
DN_QK_W = DN_QK_HEADS * DN_K_DIM
DN_V_W = DN_V_HEADS * DN_V_DIM
DN_CONV_W = 2 * DN_QK_W + DN_V_W

_OFF_AQ = 0
_OFF_AK = _OFF_AQ + ATTN_Q_W
_OFF_AV = _OFF_AK + ATTN_KV_W
_OFF_DNQKV = _OFF_AV + ATTN_KV_W
_OFF_DNZ = _OFF_DNQKV + DN_CONV_W
_OFF_DNB = _OFF_DNZ + DN_V_W
_OFF_DNA = _OFF_DNB + 2 * DN_V_HEADS
_OFF_GA = _OFF_DNA + 2 * DN_V_HEADS
_OFF_GD = _OFF_GA + D_MODEL

P_GA = 0
P_GD = P_GA + D_MODEL
P_DNQKV = P_GD + D_MODEL
P_AQ = P_DNQKV + DN_CONV_W
P_DNZ = P_AQ + ATTN_Q_W
P_AK = P_DNZ + DN_V_W
P_AV = P_AK + ATTN_KV_W
P_W = P_AV + ATTN_KV_W

LANES = 128
VMEM_LIMIT = 56 * 1024 * 1024

IN_TM = 1024
IN_TN = 512
DN_TB = 256
DN_PREP_TB = 512
POST_TM = 256
POST_SUB = 128
MOE_TM = 512
MOE_TF = 1024
SCATTER_R = 1024
COMB_R = 128


def _cparams(sem):
    return pltpu.CompilerParams(dimension_semantics=sem, vmem_limit_bytes=VMEM_LIMIT)


def _sigmoid(x):
    return 1.0 / (1.0 + jnp.exp(-x))


def _softplus(x):
    return jnp.maximum(x, 0.0) + jnp.log(1.0 + jnp.exp(-jnp.abs(x)))


def _pack_halves(lo, hi):
    a = lax.bitcast_convert_type(lo.astype(bf16).astype(f32), u32)
    b = lax.bitcast_convert_type(hi.astype(bf16).astype(f32), u32)
    return (a >> 16) | (b & jnp.uint32(0xFFFF0000))


def _unpack_halves(w):
    lo = lax.bitcast_convert_type(w << 16, f32)
    hi = lax.bitcast_convert_type(w & jnp.uint32(0xFFFF0000), f32)
    return lo, hi


ROW_TILES = D_MODEL // 2 // LANES


def _row_tile(r0, n, s):
    start = r0 * ROW_TILES
    if not isinstance(r0, int):
        start = pl.multiple_of(start, ROW_TILES)
    return pl.ds(start + s, n, stride=ROW_TILES)


def _row_span(r, n=1):
    start = r * ROW_TILES
    if not isinstance(r, int):
        start = pl.multiple_of(start, ROW_TILES)
    return pl.ds(start, n * ROW_TILES)


def _store_row_tiles(ref, packed):
    for s in range(ROW_TILES):
        ref[_row_tile(0, packed.shape[0], s), :] = packed[:, s * LANES:(s + 1) * LANES]


def _group_row_specs(tm, width, npb):
    first = pl.BlockSpec((tm, width), lambda i, *_: (jnp.minimum(i, npb - 1), 0))
    second = pl.BlockSpec((tm, width), lambda i, *_: (jnp.maximum(i - npb, 0), 0))
    return [first, second]


def _inproj_kernel(xa_ref, xb_ref, nw_ref, w_ref, b_ref, ws_ref, bs_ref, wst_ref, bst_ref,
                   o_ref, os_ref, ost_ref, h_ref, *, npb):
    @pl.when(pl.program_id(1) == 0)
    def _():
        x = jnp.where(pl.program_id(0) < npb, xa_ref[...], xb_ref[...])
        ms = jnp.mean(x * x, axis=-1, keepdims=True)
        hb = (x * lax.rsqrt(ms + NORM_EPS) * nw_ref[...]).astype(bf16)
        h_ref[...] = hb
        os_ref[...] = jnp.dot(hb, ws_ref[...], preferred_element_type=f32) + bs_ref[...]
        ost_ref[...] = lax.dot_general(wst_ref[...], hb, (((1,), (1,)), ((), ())),
                                       preferred_element_type=f32) + bst_ref[...]

    acc = jnp.dot(h_ref[...], w_ref[0], preferred_element_type=f32)
    o_ref[...] = (acc + b_ref[...]).astype(bf16)


def _inproj(xa, xb, nw, w, b, ws, bs, wst, bst):
    T = xa.shape[0] + xb.shape[0]
    tm = min(IN_TM, xa.shape[0], xb.shape[0])
    assert xa.shape[0] % tm == 0 and xb.shape[0] % tm == 0
    npb = xa.shape[0] // tm
    grid = (T // tm, P_W // IN_TN)
    return pl.pallas_call(
        functools.partial(_inproj_kernel, npb=npb),
        grid=grid,
        in_specs=_group_row_specs(tm, D_MODEL, npb) + [
            pl.BlockSpec((1, D_MODEL), lambda i, j: (0, 0)),
            pl.BlockSpec((1, D_MODEL, IN_TN), lambda i, j: (j, 0, 0)),
            pl.BlockSpec((1, IN_TN), lambda i, j: (0, j)),
            pl.BlockSpec((D_MODEL, LANES), lambda i, j: (0, 0)),
            pl.BlockSpec((1, LANES), lambda i, j: (0, 0)),
            pl.BlockSpec((LANES, D_MODEL), lambda i, j: (0, 0)),
            pl.BlockSpec((LANES, 1), lambda i, j: (0, 0)),
        ],
        out_specs=[
            pl.BlockSpec((tm, IN_TN), lambda i, j: (i, j)),
            pl.BlockSpec((tm, LANES), lambda i, j: (i, 0)),
            pl.BlockSpec((LANES, tm), lambda i, j: (0, i)),
        ],
        out_shape=[
            jax.ShapeDtypeStruct((T, P_W), bf16),
            jax.ShapeDtypeStruct((T, LANES), f32),
            jax.ShapeDtypeStruct((LANES, T), f32),
        ],
        scratch_shapes=[pltpu.VMEM((tm, D_MODEL), bf16)],
        compiler_params=_cparams(("parallel", "arbitrary")),
        name="inproj",
    )(xa, xb, nw, w, b, ws, bs, wst, bst)


def _rope(x, cos, sin):
    lane = lax.broadcasted_iota(i32, x.shape, 1) % ATTN_HEAD_DIM
    half = ROPE_DIM // 2
    partner = jnp.where((lane >= half) & (lane < ROPE_DIM),
                        pltpu.roll(x, half, 1), pltpu.roll(x, LANES - half, 1))
    return x * cos + partner * sin


def _attn_kernel(sink_ref, q_ref, kp_ref, kc_ref, kn_ref, vp_ref, vc_ref, vn_ref,
                 cos_ref, sin_ref, o_ref, *, nb):
    n = pl.program_id(1)
    W = ATTN_BLOCK
    HD = ATTN_HEAD_DIM
    G = ATTN_HEADS // ATTN_KV_HEADS

    q0 = pl.multiple_of(n * W, W)
    cq, sq = cos_ref[pl.ds(q0, W), :], sin_ref[pl.ds(q0, W), :]
    qs = []
    for c in range(ATTN_Q_W // LANES):
        xq = q_ref[:, c * LANES:(c + 1) * LANES].astype(f32)
        qs.append((_rope(xq, cq, sq) * (HD ** -0.5)).astype(bf16))

    kblocks = (kp_ref, kc_ref, kn_ref)
    ks = [[], []]
    for bi in range(3):
        p0 = pl.multiple_of(jnp.clip(n + bi - 1, 0, nb - 1) * W, W)
        ck, sk = cos_ref[pl.ds(p0, W), :], sin_ref[pl.ds(p0, W), :]
        for c in range(ATTN_KV_W // LANES):
            xk = kblocks[bi][:, c * LANES:(c + 1) * LANES].astype(f32)
            ks[c].append(_rope(xk, ck, sk).astype(bf16))
    kcat = [jnp.concatenate(ks[c], axis=0) for c in range(2)]
    vcat = jnp.concatenate([vp_ref[...], vc_ref[...], vn_ref[...]], axis=0)

    r = lax.broadcasted_iota(i32, (W, 3 * W), 0)
    c = lax.broadcasted_iota(i32, (W, 3 * W), 1)
    valid = (c >= r) & (c <= r + 2 * W)
    valid = valid & ((n > 0) | (c >= W)) & ((n < nb - 1) | (c < 2 * W))
    valid4 = jnp.concatenate([valid] * G, axis=0)

    scores, sinks = [], []
    for kh in range(ATTN_KV_HEADS):
        kc_, ko = divmod(kh * HD, LANES)
        k_h = kcat[kc_][:, ko:ko + HD]
        parts = []
        sink_parts = []
        for g in range(G):
            h = kh * G + g
            qc, qo = divmod(h * HD, LANES)
            parts.append(qs[qc][:, qo:qo + HD])
            sink_parts.append(jnp.full((W, 1), sink_ref[h], f32))
        qg = jnp.concatenate(parts, axis=0)
        sinks.append(jnp.concatenate(sink_parts, axis=0))
        scores.append(lax.dot_general(qg, k_h, (((1,), (1,)), ((), ())), preferred_element_type=f32))
    probs, rdenoms = [], []
    for s, sink in zip(scores, sinks):
        s = jnp.where(valid4, s, NEG_INF)
        m = jnp.maximum(jnp.max(s, axis=-1, keepdims=True), sink)
        p = jnp.exp(s - m)
        rdenoms.append(1.0 / (jnp.sum(p, axis=-1, keepdims=True) + jnp.exp(sink - m)))
        probs.append(p.astype(bf16))
    outs = [jnp.dot(p, vcat[:, kh * HD:(kh + 1) * HD], preferred_element_type=f32)
            for kh, p in enumerate(probs)]
    for kh in range(ATTN_KV_HEADS):
        o = outs[kh] * rdenoms[kh]
        for g in range(G):
            h = kh * G + g
            o_ref[:, h * HD:(h + 1) * HD] = o[g * W:(g + 1) * W].astype(bf16)


def _attention(proj, sinks, cos_t, sin_t, nseq, seq):
    W = ATTN_BLOCK
    nb = seq // W
    T = nseq * seq
    qcol = P_AQ // ATTN_Q_W
    kcol = P_AK // ATTN_KV_W
    vcol = P_AV // ATTN_KV_W

    def kv_spec(col, d):
        return pl.BlockSpec(
            (W, ATTN_KV_W), lambda b, n: (b * nb + jnp.clip(n + d, 0, nb - 1), col))

    return pl.pallas_call(
        functools.partial(_attn_kernel, nb=nb),
        grid=(nseq, nb),
        in_specs=[
            pl.BlockSpec(memory_space=pltpu.SMEM),
            pl.BlockSpec((W, ATTN_Q_W), lambda b, n: (b * nb + n, qcol)),
            kv_spec(kcol, -1), kv_spec(kcol, 0), kv_spec(kcol, 1),
            kv_spec(vcol, -1), kv_spec(vcol, 0), kv_spec(vcol, 1),
            pl.BlockSpec(memory_space=pltpu.VMEM),
            pl.BlockSpec(memory_space=pltpu.VMEM),
        ],
        out_specs=pl.BlockSpec((W, ATTN_Q_W), lambda b, n: (b * nb + n, 0)),
        out_shape=jax.ShapeDtypeStruct((T, ATTN_Q_W), bf16),
        compiler_params=_cparams(("parallel", "parallel")),
        name="window_attn",
    )(sinks, proj, proj, proj, proj, proj, proj, proj, cos_t, sin_t)


def _dnconv_kernel(x_ref, w_ref, o_ref, pad_ref, *, seq, rt):
    j = pl.program_id(1)
    half = DN_CONV // 2
    cw = x_ref.shape[1]
    pad_ref[0:8, :] = jnp.zeros((8, cw), f32)
    pad_ref[seq + 8:seq + 16, :] = jnp.zeros((8, cw), f32)
    pad_ref[8:seq + 8, :] = x_ref[...].astype(f32)
    w = w_ref[...]
    qk_blocks = 2 * DN_QK_W // cw
    q_blocks = DN_QK_W // cw
    for r0 in range(0, seq, rt):
        acc = jnp.zeros((rt, cw), f32)
        for t in range(DN_CONV):
            s0 = r0 + 8 - half + t
            acc = acc + pad_ref[s0:s0 + rt, :] * w[t:t + 1, :]
        y = acc * _sigmoid(acc)
        outs = []
        for hh in range(cw // DN_K_DIM):
            seg = y[:, hh * DN_K_DIM:(hh + 1) * DN_K_DIM]
            inv = lax.rsqrt(jnp.sum(seg * seg, axis=-1, keepdims=True) + NORM_EPS)
            fac = jnp.where(j < q_blocks, inv * (DN_K_DIM ** -0.5),
                            jnp.where(j < qk_blocks, inv, jnp.ones_like(inv)))
            outs.append(seg * fac)
        o_ref[r0:r0 + rt, :] = jnp.concatenate(outs, axis=1).astype(bf16)


def _dnconv(proj, conv_w, nseq, seq):
    cw = 256
    T = nseq * seq
    col0 = P_DNQKV // cw
    rt = min(512, seq)
    return pl.pallas_call(
        functools.partial(_dnconv_kernel, seq=seq, rt=rt),
        grid=(nseq, DN_CONV_W // cw),
        in_specs=[
            pl.BlockSpec((seq, cw), lambda b, j: (b, col0 + j)),
            pl.BlockSpec((DN_CONV, cw), lambda b, j: (0, j)),
        ],
        out_specs=pl.BlockSpec((seq, cw), lambda b, j: (b, j)),
        out_shape=jax.ShapeDtypeStruct((T, DN_CONV_W), bf16),
        scratch_shapes=[pltpu.VMEM((seq + 16, cw), f32)],
        compiler_params=_cparams(("parallel", "parallel")),
        name="dn_conv",
    )(proj, conv_w)


def _mm(a, b):
    return jnp.dot(a.astype(bf16), b.astype(bf16), preferred_element_type=f32)


def _unit_tri_inverse_all(lmats, eye, blk_i, blk_j):
    same8 = (blk_i // 8) == (blk_j // 8)
    n1 = [jnp.where(same8, -l, 0.0) for l in lmats]
    p = [eye + n for n in n1]
    n2 = [_mm(n, n) for n in n1]
    p = [pp + _mm(pp, nn) for pp, nn in zip(p, n2)]
    n4 = [_mm(n, n) for n in n2]
    inv = [pp + _mm(pp, nn) for pp, nn in zip(p, n4)]
    for s in (8, 16, 32):
        coupling = ((blk_i // (2 * s)) == (blk_j // (2 * s))) & ((blk_i // s) != (blk_j // s))
        t = [_mm(jnp.where(coupling, l, 0.0), iv) for l, iv in zip(lmats, inv)]
        inv = [iv - _mm(iv, tt) for iv, tt in zip(inv, t)]
    return inv


def _chunk_cumsum(x, axis, reverse):
    n = x.shape[axis]
    pos = lax.broadcasted_iota(i32, x.shape, axis) % DN_CHUNK
    shift = 1
    while shift < DN_CHUNK:
        if reverse:
            moved = pltpu.roll(x, n - shift, axis)
            x = x + jnp.where(pos < DN_CHUNK - shift, moved, 0.0)
        else:
            moved = pltpu.roll(x, shift, axis)
            x = x + jnp.where(pos >= shift, moved, 0.0)
        shift *= 2
    return x


def _dnprep_kernel(q_ref, k_ref, v_ref, sm_ref, smt_ref, ar_ref, dr_ref, ac_ref, dc_ref,
                   u_ref, w_ref, qd_ref, kd_ref, aqk_ref, egl_ref):
    hq = pl.program_id(1)
    C = DN_CHUNK
    P = 2 * C
    TB = q_ref.shape[0]
    nch = TB // C

    sm = sm_ref[...]
    beta_full = _sigmoid(sm)
    g_full = -jnp.exp(ar_ref[...]) * _softplus(sm + dr_ref[...])
    shift = (LANES - hq * 8) % LANES
    beta_c = pltpu.roll(beta_full, shift, 1)
    g_c = pltpu.roll(g_full, shift, 1)
    smt = smt_ref[...]
    g_r = -jnp.exp(ac_ref[...]) * _softplus(smt + dc_ref[...])

    bi = lax.broadcasted_iota(i32, (P, P), 0)
    bj = lax.broadcasted_iota(i32, (P, P), 1)
    eye = (bi == bj).astype(f32)
    same_head = (bi // C) == (bj // C)

    gcum_c = [_chunk_cumsum(g_c, 0, d == 1) for d in range(2)]
    gcum_r = [_chunk_cumsum(g_r, 1, d == 1) for d in range(2)]

    k2, q2, kk2, qk2, v2 = [], [], [], [], []
    for c in range(nch):
        rows = slice(c * C, (c + 1) * C)
        k = k_ref[rows, :]
        q = q_ref[rows, :]
        kp = jnp.concatenate([k, k], axis=0)
        qp = jnp.concatenate([q, q], axis=0)
        k2.append(kp.astype(f32))
        q2.append(qp.astype(f32))
        kk2.append(lax.dot_general(kp, kp, (((1,), (1,)), ((), ())), preferred_element_type=f32))
        qk2.append(lax.dot_general(qp, kp, (((1,), (1,)), ((), ())), preferred_element_type=f32))
        v2.append(jnp.concatenate([v_ref[rows, :DN_V_DIM], v_ref[rows, DN_V_DIM:]], axis=0).astype(f32))

    inst = [(d, c) for d in range(2) for c in range(nch)]
    gcol, bcol, decay, lmats = [], [], [], []
    for d, c in inst:
        rows = slice(c * C, (c + 1) * C)
        b0, g0 = d * 2, 4 + d * 2
        gc = jnp.concatenate([gcum_c[d][rows, g0:g0 + 1], gcum_c[d][rows, g0 + 1:g0 + 2]], axis=0)
        gr = jnp.concatenate([gcum_r[d][g0:g0 + 1, rows], gcum_r[d][g0 + 1:g0 + 2, rows]], axis=1)
        bc = jnp.concatenate([beta_c[rows, b0:b0 + 1], beta_c[rows, b0 + 1:b0 + 2]], axis=0)
        incl = same_head & ((bj <= bi) if d == 0 else (bj >= bi))
        strict = same_head & ((bj < bi) if d == 0 else (bj > bi))
        dec = jnp.where(incl, jnp.exp(jnp.where(incl, gc - gr, 0.0)), 0.0)
        gcol.append(gc)
        bcol.append(bc)
        decay.append(dec)
        lmats.append(jnp.where(strict, kk2[c] * bc * dec, 0.0))

    invs = _unit_tri_inverse_all(lmats, eye, bi, bj)

    for n, (d, c) in enumerate(inst):
        rows = slice(c * C, (c + 1) * C)
        last = C - 1 if d == 0 else 0
        gc, bc = gcol[n], bcol[n]
        eg = jnp.exp(gc)
        rhs = jnp.concatenate([v2[c] * bc, k2[c] * (bc * eg)], axis=1)
        sol = _mm(invs[n], rhs)
        gl0 = gc[last:last + 1, :]
        gl1 = gc[C + last:C + last + 1, :]
        glast = jnp.concatenate([jnp.broadcast_to(gl0, (C, 1)), jnp.broadcast_to(gl1, (C, 1))], axis=0)
        qd = q2[c] * eg
        kd = k2[c] * jnp.exp(glast - gc)
        aqk = qk2[c] * decay[n]
        for hvl in range(2):
            pr = slice(hvl * C, (hvl + 1) * C)
            cols = slice(hvl * DN_V_DIM, (hvl + 1) * DN_V_DIM)
            u_ref[d, rows, cols] = sol[pr, :DN_V_DIM].astype(bf16)
            w_ref[d, rows, cols] = sol[pr, DN_V_DIM:].astype(bf16)
            qd_ref[d, rows, cols] = qd[pr].astype(bf16)
            kd_ref[d, rows, cols] = kd[pr].astype(bf16)
            aqk_ref[d, rows, pr] = aqk[pr, pr].astype(bf16)
        egl_ref[d, 0, c, 0:1, :] = jnp.broadcast_to(jnp.exp(gl0), (1, LANES))
        egl_ref[d, 0, c, 1:2, :] = jnp.broadcast_to(jnp.exp(gl1), (1, LANES))


def _dnprep(qkv, small, small_t, ar, dr, ac, dc):
    T = qkv.shape[0]
    TB = DN_PREP_TB
    nblk = T // TB
    nch = TB // DN_CHUNK
    kcol = DN_QK_W // DN_K_DIM
    vcol = 2 * DN_QK_W // (2 * DN_V_DIM)
    wide = lambda: pl.BlockSpec((2, TB, 2 * DN_V_DIM), lambda i, h: (0, i, h))
    big = jax.ShapeDtypeStruct((2, T, DN_V_W), bf16)
    return pl.pallas_call(
        _dnprep_kernel,
        grid=(nblk, DN_QK_HEADS),
        in_specs=[
            pl.BlockSpec((TB, DN_K_DIM), lambda i, h: (i, h)),
            pl.BlockSpec((TB, DN_K_DIM), lambda i, h: (i, kcol + h)),
            pl.BlockSpec((TB, 2 * DN_V_DIM), lambda i, h: (i, vcol + h)),
            pl.BlockSpec((TB, LANES), lambda i, h: (i, 0)),
            pl.BlockSpec((8, TB), lambda i, h: (h, i)),
            pl.BlockSpec((1, LANES), lambda i, h: (0, 0)),
            pl.BlockSpec((1, LANES), lambda i, h: (0, 0)),
            pl.BlockSpec((8, 1), lambda i, h: (h, 0)),
            pl.BlockSpec((8, 1), lambda i, h: (h, 0)),
        ],
        out_specs=[
            wide(), wide(), wide(), wide(),
            pl.BlockSpec((2, TB, 2 * DN_CHUNK), lambda i, h: (0, i, h)),
            pl.BlockSpec((2, 1, nch, 2, LANES), lambda i, h: (0, h, i, 0, 0)),
        ],
        out_shape=[
            big, big, big, big,
            jax.ShapeDtypeStruct((2, T, DN_QK_HEADS * 2 * DN_CHUNK), bf16),
            jax.ShapeDtypeStruct((2, DN_QK_HEADS, T // DN_CHUNK, 2, LANES), f32),
        ],
        compiler_params=_cparams(("parallel", "parallel")),
        name="dn_prep",
    )(qkv, qkv, qkv, small, small_t, ar, dr, ac, dc)


def _dnscan_kernel(uf, wf, qf, kf, af, ef, ub, wb, qb, kb, ab, eb, of_ref, ob_ref, s_ref):
    t = pl.program_id(1)
    C = DN_CHUNK
    nch = uf.shape[1] // C

    @pl.when(t == 0)
    def _():
        s_ref[...] = jnp.zeros(s_ref.shape, f32)

    dirs = ((uf, wf, qf, kf, af, ef, of_ref), (ub, wb, qb, kb, ab, eb, ob_ref))
    chains = [(d, hv) for d in range(2) for hv in range(DN_V_HEADS)]
    for step in range(nch):
        def rows(d):
            c = step if d == 0 else nch - 1 - step
            return c, slice(c * C, (c + 1) * C)

        def cols(hv):
            return slice(hv * DN_V_DIM, (hv + 1) * DN_V_DIM)

        s_old = [s_ref[d * DN_V_HEADS + hv] for d, hv in chains]
        s_bf = [s.astype(bf16) for s in s_old]
        w_s = [jnp.dot(dirs[d][1][0, rows(d)[1], cols(hv)], sb, preferred_element_type=f32)
               for (d, hv), sb in zip(chains, s_bf)]
        q_s = [jnp.dot(dirs[d][2][0, rows(d)[1], cols(hv)], sb, preferred_element_type=f32)
               for (d, hv), sb in zip(chains, s_bf)]
        v_bf = [(dirs[d][0][0, rows(d)[1], cols(hv)].astype(f32) - ws).astype(bf16)
                for (d, hv), ws in zip(chains, w_s)]
        a_v = [jnp.dot(dirs[d][4][0, rows(d)[1], hv * C:(hv + 1) * C], vb, preferred_element_type=f32)
               for (d, hv), vb in zip(chains, v_bf)]
        upd = [lax.dot_general(dirs[d][3][0, rows(d)[1], cols(hv)], vb, (((0,), (0,)), ((), ())),
                               preferred_element_type=f32)
               for (d, hv), vb in zip(chains, v_bf)]
        for n, (d, hv) in enumerate(chains):
            c, rs = rows(d)
            e = dirs[d][5][0, hv // 2, c, hv % 2:hv % 2 + 1, :]
            s_ref[d * DN_V_HEADS + hv] = s_old[n] * e + upd[n]
            dirs[d][6][rs, cols(hv)] = (q_s[n] + a_v[n]).astype(bf16)


def _dnscan(u, w, qd, kd, aqk, egl, nseq, seq):
    TB = DN_TB
    nt = seq // TB
    nch = TB // DN_CHUNK
    T = nseq * seq

    def blk(d):
        if d == 0:
            return lambda b, t: b * nt + t
        return lambda b, t: b * nt + nt - 1 - t

    def specs(d):
        wide = lambda: pl.BlockSpec((1, TB, DN_V_W), lambda b, t: (d, blk(d)(b, t), 0))
        return [wide(), wide(), wide(), wide(),
                pl.BlockSpec((1, TB, DN_V_HEADS * DN_CHUNK), lambda b, t: (d, blk(d)(b, t), 0)),
                pl.BlockSpec((1, DN_QK_HEADS, nch, 2, LANES), lambda b, t: (d, 0, blk(d)(b, t), 0, 0))]

    def ospec(d):
        return pl.BlockSpec((TB, DN_V_W), lambda b, t: (blk(d)(b, t), 0))

    args = (u, w, qd, kd, aqk, egl)
    return pl.pallas_call(
        _dnscan_kernel,
        grid=(nseq, nt),
        in_specs=specs(0) + specs(1),
        out_specs=[ospec(0), ospec(1)],
        out_shape=[jax.ShapeDtypeStruct((T, DN_V_W), bf16)] * 2,
        scratch_shapes=[pltpu.VMEM((2 * DN_V_HEADS, DN_K_DIM, DN_V_DIM), f32)],
        compiler_params=_cparams(("parallel", "arbitrary")),
        name="dn_scan",
    )(*args, *args)


def _post_kernel(xa_ref, xb_ref, oa_ref, of_ref, ob_ref, z_ref, ga_ref, gd_ref, dnw_ref, wa_ref, wd_ref,
                 wo_ref, n2_ref, rw_ref, rb_ref, x2_ref, hp_ref, idx_ref, gate_ref, *, npb):
    sub = POST_SUB
    subs = [slice(s * sub, (s + 1) * sub) for s in range(x2_ref.shape[0] // sub)]
    first_group = pl.program_id(0) < npb
    half = D_MODEL // 2

    odn = []
    for r in subs:
        od = of_ref[r, :].astype(f32) + ob_ref[r, :].astype(f32)
        z = z_ref[r, :].astype(f32)
        parts = []
        for h in range(DN_V_HEADS):
            cols = slice(h * DN_V_DIM, (h + 1) * DN_V_DIM)
            seg = od[:, cols]
            seg = seg * lax.rsqrt(jnp.mean(seg * seg, axis=-1, keepdims=True) + NORM_EPS) * dnw_ref[...]
            zz = z[:, cols]
            parts.append((seg * (zz * _sigmoid(zz))).astype(bf16))
        odn.append(jnp.concatenate(parts, axis=1))
    ya = [jnp.dot(oa_ref[r, :], wa_ref[...], preferred_element_type=f32) for r in subs]
    yd = [jnp.dot(o, wd_ref[...], preferred_element_type=f32) for o in odn]
    merged = [(_sigmoid(ga_ref[r, :].astype(f32)) * a + _sigmoid(gd_ref[r, :].astype(f32)) * d).astype(bf16)
              for r, a, d in zip(subs, ya, yd)]
    mo = [jnp.dot(m, wo_ref[...], preferred_element_type=f32) for m in merged]
    h2s = []
    for r, m in zip(subs, mo):
        x2 = jnp.where(first_group, xa_ref[r, :], xb_ref[r, :]) + m
        x2_ref[r, :] = x2
        h2s.append(x2 * lax.rsqrt(jnp.mean(x2 * x2, axis=-1, keepdims=True) + NORM_EPS) * n2_ref[...])
    logits = [lax.dot_general(rw_ref[...], h2, (((1,), (1,)), ((), ())), precision=lax.Precision.HIGHEST,
                              preferred_element_type=f32) + rb_ref[...] for h2 in h2s]
    for n, (r, h2) in enumerate(zip(subs, h2s)):
        packed = _pack_halves(h2[:, :half], h2[:, half:])
        for s in range(ROW_TILES):
            hp_ref[_row_tile(n * sub, sub, s), :] = packed[:, s * LANES:(s + 1) * LANES]
        cur = logits[n]
        erow = lax.broadcasted_iota(i32, cur.shape, 0).astype(f32)
        vals, idxs = [], []
        for _ in range(TOP_K):
            m = jnp.max(cur, axis=0, keepdims=True)
            idx = jnp.min(jnp.where(cur == m, erow, float(N_EXPERTS)), axis=0, keepdims=True)
            vals.append(m)
            idxs.append(idx)
            cur = jnp.where(erow == idx, -jnp.inf, cur)
        exps = [jnp.exp(v - vals[0]) for v in vals]
        rtot = 1.0 / (exps[0] + exps[1] + exps[2] + exps[3])
        out_row = lax.broadcasted_iota(i32, (8, sub), 0)
        idx_out = jnp.zeros((8, sub), f32)
        gate_out = jnp.zeros((8, sub), f32)
        for kk in range(TOP_K):
            idx_out = jnp.where(out_row == kk, idxs[kk], idx_out)
            gate_out = jnp.where(out_row == kk, exps[kk] * rtot, gate_out)
        idx_ref[:, r] = idx_out.astype(i32)
        gate_ref[:, r] = gate_out


def _post(xa, xb, o_attn, o_f, o_b, proj, dnw, wa, wd, wo, n2, rw, rb):
    T = xa.shape[0] + xb.shape[0]
    tm = min(POST_TM, xa.shape[0], xb.shape[0])
    assert xa.shape[0] % tm == 0 and xb.shape[0] % tm == 0
    npb = xa.shape[0] // tm
    zcol = P_DNZ // DN_V_W
    const = lambda shape: pl.BlockSpec(shape, lambda i: (0, 0))
    row = lambda width, col=0: pl.BlockSpec((tm, width), lambda i: (i, col))
    return pl.pallas_call(
        functools.partial(_post_kernel, npb=npb),
        grid=(T // tm,),
        in_specs=_group_row_specs(tm, D_MODEL, npb) + [
            row(ATTN_Q_W), row(DN_V_W), row(DN_V_W),
            row(DN_V_W, zcol), row(D_MODEL, P_GA // D_MODEL), row(D_MODEL, P_GD // D_MODEL),
            const((1, DN_V_DIM)), const((ATTN_Q_W, D_MODEL)), const((DN_V_W, D_MODEL)),
            const((D_MODEL, D_MODEL)), const((1, D_MODEL)), const((N_EXPERTS, D_MODEL)),
            const((N_EXPERTS, 1)),
        ],
        out_specs=[row(D_MODEL), pl.BlockSpec((tm * ROW_TILES, LANES), lambda i: (i, 0)),
                   pl.BlockSpec((8, tm), lambda i: (0, i)), pl.BlockSpec((8, tm), lambda i: (0, i))],
        out_shape=[
            jax.ShapeDtypeStruct((T, D_MODEL), f32),
            jax.ShapeDtypeStruct((T * ROW_TILES, LANES), u32),
            jax.ShapeDtypeStruct((8, T), i32),
            jax.ShapeDtypeStruct((8, T), f32),
        ],
        compiler_params=_cparams(("parallel",)),
        name="post_mix",
    )(xa, xb, o_attn, o_f, o_b, proj, proj, proj, dnw, wa, wd, wo, n2, rw, rb)


def _scatter_kernel(poff_ref, pn_ref, pos_ref, h_ref, xs_ref, zero_ref, sem, zsem):
    R = pos_ref.shape[2] // TOP_K
    nbits = MOE_TM.bit_length() - 1

    @pl.when(pl.program_id(0) == 0)
    def _():
        zero_ref[...] = jnp.zeros(zero_ref.shape, u32)

        def per_expert(e, carry):
            off = poff_ref[e]
            n = pn_ref[e]
            for bit in reversed(range(nbits)):
                size = 1 << bit
                take = ((n >> bit) & 1) == 1

                @pl.when(take)
                def _():
                    cp = pltpu.make_async_copy(zero_ref.at[_row_span(0, size)],
                                               xs_ref.at[_row_span(off, size)], zsem)
                    cp.start()
                    cp.wait()

                off = off + jnp.where(take, size, 0)
            return carry

        lax.fori_loop(0, N_EXPERTS, per_expert, 0)

        zrows = zero_ref.shape[0] // ROW_TILES

        def tail(j, carry):
            row = poff_ref[N_EXPERTS] + j * zrows
            cp = pltpu.make_async_copy(zero_ref, xs_ref.at[_row_span(row, zrows)], zsem)
            cp.start()
            cp.wait()
            return carry

        lax.fori_loop(0, pn_ref[N_EXPERTS], tail, 0)

    def body(t, carry):
        for k in range(TOP_K):
            p = pos_ref[0, 0, t * TOP_K + k]
            pltpu.make_async_copy(h_ref.at[_row_span(t)], xs_ref.at[_row_span(p)], sem).start()
        return carry

    lax.fori_loop(0, R, body, 0, unroll=2)
    done = xs_ref.at[_row_span(0, TOP_K * R)]
    pltpu.make_async_copy(done, done, sem).wait()


def _scatter_rows(pad_off, pad_n, pos, hp, n_slots):
    T = hp.shape[0] // ROW_TILES
    R = min(SCATTER_R, T)
    grid_spec = pltpu.PrefetchScalarGridSpec(
        num_scalar_prefetch=2,
        grid=(T // R,),
        in_specs=[
            pl.BlockSpec((1, 1, TOP_K * R), lambda i, po, pn: (i, 0, 0), memory_space=pltpu.SMEM),
            pl.BlockSpec((R * ROW_TILES, LANES), lambda i, po, pn: (i, 0)),
        ],
        out_specs=pl.BlockSpec(memory_space=pl.ANY),
        scratch_shapes=[pltpu.VMEM((MOE_TM // 2 * ROW_TILES, LANES), u32),
                        pltpu.SemaphoreType.DMA(()), pltpu.SemaphoreType.DMA(())],
    )
    return pl.pallas_call(
        _scatter_kernel,
        grid_spec=grid_spec,
        out_shape=jax.ShapeDtypeStruct((n_slots * ROW_TILES, LANES), u32),
        compiler_params=_cparams(("arbitrary",)),
        name="moe_scatter",
    )(pad_off, pad_n, pos.reshape(T // R, 1, TOP_K * R), hp)


def _moe_kernel(te_ref, nu_ref, x_ref, wg_ref, wu_ref, bg_ref, bu_ref, wd_ref, bd_ref,
                o_ref, xb_ref, acc_ref, *, nf):
    i = pl.program_id(0)
    f = pl.program_id(1)
    half = D_MODEL // 2
    used = i < nu_ref[0]

    @pl.when((i == 0) & (f == 0))
    def _():
        acc_ref[...] = jnp.zeros(acc_ref.shape, f32)

    @pl.when(used & (f == 0))
    def _():
        for s in range(ROW_TILES):
            lo, hi = _unpack_halves(x_ref[_row_tile(0, xb_ref.shape[0], s), :])
            xb_ref[:, s * LANES:(s + 1) * LANES] = lo.astype(bf16)
            xb_ref[:, half + s * LANES:half + (s + 1) * LANES] = hi.astype(bf16)

    @pl.when(used)
    def _():
        xb = xb_ref[...]
        g = jnp.dot(xb, wg_ref[0, 0], preferred_element_type=f32) + bg_ref[0]
        u = jnp.dot(xb, wu_ref[0, 0], preferred_element_type=f32) + bu_ref[0]
        gate = jnp.minimum(g, SWIGLU_LIMIT)
        up = jnp.clip(u, -SWIGLU_LIMIT, SWIGLU_LIMIT)
        act = (up + 1.0) * (gate * _sigmoid(SWIGLU_ALPHA * gate))
        part = jnp.dot(act.astype(bf16), wd_ref[0].astype(bf16), preferred_element_type=f32)

        if nf == 1:
            total = part
        else:
            total = jnp.where(f == 0, part, acc_ref[...] + part)
            acc_ref[...] = total
        y = total + bd_ref[0]
        _store_row_tiles(o_ref, _pack_halves(y[:, :half], y[:, half:]))

    @pl.when(jnp.logical_not(used) & (f == nf - 1))
    def _():
        o_ref[...] = jnp.zeros(o_ref.shape, u32)


def _moe(tile_expert, n_used, xs, wg, wu, bg, bu, wd, bd):
    n_slots = xs.shape[0] // ROW_TILES
    tm, tf = MOE_TM, MOE_TF
    n_tiles = n_slots // tm
    nf = D_FF // tf

    def fidx(i, f, te, nu):
        return jnp.where(i < nu[0], f, nf - 1)

    grid_spec = pltpu.PrefetchScalarGridSpec(
        num_scalar_prefetch=2,
        grid=(n_tiles, nf),
        in_specs=[
            pl.BlockSpec((tm * ROW_TILES, LANES), lambda i, f, te, nu: (i, 0)),
            pl.BlockSpec((1, 1, D_MODEL, tf), lambda i, f, te, nu: (te[i], fidx(i, f, te, nu), 0, 0)),
            pl.BlockSpec((1, 1, D_MODEL, tf), lambda i, f, te, nu: (te[i], fidx(i, f, te, nu), 0, 0)),
            pl.BlockSpec((1, 1, tf), lambda i, f, te, nu: (te[i], 0, fidx(i, f, te, nu))),
            pl.BlockSpec((1, 1, tf), lambda i, f, te, nu: (te[i], 0, fidx(i, f, te, nu))),
            pl.BlockSpec((1, tf, D_MODEL), lambda i, f, te, nu: (te[i], fidx(i, f, te, nu), 0)),
            pl.BlockSpec((1, 1, D_MODEL), lambda i, f, te, nu: (te[i], 0, 0)),
        ],
        out_specs=pl.BlockSpec((tm * ROW_TILES, LANES), lambda i, f, te, nu: (i, 0)),
        scratch_shapes=[pltpu.VMEM((tm, D_MODEL), bf16), pltpu.VMEM((tm, D_MODEL), f32)],
    )
    return pl.pallas_call(
        functools.partial(_moe_kernel, nf=nf),
        grid_spec=grid_spec,
        out_shape=jax.ShapeDtypeStruct((n_slots * ROW_TILES, LANES), u32),
        compiler_params=_cparams(("arbitrary", "arbitrary")),
        name="moe_experts",
    )(tile_expert, n_used, xs, wg, wu, bg, bu, wd, bd)


def _combine_kernel(pos_ref, posn_ref, x2_ref, gate_ref, fw_ref, ys_ref, oa_ref, ob_ref,
                    buf_ref, acc_ref, sem, *, npb):
    R = x2_ref.shape[0]
    half = D_MODEL // 2
    nrows = TOP_K * R
    i = pl.program_id(0)
    slot = i % 2

    def issue(p_ref, dst_slot):
        def body(a, carry):
            p = p_ref[0, 0, a]
            pltpu.make_async_copy(ys_ref.at[_row_span(p)],
                                  buf_ref.at[_row_span(dst_slot * nrows + a)], sem.at[dst_slot]).start()
            return carry

        lax.fori_loop(0, nrows, body, 0, unroll=8)

    @pl.when(i == 0)
    def _():
        issue(pos_ref, 0)

    @pl.when(i + 1 < pl.num_programs(0))
    def _():
        issue(posn_ref, 1 - slot)

    pltpu.make_async_copy(ys_ref.at[_row_span(0, nrows)], buf_ref.at[_row_span(slot * nrows, nrows)],
                          sem.at[slot]).wait()
    row0 = slot * nrows

    gates = gate_ref[...]
    gk = [gates[:, k:k + 1] for k in range(TOP_K)]
    ss = jnp.zeros((R, 1), f32)
    for s in range(ROW_TILES):
        cl = slice(s * LANES, (s + 1) * LANES)
        ch = slice(half + s * LANES, half + (s + 1) * LANES)
        lo_acc = x2_ref[:, cl]
        hi_acc = x2_ref[:, ch]
        for k in range(TOP_K):
            lo, hi = _unpack_halves(buf_ref[_row_tile(row0 + k * R, R, s), :])
            lo_acc = lo_acc + gk[k] * lo
            hi_acc = hi_acc + gk[k] * hi
        ss = ss + jnp.sum(lo_acc * lo_acc, axis=-1, keepdims=True) + jnp.sum(
            hi_acc * hi_acc, axis=-1, keepdims=True)
        acc_ref[:, cl] = lo_acc
        acc_ref[:, ch] = hi_acc
    inv = lax.rsqrt(ss / D_MODEL + NORM_EPS)

    @pl.when(i < npb)
    def _():
        oa_ref[...] = acc_ref[...] * inv * fw_ref[...]

    @pl.when(i >= npb)
    def _():
        ob_ref[...] = acc_ref[...] * inv * fw_ref[...]


def _combine(pos, x2, gates, fw, ys, rows_a):
    T = x2.shape[0]
    R = min(COMB_R, rows_a, T - rows_a)
    assert rows_a % R == 0 and T % R == 0
    npb = rows_a // R
    nsteps = T // R
    return pl.pallas_call(
        functools.partial(_combine_kernel, npb=npb),
        grid=(nsteps,),
        in_specs=[
            pl.BlockSpec((1, 1, TOP_K * R), lambda i: (i, 0, 0), memory_space=pltpu.SMEM),
            pl.BlockSpec((1, 1, TOP_K * R), lambda i: (jnp.minimum(i + 1, nsteps - 1), 0, 0),
                         memory_space=pltpu.SMEM),
            pl.BlockSpec((R, D_MODEL), lambda i: (i, 0)),
            pl.BlockSpec((R, TOP_K), lambda i: (i, 0)),
            pl.BlockSpec((1, D_MODEL), lambda i: (0, 0)),
            pl.BlockSpec(memory_space=pl.ANY),
        ],
        out_specs=_group_row_specs(R, D_MODEL, npb),
        out_shape=[jax.ShapeDtypeStruct((rows_a, D_MODEL), f32),
                   jax.ShapeDtypeStruct((T - rows_a, D_MODEL), f32)],
        scratch_shapes=[pltpu.VMEM((2 * TOP_K * R * ROW_TILES, LANES), u32),
                        pltpu.VMEM((R, D_MODEL), f32),
                        pltpu.SemaphoreType.DMA((2,))],
        compiler_params=_cparams(("arbitrary",)),
        name="moe_combine",
    )(pos, pos, x2, gates, fw, ys)


def _dn_channels(t):
    lead = t.shape[:-1]
    t = t.reshape(lead + (2, 2, DN_QK_HEADS, 2))
    perm = tuple(range(len(lead))) + tuple(len(lead) + a for a in (2, 0, 1, 3))
    return t.transpose(perm).reshape(lead + (4 * DN_V_HEADS,))


def _pad_lanes(t):
    pad = [(0, 0)] * (t.ndim - 1) + [(0, LANES - t.shape[-1])]
    return jnp.pad(t, pad)


def _deint_kernel(w_ref, p_ref, g_ref, u_ref):
    for c in range(w_ref.shape[2] // (2 * LANES)):
        w = w_ref[0, :, c * 2 * LANES:(c + 1) * 2 * LANES].astype(bf16)
        r = jnp.dot(w, p_ref[...], preferred_element_type=f32)
        g_ref[0, 0, :, c * LANES:(c + 1) * LANES] = r[:, :LANES].astype(bf16)
        u_ref[0, 0, :, c * LANES:(c + 1) * LANES] = r[:, LANES:].astype(bf16)


def _deinterleave(w):
    E, D, F2 = w.shape
    rb, cb = 1024, 2 * MOE_TF
    i = jnp.arange(2 * LANES)
    src = jnp.where(i < LANES, 2 * i, 2 * (i - LANES) + 1)
    perm = (jnp.arange(2 * LANES)[:, None] == src[None, :]).astype(bf16)
    out = jax.ShapeDtypeStruct((E, F2 // cb, D, MOE_TF), bf16)
    return pl.pallas_call(
        _deint_kernel,
        grid=(E, D // rb, F2 // cb),
        in_specs=[pl.BlockSpec((1, rb, cb), lambda e, r, c: (e, r, c)),
                  pl.BlockSpec((2 * LANES, 2 * LANES), lambda e, r, c: (0, 0))],
        out_specs=[pl.BlockSpec((1, 1, rb, MOE_TF), lambda e, r, c: (e, c, r, 0))] * 2,
        out_shape=[out, out],
        compiler_params=_cparams(("parallel", "parallel", "parallel")),
        name="moe_deinterleave",
    )(w, perm)


def _routing(top_idx, T, tm):
    A = T * TOP_K
    n_tiles = A // tm + N_EXPERTS
    n_slots = n_tiles * tm
    flat_e = top_idx.reshape(-1)
    onehot = (flat_e[:, None] == jnp.arange(N_EXPERTS, dtype=i32)[None, :]).astype(i32)
    csum = jnp.cumsum(onehot, axis=0)
    counts = csum[-1]
    padded = ((counts + tm - 1) // tm) * tm
    padded_end = jnp.cumsum(padded)
    padded_start = padded_end - padded
    pos = jnp.sum(onehot * (csum - 1 + padded_start[None, :]), axis=1).astype(i32)
    tile_start = jnp.arange(n_tiles, dtype=i32) * tm
    tile_expert = jnp.minimum(
        jnp.sum(padded_end[None, :] <= tile_start[:, None], axis=1), N_EXPERTS - 1).astype(i32)
    n_used = (padded_end[-1] // tm).astype(i32).reshape(1)
    pad_off = jnp.concatenate([padded_start + counts, padded_end[-1:]]).astype(i32)
    pad_n = jnp.concatenate([padded - counts, (n_slots - padded_end[-1:]) // (tm // 2)]).astype(i32)
    return pos, pad_off, pad_n, tile_expert, n_used, n_slots


def _layer(xa, xb, seq, norm1_w, w_in, b_in, attn_sinks, dn_conv_w, dn_a_log, dn_dt_bias,
           dn_norm_w, w_attn_o, w_dn_o, w_out, norm2_w, router_w, router_b, w_gate_up, b_gate_up,
           w_down, b_down, final_norm_w):
    T = xa.shape[0] + xb.shape[0]
    nseq = T // seq

    order = ([_OFF_GA, D_MODEL], [_OFF_GD, D_MODEL], [_OFF_DNQKV, DN_CONV_W], [_OFF_AQ, ATTN_Q_W],
             [_OFF_DNZ, DN_V_W], [_OFF_AK, ATTN_KV_W], [_OFF_AV, ATTN_KV_W])
    w_main = jnp.concatenate([w_in[:, o:o + n] for o, n in order], axis=1).astype(bf16)
    w_main = w_main.reshape(D_MODEL, P_W // IN_TN, IN_TN).transpose(1, 0, 2)
    b_main = jnp.concatenate([b_in[o:o + n] for o, n in order])[None, :]
    nsm = 4 * DN_V_HEADS
    w_small = _pad_lanes(_dn_channels(w_in[:, _OFF_DNB:_OFF_DNB + nsm])).astype(bf16)
    b_small = _pad_lanes(_dn_channels(b_in[_OFF_DNB:_OFF_DNB + nsm]))
    zeros_b = jnp.zeros((2 * DN_V_HEADS,), f32)
    a_ch = _pad_lanes(_dn_channels(jnp.concatenate([zeros_b, dn_a_log.reshape(-1)])))
    d_ch = _pad_lanes(_dn_channels(jnp.concatenate([zeros_b, dn_dt_bias.reshape(-1)])))

    proj, small, small_t = _inproj(xa, xb, norm1_w[None, :], w_main, b_main, w_small, b_small[None, :],
                                   w_small.T, b_small[:, None])

    pos = jnp.arange(seq, dtype=f32)
    inv_freq = 1.0 / (ROPE_THETA ** (jnp.arange(0, ROPE_DIM, 2, dtype=f32) / ROPE_DIM))
    ang = pos[:, None] * inv_freq[None, :]
    cos, sin = jnp.cos(ang), jnp.sin(ang)
    half = ROPE_DIM // 2
    ones_tail = jnp.ones((seq, ATTN_HEAD_DIM - ROPE_DIM), f32)
    cos_head = jnp.concatenate([cos, cos, ones_tail], axis=1)
    sin_head = jnp.concatenate([-sin, sin, 0.0 * ones_tail], axis=1)
    cos_t = jnp.concatenate([cos_head, cos_head], axis=1)
    sin_t = jnp.concatenate([sin_head, sin_head], axis=1)

    o_attn = _attention(proj, attn_sinks, cos_t, sin_t, nseq, seq)

    qkv = _dnconv(proj, dn_conv_w, nseq, seq)
    u, w, qd, kd, aqk, egl = _dnprep(qkv, small, small_t, a_ch[None, :], d_ch[None, :],
                                     a_ch[:, None], d_ch[:, None])
    o_f, o_b = _dnscan(u, w, qd, kd, aqk, egl, nseq, seq)

    x2, hp, top_idx_t, top_gate_t = _post(
        xa, xb, o_attn, o_f, o_b, proj, dn_norm_w[None, :], w_attn_o.astype(bf16), w_dn_o.astype(bf16),
        w_out.astype(bf16), norm2_w[None, :], router_w.T, router_b[:, None])
    top_idx = top_idx_t[:TOP_K].T
    top_gate = top_gate_t[:TOP_K].T

    pos_a, pad_off, pad_n, tile_expert, n_used, n_slots = _routing(top_idx, T, MOE_TM)
    xs = _scatter_rows(pad_off, pad_n, pos_a, hp, n_slots)
    wg, wu = _deinterleave(w_gate_up)
    b_pairs = b_gate_up.reshape(N_EXPERTS, 1, D_FF, 2)
    bg, bu = b_pairs[..., 0], b_pairs[..., 1]
    ys = _moe(tile_expert, n_used, xs, wg, wu, bg, bu, w_down, b_down[:, None, :])
    R = min(COMB_R, xa.shape[0], xb.shape[0])
    pos_tiles = pos_a.reshape(T // R, R, TOP_K).transpose(0, 2, 1).reshape(T // R, 1, TOP_K * R)
    return _combine(pos_tiles, x2, top_gate, final_norm_w[None, :], ys, xa.shape[0])


def kernel(x_prompt, x_sample, norm1_w, w_in, b_in, attn_sinks, dn_conv_w, dn_a_log, dn_dt_bias,
           dn_norm_w, w_attn_o, w_dn_o, w_out, norm2_w, router_w, router_b, w_gate_up, b_gate_up,
           w_down, b_down, final_norm_w):
    bp, seq, d = x_prompt.shape
    bs = x_sample.shape[0]
    assert x_sample.shape[1] == seq and d == D_MODEL
    ya, yb = _layer(x_prompt.reshape(bp * seq, d), x_sample.reshape(bs * seq, d), seq,
                    norm1_w[0], w_in[0], b_in[0], attn_sinks[0], dn_conv_w[0], dn_a_log[0],
                    dn_dt_bias[0], dn_norm_w[0], w_attn_o[0], w_dn_o[0], w_out[0], norm2_w[0],
                    router_w[0], router_b[0], w_gate_up[0], b_gate_up[0], w_down[0], b_down[0],
                    final_norm_w)
    return (ya.reshape(bp, seq, d), yb.reshape(bs, seq, d))
```

```python
import functools

import jax
import jax.numpy as jnp
from jax import lax
from jax.experimental import pallas as pl
from jax.experimental.pallas import tpu as pltpu

f32 = jnp.float32
bf16 = jnp.bfloat16
u32 = jnp.uint32
i32 = jnp.int32

D_MODEL = 2048
ATTN_HEADS = 16
ATTN_KV_HEADS = 4
ATTN_HEAD_DIM = 64
ATTN_BLOCK = 128
ROPE_THETA = 500000.0
ROPE_DIM = ATTN_HEAD_DIM // 4
DN_QK_HEADS = 4
DN_V_HEADS = 8
DN_K_DIM = 128
DN_V_DIM = 128
DN_CONV = 5
DN_CHUNK = 64
N_EXPERTS = 32
TOP_K = 4
D_FF = D_MODEL
SWIGLU_LIMIT = 7.0
SWIGLU_ALPHA = 1.702
NORM_EPS = 1e-6
NEG_INF = -1e30

ATTN_Q_W = ATTN_HEADS * ATTN_HEAD_DIM
ATTN_KV_W = ATTN_KV_HEADS * ATTN_HEAD_DIM
DN_QK_W = DN_QK_HEADS * DN_K_DIM
DN_V_W = DN_V_HEADS * DN_V_DIM
DN_CONV_W = 2 * DN_QK_W + DN_V_W

_OFF_AQ = 0
_OFF_AK = _OFF_AQ + ATTN_Q_W
_OFF_AV = _OFF_AK + ATTN_KV_W
_OFF_DNQKV = _OFF_AV + ATTN_KV_W
_OFF_DNZ = _OFF_DNQKV + DN_CONV_W
_OFF_DNB = _OFF_DNZ + DN_V_W
_OFF_DNA = _OFF_DNB + 2 * DN_V_HEADS
_OFF_GA = _OFF_DNA + 2 * DN_V_HEADS
_OFF_GD = _OFF_GA + D_MODEL

P_GA = 0
P_GD = P_GA + D_MODEL
P_DNQKV = P_GD + D_MODEL
P_AQ = P_DNQKV + DN_CONV_W
P_DNZ = P_AQ + ATTN_Q_W
P_AK = P_DNZ + DN_V_W
P_AV = P_AK + ATTN_KV_W
P_W = P_AV + ATTN_KV_W

LANES = 128
SUBLANES = 8
DN_GROUP = 8
VMEM_LIMIT = 56 * 1024 * 1024

IN_TM = 1024
IN_TN = 512
DN_TB = 256
DN_PREP_TB = 512
POST_TM = 256
POST_SUB = 128
MOE_TM = 512
MOE_TF = 1024
SCATTER_R = 1024
COMB_R = 256


def _cparams(sem):
    return pltpu.CompilerParams(dimension_semantics=sem, vmem_limit_bytes=VMEM_LIMIT)


def _sigmoid(x):
    return 1.0 / (1.0 + jnp.exp(-x))


def _softplus(x):
    return jnp.maximum(x, 0.0) + jnp.log(1.0 + jnp.exp(-jnp.abs(x)))


def _pack_halves(lo, hi):
    a = lax.bitcast_convert_type(lo.astype(bf16).astype(f32), u32)
    b = lax.bitcast_convert_type(hi.astype(bf16).astype(f32), u32)
    return (a >> 16) | (b & jnp.uint32(0xFFFF0000))


def _unpack_halves(w):
    lo = lax.bitcast_convert_type(w << 16, f32)
    hi = lax.bitcast_convert_type(w & jnp.uint32(0xFFFF0000), f32)
    return lo, hi


ROW_TILES = D_MODEL // 2 // LANES


def _row_tile(r0, n, s):
    start = r0 * ROW_TILES
    if not isinstance(r0, int):
        start = pl.multiple_of(start, ROW_TILES)
    return pl.ds(start + s, n, stride=ROW_TILES)


def _row_span(r, n=1):
    start = r * ROW_TILES
    if not isinstance(r, int):
        start = pl.multiple_of(start, ROW_TILES)
    return pl.ds(start, n * ROW_TILES)


def _store_row_tiles(ref, packed):
    for s in range(ROW_TILES):
        ref[_row_tile(0, packed.shape[0], s), :] = packed[:, s * LANES:(s + 1) * LANES]


def _group_row_specs(tm, width, npb):
    first = pl.BlockSpec((tm, width), lambda i, *_: (jnp.minimum(i, npb - 1), 0))
    second = pl.BlockSpec((tm, width), lambda i, *_: (jnp.maximum(i - npb, 0), 0))
    return [first, second]


def _inproj_kernel(xa_ref, xb_ref, nw_ref, w_ref, b_ref, ws_ref, bs_ref, wst_ref, bst_ref,
                   o_ref, os_ref, ost_ref, h_ref, *, npb):
    @pl.when(pl.program_id(1) == 0)
    def _():
        x = jnp.where(pl.program_id(0) < npb, xa_ref[...], xb_ref[...])
        ms = jnp.mean(x * x, axis=-1, keepdims=True)
        hb = (x * lax.rsqrt(ms + NORM_EPS) * nw_ref[...]).astype(bf16)
        h_ref[...] = hb
        os_ref[...] = jnp.dot(hb, ws_ref[...], preferred_element_type=f32) + bs_ref[...]
        ost_ref[...] = lax.dot_general(wst_ref[...], hb, (((1,), (1,)), ((), ())),
                                       preferred_element_type=f32) + bst_ref[...]

    acc = jnp.dot(h_ref[...], w_ref[0], preferred_element_type=f32)
    o_ref[...] = (acc + b_ref[...]).astype(bf16)


def _inproj(xa, xb, nw, w, b, ws, bs, wst, bst):
    T = xa.shape[0] + xb.shape[0]
    tm = min(IN_TM, xa.shape[0], xb.shape[0])
    assert xa.shape[0] % tm == 0 and xb.shape[0] % tm == 0
    npb = xa.shape[0] // tm
    grid = (T // tm, P_W // IN_TN)
    return pl.pallas_call(
        functools.partial(_inproj_kernel, npb=npb),
        grid=grid,
        in_specs=_group_row_specs(tm, D_MODEL, npb) + [
            pl.BlockSpec((1, D_MODEL), lambda i, j: (0, 0)),
            pl.BlockSpec((1, D_MODEL, IN_TN), lambda i, j: (j, 0, 0)),
            pl.BlockSpec((1, IN_TN), lambda i, j: (0, j)),
            pl.BlockSpec((D_MODEL, LANES), lambda i, j: (0, 0)),
            pl.BlockSpec((1, LANES), lambda i, j: (0, 0)),
            pl.BlockSpec((LANES, D_MODEL), lambda i, j: (0, 0)),
            pl.BlockSpec((LANES, 1), lambda i, j: (0, 0)),
        ],
        out_specs=[
            pl.BlockSpec((tm, IN_TN), lambda i, j: (i, j)),
            pl.BlockSpec((tm, LANES), lambda i, j: (i, 0)),
            pl.BlockSpec((LANES, tm), lambda i, j: (0, i)),
        ],
        out_shape=[
            jax.ShapeDtypeStruct((T, P_W), bf16),
            jax.ShapeDtypeStruct((T, LANES), f32),
            jax.ShapeDtypeStruct((LANES, T), f32),
        ],
        scratch_shapes=[pltpu.VMEM((tm, D_MODEL), bf16)],
        compiler_params=_cparams(("parallel", "arbitrary")),
        name="inproj",
    )(xa, xb, nw, w, b, ws, bs, wst, bst)


def _rope(x, cos, sin):
    lane = lax.broadcasted_iota(i32, x.shape, 1) % ATTN_HEAD_DIM
    half = ROPE_DIM // 2
    partner = jnp.where((lane >= half) & (lane < ROPE_DIM),
                        pltpu.roll(x, half, 1), pltpu.roll(x, LANES - half, 1))
    return x * cos + partner * sin


def _attn_kernel(sink_ref, q_ref, kp_ref, kc_ref, kn_ref, vp_ref, vc_ref, vn_ref,
                 cos_ref, sin_ref, o_ref, *, nb):
    n = pl.program_id(1)
    W = ATTN_BLOCK
    HD = ATTN_HEAD_DIM
    G = ATTN_HEADS // ATTN_KV_HEADS

    q0 = pl.multiple_of(n * W, W)
    cq, sq = cos_ref[pl.ds(q0, W), :], sin_ref[pl.ds(q0, W), :]
    qs = []
    for c in range(ATTN_Q_W // LANES):
        xq = q_ref[:, c * LANES:(c + 1) * LANES].astype(f32)
        qs.append((_rope(xq, cq, sq) * (HD ** -0.5)).astype(bf16))

    kblocks = (kp_ref, kc_ref, kn_ref)
    ks = [[], []]
    for bi in range(3):
        p0 = pl.multiple_of(jnp.clip(n + bi - 1, 0, nb - 1) * W, W)
        ck, sk = cos_ref[pl.ds(p0, W), :], sin_ref[pl.ds(p0, W), :]
        for c in range(ATTN_KV_W // LANES):
            xk = kblocks[bi][:, c * LANES:(c + 1) * LANES].astype(f32)
            ks[c].append(_rope(xk, ck, sk).astype(bf16))
    kcat = [jnp.concatenate(ks[c], axis=0) for c in range(2)]
    vcat = jnp.concatenate([vp_ref[...], vc_ref[...], vn_ref[...]], axis=0)

    r = lax.broadcasted_iota(i32, (W, 3 * W), 0)
    c = lax.broadcasted_iota(i32, (W, 3 * W), 1)
    valid = (c >= r) & (c <= r + 2 * W)
    valid = valid & ((n > 0) | (c >= W)) & ((n < nb - 1) | (c < 2 * W))
    valid4 = jnp.concatenate([valid] * G, axis=0)

    scores, sinks = [], []
    for kh in range(ATTN_KV_HEADS):
        kc_, ko = divmod(kh * HD, LANES)
        k_h = kcat[kc_][:, ko:ko + HD]
        parts = []
        sink_parts = []
        for g in range(G):
            h = kh * G + g
            qc, qo = divmod(h * HD, LANES)
            parts.append(qs[qc][:, qo:qo + HD])
            sink_parts.append(jnp.full((W, 1), sink_ref[h], f32))
        qg = jnp.concatenate(parts, axis=0)
        sinks.append(jnp.concatenate(sink_parts, axis=0))
        scores.append(lax.dot_general(qg, k_h, (((1,), (1,)), ((), ())), preferred_element_type=f32))
    probs, rdenoms = [], []
    for s, sink in zip(scores, sinks):
        s = jnp.where(valid4, s, NEG_INF)
        m = jnp.maximum(jnp.max(s, axis=-1, keepdims=True), sink)
        p = jnp.exp(s - m)
        rdenoms.append(1.0 / (jnp.sum(p, axis=-1, keepdims=True) + jnp.exp(sink - m)))
        probs.append(p.astype(bf16))
    outs = [jnp.dot(p, vcat[:, kh * HD:(kh + 1) * HD], preferred_element_type=f32)
            for kh, p in enumerate(probs)]
    for kh in range(ATTN_KV_HEADS):
        o = outs[kh] * rdenoms[kh]
        for g in range(G):
            h = kh * G + g
            o_ref[:, h * HD:(h + 1) * HD] = o[g * W:(g + 1) * W].astype(bf16)


def _attention(proj, sinks, cos_t, sin_t, nseq, seq):
    W = ATTN_BLOCK
    nb = seq // W
    T = nseq * seq
    qcol = P_AQ // ATTN_Q_W
    kcol = P_AK // ATTN_KV_W
    vcol = P_AV // ATTN_KV_W

    def kv_spec(col, d):
        return pl.BlockSpec(
            (W, ATTN_KV_W), lambda b, n: (b * nb + jnp.clip(n + d, 0, nb - 1), col))

    return pl.pallas_call(
        functools.partial(_attn_kernel, nb=nb),
        grid=(nseq, nb),
        in_specs=[
            pl.BlockSpec(memory_space=pltpu.SMEM),
            pl.BlockSpec((W, ATTN_Q_W), lambda b, n: (b * nb + n, qcol)),
            kv_spec(kcol, -1), kv_spec(kcol, 0), kv_spec(kcol, 1),
            kv_spec(vcol, -1), kv_spec(vcol, 0), kv_spec(vcol, 1),
            pl.BlockSpec(memory_space=pltpu.VMEM),
            pl.BlockSpec(memory_space=pltpu.VMEM),
        ],
        out_specs=pl.BlockSpec((W, ATTN_Q_W), lambda b, n: (b * nb + n, 0)),
        out_shape=jax.ShapeDtypeStruct((T, ATTN_Q_W), bf16),
        compiler_params=_cparams(("parallel", "parallel")),
        name="window_attn",
    )(sinks, proj, proj, proj, proj, proj, proj, proj, cos_t, sin_t)


def _dnconv_kernel(x_ref, w_ref, o_ref, pad_ref, *, seq, rt):
    j = pl.program_id(1)
    half = DN_CONV // 2
    cw = x_ref.shape[1]
    pad_ref[0:SUBLANES, :] = jnp.zeros((SUBLANES, cw), f32)
    pad_ref[seq + SUBLANES:seq + 2 * SUBLANES, :] = jnp.zeros((SUBLANES, cw), f32)
    pad_ref[SUBLANES:seq + SUBLANES, :] = x_ref[...].astype(f32)
    w = w_ref[...]
    qk_blocks = 2 * DN_QK_W // cw
    q_blocks = DN_QK_W // cw
    for r0 in range(0, seq, rt):
        acc = jnp.zeros((rt, cw), f32)
        for t in range(DN_CONV):
            s0 = r0 + SUBLANES - half + t
            acc = acc + pad_ref[s0:s0 + rt, :] * w[t:t + 1, :]
        y = acc * _sigmoid(acc)
        outs = []
        for hh in range(cw // DN_K_DIM):
            seg = y[:, hh * DN_K_DIM:(hh + 1) * DN_K_DIM]
            inv = lax.rsqrt(jnp.sum(seg * seg, axis=-1, keepdims=True) + NORM_EPS)
            fac = jnp.where(j < q_blocks, inv * (DN_K_DIM ** -0.5),
                            jnp.where(j < qk_blocks, inv, jnp.ones_like(inv)))
            outs.append(seg * fac)
        o_ref[r0:r0 + rt, :] = jnp.concatenate(outs, axis=1).astype(bf16)


def _dnconv(proj, conv_w, nseq, seq):
    cw = 256
    T = nseq * seq
    col0 = P_DNQKV // cw
    rt = min(512, seq)
    return pl.pallas_call(
        functools.partial(_dnconv_kernel, seq=seq, rt=rt),
        grid=(nseq, DN_CONV_W // cw),
        in_specs=[
            pl.BlockSpec((seq, cw), lambda b, j: (b, col0 + j)),
            pl.BlockSpec((DN_CONV, cw), lambda b, j: (0, j)),
        ],
        out_specs=pl.BlockSpec((seq, cw), lambda b, j: (b, j)),
        out_shape=jax.ShapeDtypeStruct((T, DN_CONV_W), bf16),
        scratch_shapes=[pltpu.VMEM((seq + 2 * SUBLANES, cw), f32)],
        compiler_params=_cparams(("parallel", "parallel")),
        name="dn_conv",
    )(proj, conv_w)


def _mm(a, b):
    return jnp.dot(a.astype(bf16), b.astype(bf16), preferred_element_type=f32)


def _unit_tri_inverse_all(lmats, eye, blk_i, blk_j):
    same8 = (blk_i // 8) == (blk_j // 8)
    n1 = [jnp.where(same8, -l, 0.0) for l in lmats]
    p = [eye + n for n in n1]
    n2 = [_mm(n, n) for n in n1]
    p = [pp + _mm(pp, nn) for pp, nn in zip(p, n2)]
    n4 = [_mm(n, n) for n in n2]
    inv = [pp + _mm(pp, nn) for pp, nn in zip(p, n4)]
    for s in (8, 16, 32):
        coupling = ((blk_i // (2 * s)) == (blk_j // (2 * s))) & ((blk_i // s) != (blk_j // s))
        t = [_mm(jnp.where(coupling, l, 0.0), iv) for l, iv in zip(lmats, inv)]
        inv = [iv - _mm(iv, tt) for iv, tt in zip(inv, t)]
    return inv


def _chunk_cumsum(x, axis, reverse):
    n = x.shape[axis]
    pos = lax.broadcasted_iota(i32, x.shape, axis) % DN_CHUNK
    shift = 1
    while shift < DN_CHUNK:
        if reverse:
            moved = pltpu.roll(x, n - shift, axis)
            x = x + jnp.where(pos < DN_CHUNK - shift, moved, 0.0)
        else:
            moved = pltpu.roll(x, shift, axis)
            x = x + jnp.where(pos >= shift, moved, 0.0)
        shift *= 2
    return x


def _dnprep_kernel(q_ref, k_ref, v_ref, sm_ref, smt_ref, ar_ref, dr_ref, ac_ref, dc_ref,
                   u_ref, w_ref, qd_ref, kd_ref, aqk_ref, egl_ref):
    hq = pl.program_id(1)
    C = DN_CHUNK
    P = 2 * C
    TB = q_ref.shape[0]
    nch = TB // C

    sm = sm_ref[...]
    beta_full = _sigmoid(sm)
    g_full = -jnp.exp(ar_ref[...]) * _softplus(sm + dr_ref[...])
    shift = (LANES - hq * DN_GROUP) % LANES
    beta_c = pltpu.roll(beta_full, shift, 1)
    g_c = pltpu.roll(g_full, shift, 1)
    smt = smt_ref[...]
    g_r = -jnp.exp(ac_ref[...]) * _softplus(smt + dc_ref[...])

    bi = lax.broadcasted_iota(i32, (P, P), 0)
    bj = lax.broadcasted_iota(i32, (P, P), 1)
    eye = (bi == bj).astype(f32)
    same_head = (bi // C) == (bj // C)

    gcum_c = [_chunk_cumsum(g_c, 0, d == 1) for d in range(2)]
    gcum_r = [_chunk_cumsum(g_r, 1, d == 1) for d in range(2)]

    k2, q2, kk2, qk2, v2 = [], [], [], [], []
    for c in range(nch):
        rows = slice(c * C, (c + 1) * C)
        k = k_ref[rows, :]
        q = q_ref[rows, :]
        kp = jnp.concatenate([k, k], axis=0)
        qp = jnp.concatenate([q, q], axis=0)
        k2.append(kp.astype(f32))
        q2.append(qp.astype(f32))
        kk2.append(lax.dot_general(kp, kp, (((1,), (1,)), ((), ())), preferred_element_type=f32))
        qk2.append(lax.dot_general(qp, kp, (((1,), (1,)), ((), ())), preferred_element_type=f32))
        v2.append(jnp.concatenate([v_ref[rows, :DN_V_DIM], v_ref[rows, DN_V_DIM:]], axis=0).astype(f32))

    inst = [(d, c) for d in range(2) for c in range(nch)]
    gcol, bcol, decay, lmats = [], [], [], []
    for d, c in inst:
        rows = slice(c * C, (c + 1) * C)
        b0, g0 = d * 2, 4 + d * 2
        gc = jnp.concatenate([gcum_c[d][rows, g0:g0 + 1], gcum_c[d][rows, g0 + 1:g0 + 2]], axis=0)
        gr = jnp.concatenate([gcum_r[d][g0:g0 + 1, rows], gcum_r[d][g0 + 1:g0 + 2, rows]], axis=1)
        bc = jnp.concatenate([beta_c[rows, b0:b0 + 1], beta_c[rows, b0 + 1:b0 + 2]], axis=0)
        incl = same_head & ((bj <= bi) if d == 0 else (bj >= bi))
        strict = same_head & ((bj < bi) if d == 0 else (bj > bi))
        dec = jnp.where(incl, jnp.exp(jnp.where(incl, gc - gr, 0.0)), 0.0)
        gcol.append(gc)
        bcol.append(bc)
        decay.append(dec)
        lmats.append(jnp.where(strict, kk2[c] * bc * dec, 0.0))

    invs = _unit_tri_inverse_all(lmats, eye, bi, bj)

    for n, (d, c) in enumerate(inst):
        rows = slice(c * C, (c + 1) * C)
        last = C - 1 if d == 0 else 0
        gc, bc = gcol[n], bcol[n]
        eg = jnp.exp(gc)
        rhs = jnp.concatenate([v2[c] * bc, k2[c] * (bc * eg)], axis=1)
        sol = _mm(invs[n], rhs)
        gl0 = gc[last:last + 1, :]
        gl1 = gc[C + last:C + last + 1, :]
        glast = jnp.concatenate([jnp.broadcast_to(gl0, (C, 1)), jnp.broadcast_to(gl1, (C, 1))], axis=0)
        qd = q2[c] * eg
        kd = k2[c] * jnp.exp(glast - gc)
        aqk = qk2[c] * decay[n]
        for hvl in range(2):
            pr = slice(hvl * C, (hvl + 1) * C)
            cols = slice(hvl * DN_V_DIM, (hvl + 1) * DN_V_DIM)
            u_ref[d, rows, cols] = sol[pr, :DN_V_DIM].astype(bf16)
            w_ref[d, rows, cols] = sol[pr, DN_V_DIM:].astype(bf16)
            qd_ref[d, rows, cols] = qd[pr].astype(bf16)
            kd_ref[d, rows, cols] = kd[pr].astype(bf16)
            aqk_ref[d, rows, pr] = aqk[pr, pr].astype(bf16)
        egl_ref[d, 0, c, 0:1, :] = jnp.broadcast_to(jnp.exp(gl0), (1, LANES))
        egl_ref[d, 0, c, 1:2, :] = jnp.broadcast_to(jnp.exp(gl1), (1, LANES))


def _dnprep(qkv, small, small_t, ar, dr, ac, dc):
    T = qkv.shape[0]
    TB = DN_PREP_TB
    nblk = T // TB
    nch = TB // DN_CHUNK
    kcol = DN_QK_W // DN_K_DIM
    vcol = 2 * DN_QK_W // (2 * DN_V_DIM)
    wide = lambda: pl.BlockSpec((2, TB, 2 * DN_V_DIM), lambda i, h: (0, i, h))
    big = jax.ShapeDtypeStruct((2, T, DN_V_W), bf16)
    return pl.pallas_call(
        _dnprep_kernel,
        grid=(nblk, DN_QK_HEADS),
        in_specs=[
            pl.BlockSpec((TB, DN_K_DIM), lambda i, h: (i, h)),
            pl.BlockSpec((TB, DN_K_DIM), lambda i, h: (i, kcol + h)),
            pl.BlockSpec((TB, 2 * DN_V_DIM), lambda i, h: (i, vcol + h)),
            pl.BlockSpec((TB, LANES), lambda i, h: (i, 0)),
            pl.BlockSpec((DN_GROUP, TB), lambda i, h: (h, i)),
            pl.BlockSpec((1, LANES), lambda i, h: (0, 0)),
            pl.BlockSpec((1, LANES), lambda i, h: (0, 0)),
            pl.BlockSpec((DN_GROUP, 1), lambda i, h: (h, 0)),
            pl.BlockSpec((DN_GROUP, 1), lambda i, h: (h, 0)),
        ],
        out_specs=[
            wide(), wide(), wide(), wide(),
            pl.BlockSpec((2, TB, 2 * DN_CHUNK), lambda i, h: (0, i, h)),
            pl.BlockSpec((2, 1, nch, 2, LANES), lambda i, h: (0, h, i, 0, 0)),
        ],
        out_shape=[
            big, big, big, big,
            jax.ShapeDtypeStruct((2, T, DN_QK_HEADS * 2 * DN_CHUNK), bf16),
            jax.ShapeDtypeStruct((2, DN_QK_HEADS, T // DN_CHUNK, 2, LANES), f32),
        ],
        compiler_params=_cparams(("parallel", "parallel")),
        name="dn_prep",
    )(qkv, qkv, qkv, small, small_t, ar, dr, ac, dc)


def _dnscan_kernel(uf, wf, qf, kf, af, ef, ub, wb, qb, kb, ab, eb, of_ref, ob_ref, s_ref):
    t = pl.program_id(1)
    C = DN_CHUNK
    nch = uf.shape[1] // C

    @pl.when(t == 0)
    def _():
        s_ref[...] = jnp.zeros(s_ref.shape, f32)

    dirs = ((uf, wf, qf, kf, af, ef, of_ref), (ub, wb, qb, kb, ab, eb, ob_ref))
    chains = [(d, hv) for d in range(2) for hv in range(DN_V_HEADS)]
    for step in range(nch):
        def rows(d):
            c = step if d == 0 else nch - 1 - step
            return c, slice(c * C, (c + 1) * C)

        def cols(hv):
            return slice(hv * DN_V_DIM, (hv + 1) * DN_V_DIM)

        s_old = [s_ref[d * DN_V_HEADS + hv] for d, hv in chains]
        s_bf = [s.astype(bf16) for s in s_old]
        w_s = [jnp.dot(dirs[d][1][0, rows(d)[1], cols(hv)], sb, preferred_element_type=f32)
               for (d, hv), sb in zip(chains, s_bf)]
        q_s = [jnp.dot(dirs[d][2][0, rows(d)[1], cols(hv)], sb, preferred_element_type=f32)
               for (d, hv), sb in zip(chains, s_bf)]
        v_bf = [(dirs[d][0][0, rows(d)[1], cols(hv)].astype(f32) - ws).astype(bf16)
                for (d, hv), ws in zip(chains, w_s)]
        a_v = [jnp.dot(dirs[d][4][0, rows(d)[1], hv * C:(hv + 1) * C], vb, preferred_element_type=f32)
               for (d, hv), vb in zip(chains, v_bf)]
        upd = [lax.dot_general(dirs[d][3][0, rows(d)[1], cols(hv)], vb, (((0,), (0,)), ((), ())),
                               preferred_element_type=f32)
               for (d, hv), vb in zip(chains, v_bf)]
        for n, (d, hv) in enumerate(chains):
            c, rs = rows(d)
            e = dirs[d][5][0, hv // 2, c, hv % 2:hv % 2 + 1, :]
            s_ref[d * DN_V_HEADS + hv] = s_old[n] * e + upd[n]
            dirs[d][6][rs, cols(hv)] = (q_s[n] + a_v[n]).astype(bf16)


def _dnscan(u, w, qd, kd, aqk, egl, nseq, seq):
    TB = DN_TB
    nt = seq // TB
    nch = TB // DN_CHUNK
    T = nseq * seq

    def blk(d):
        if d == 0:
            return lambda b, t: b * nt + t
        return lambda b, t: b * nt + nt - 1 - t

    def specs(d):
        wide = lambda: pl.BlockSpec((1, TB, DN_V_W), lambda b, t: (d, blk(d)(b, t), 0))
        return [wide(), wide(), wide(), wide(),
                pl.BlockSpec((1, TB, DN_V_HEADS * DN_CHUNK), lambda b, t: (d, blk(d)(b, t), 0)),
                pl.BlockSpec((1, DN_QK_HEADS, nch, 2, LANES), lambda b, t: (d, 0, blk(d)(b, t), 0, 0))]

    def ospec(d):
        return pl.BlockSpec((TB, DN_V_W), lambda b, t: (blk(d)(b, t), 0))

    args = (u, w, qd, kd, aqk, egl)
    return pl.pallas_call(
        _dnscan_kernel,
        grid=(nseq, nt),
        in_specs=specs(0) + specs(1),
        out_specs=[ospec(0), ospec(1)],
        out_shape=[jax.ShapeDtypeStruct((T, DN_V_W), bf16)] * 2,
        scratch_shapes=[pltpu.VMEM((2 * DN_V_HEADS, DN_K_DIM, DN_V_DIM), f32)],
        compiler_params=_cparams(("parallel", "arbitrary")),
        name="dn_scan",
    )(*args, *args)


def _post_kernel(xa_ref, xb_ref, oa_ref, of_ref, ob_ref, z_ref, ga_ref, gd_ref, dnw_ref, wa_ref, wd_ref,
                 wo_ref, n2_ref, rw_ref, rb_ref, x2_ref, hp_ref, idx_ref, gate_ref, *, npb):
    sub = POST_SUB
    subs = [slice(s * sub, (s + 1) * sub) for s in range(x2_ref.shape[0] // sub)]
    first_group = pl.program_id(0) < npb
    half = D_MODEL // 2

    odn = []
    for r in subs:
        od = of_ref[r, :].astype(f32) + ob_ref[r, :].astype(f32)
        z = z_ref[r, :].astype(f32)
        parts = []
        for h in range(DN_V_HEADS):
            cols = slice(h * DN_V_DIM, (h + 1) * DN_V_DIM)
            seg = od[:, cols]
            seg = seg * lax.rsqrt(jnp.mean(seg * seg, axis=-1, keepdims=True) + NORM_EPS) * dnw_ref[...]
            zz = z[:, cols]
            parts.append((seg * (zz * _sigmoid(zz))).astype(bf16))
        odn.append(jnp.concatenate(parts, axis=1))
    ya = [jnp.dot(oa_ref[r, :], wa_ref[...], preferred_element_type=f32) for r in subs]
    yd = [jnp.dot(o, wd_ref[...], preferred_element_type=f32) for o in odn]
    merged = [(_sigmoid(ga_ref[r, :].astype(f32)) * a + _sigmoid(gd_ref[r, :].astype(f32)) * d).astype(bf16)
              for r, a, d in zip(subs, ya, yd)]
    mo = [jnp.dot(m, wo_ref[...], preferred_element_type=f32) for m in merged]
    h2s = []
    for r, m in zip(subs, mo):
        x2 = jnp.where(first_group, xa_ref[r, :], xb_ref[r, :]) + m
        x2_ref[r, :] = x2
        h2s.append(x2 * lax.rsqrt(jnp.mean(x2 * x2, axis=-1, keepdims=True) + NORM_EPS) * n2_ref[...])
    logits = [lax.dot_general(rw_ref[...], h2, (((1,), (1,)), ((), ())), precision=lax.Precision.HIGHEST,
                              preferred_element_type=f32) + rb_ref[...] for h2 in h2s]
    for n, (r, h2) in enumerate(zip(subs, h2s)):
        packed = _pack_halves(h2[:, :half], h2[:, half:])
        for s in range(ROW_TILES):
            hp_ref[_row_tile(n * sub, sub, s), :] = packed[:, s * LANES:(s + 1) * LANES]
        cur = logits[n]
        erow = lax.broadcasted_iota(i32, cur.shape, 0).astype(f32)
        vals, idxs = [], []
        for _ in range(TOP_K):
            m = jnp.max(cur, axis=0, keepdims=True)
            idx = jnp.min(jnp.where(cur == m, erow, float(N_EXPERTS)), axis=0, keepdims=True)
            vals.append(m)
            idxs.append(idx)
            cur = jnp.where(erow == idx, -jnp.inf, cur)
        exps = [jnp.exp(v - vals[0]) for v in vals]
        rtot = 1.0 / (exps[0] + exps[1] + exps[2] + exps[3])
        out_row = lax.broadcasted_iota(i32, (SUBLANES, sub), 0)
        idx_out = jnp.zeros((SUBLANES, sub), f32)
        gate_out = jnp.zeros((SUBLANES, sub), f32)
        for kk in range(TOP_K):
            idx_out = jnp.where(out_row == kk, idxs[kk], idx_out)
            gate_out = jnp.where(out_row == kk, exps[kk] * rtot, gate_out)
        idx_ref[:, r] = idx_out.astype(i32)
        gate_ref[:, r] = gate_out


def _post(xa, xb, o_attn, o_f, o_b, proj, dnw, wa, wd, wo, n2, rw, rb):
    T = xa.shape[0] + xb.shape[0]
    tm = min(POST_TM, xa.shape[0], xb.shape[0])
    assert xa.shape[0] % tm == 0 and xb.shape[0] % tm == 0
    npb = xa.shape[0] // tm
    zcol = P_DNZ // DN_V_W
    const = lambda shape: pl.BlockSpec(shape, lambda i: (0, 0))
    row = lambda width, col=0: pl.BlockSpec((tm, width), lambda i: (i, col))
    return pl.pallas_call(
        functools.partial(_post_kernel, npb=npb),
        grid=(T // tm,),
        in_specs=_group_row_specs(tm, D_MODEL, npb) + [
            row(ATTN_Q_W), row(DN_V_W), row(DN_V_W),
            row(DN_V_W, zcol), row(D_MODEL, P_GA // D_MODEL), row(D_MODEL, P_GD // D_MODEL),
            const((1, DN_V_DIM)), const((ATTN_Q_W, D_MODEL)), const((DN_V_W, D_MODEL)),
            const((D_MODEL, D_MODEL)), const((1, D_MODEL)), const((N_EXPERTS, D_MODEL)),
            const((N_EXPERTS, 1)),
        ],
        out_specs=[row(D_MODEL), pl.BlockSpec((tm * ROW_TILES, LANES), lambda i: (i, 0)),
                   pl.BlockSpec((SUBLANES, tm), lambda i: (0, i)),
                   pl.BlockSpec((SUBLANES, tm), lambda i: (0, i))],
        out_shape=[
            jax.ShapeDtypeStruct((T, D_MODEL), f32),
            jax.ShapeDtypeStruct((T * ROW_TILES, LANES), u32),
            jax.ShapeDtypeStruct((SUBLANES, T), i32),
            jax.ShapeDtypeStruct((SUBLANES, T), f32),
        ],
        compiler_params=_cparams(("parallel",)),
        name="post_mix",
    )(xa, xb, o_attn, o_f, o_b, proj, proj, proj, dnw, wa, wd, wo, n2, rw, rb)


def _scatter_kernel(poff_ref, pn_ref, pos_ref, h_ref, xs_ref, zero_ref, sem, zsem):
    R = pos_ref.shape[2] // TOP_K
    nbits = MOE_TM.bit_length() - 1

    @pl.when(pl.program_id(0) == 0)
    def _():
        zero_ref[...] = jnp.zeros(zero_ref.shape, u32)

        def per_expert(e, carry):
            off = poff_ref[e]
            n = pn_ref[e]
            for bit in reversed(range(nbits)):
                size = 1 << bit
                take = ((n >> bit) & 1) == 1

                @pl.when(take)
                def _():
                    cp = pltpu.make_async_copy(zero_ref.at[_row_span(0, size)],
                                               xs_ref.at[_row_span(off, size)], zsem)
                    cp.start()
                    cp.wait()

                off = off + jnp.where(take, size, 0)
            return carry

        lax.fori_loop(0, N_EXPERTS, per_expert, 0)

        zrows = zero_ref.shape[0] // ROW_TILES

        def tail(j, carry):
            row = poff_ref[N_EXPERTS] + j * zrows
            cp = pltpu.make_async_copy(zero_ref, xs_ref.at[_row_span(row, zrows)], zsem)
            cp.start()
            cp.wait()
            return carry

        lax.fori_loop(0, pn_ref[N_EXPERTS], tail, 0)

    def body(t, carry):
        for k in range(TOP_K):
            p = pos_ref[0, 0, t * TOP_K + k]
            pltpu.make_async_copy(h_ref.at[_row_span(t)], xs_ref.at[_row_span(p)], sem).start()
        return carry

    lax.fori_loop(0, R, body, 0, unroll=2)
    done = xs_ref.at[_row_span(0, TOP_K * R)]
    pltpu.make_async_copy(done, done, sem).wait()


def _scatter_rows(pad_off, pad_n, pos, hp, n_slots):
    T = hp.shape[0] // ROW_TILES
    R = min(SCATTER_R, T)
    grid_spec = pltpu.PrefetchScalarGridSpec(
        num_scalar_prefetch=2,
        grid=(T // R,),
        in_specs=[
            pl.BlockSpec((1, 1, TOP_K * R), lambda i, po, pn: (i, 0, 0), memory_space=pltpu.SMEM),
            pl.BlockSpec((R * ROW_TILES, LANES), lambda i, po, pn: (i, 0)),
        ],
        out_specs=pl.BlockSpec(memory_space=pl.ANY),
        scratch_shapes=[pltpu.VMEM((MOE_TM // 2 * ROW_TILES, LANES), u32),
                        pltpu.SemaphoreType.DMA(()), pltpu.SemaphoreType.DMA(())],
    )
    return pl.pallas_call(
        _scatter_kernel,
        grid_spec=grid_spec,
        out_shape=jax.ShapeDtypeStruct((n_slots * ROW_TILES, LANES), u32),
        compiler_params=_cparams(("arbitrary",)),
        name="moe_scatter",
    )(pad_off, pad_n, pos.reshape(T // R, 1, TOP_K * R), hp)


def _moe_kernel(te_ref, nu_ref, x_ref, wg_ref, wu_ref, bg_ref, bu_ref, wd_ref, bd_ref,
                o_ref, xb_ref, acc_ref, *, nf):
    i = pl.program_id(0)
    f = pl.program_id(1)
    half = D_MODEL // 2
    used = i < nu_ref[0]

    @pl.when((i == 0) & (f == 0))
    def _():
        acc_ref[...] = jnp.zeros(acc_ref.shape, f32)

    @pl.when(used & (f == 0))
    def _():
        for s in range(ROW_TILES):
            lo, hi = _unpack_halves(x_ref[_row_tile(0, xb_ref.shape[0], s), :])
            xb_ref[:, s * LANES:(s + 1) * LANES] = lo.astype(bf16)
            xb_ref[:, half + s * LANES:half + (s + 1) * LANES] = hi.astype(bf16)

    @pl.when(used)
    def _():
        xb = xb_ref[...]
        g = jnp.dot(xb, wg_ref[0, 0], preferred_element_type=f32) + bg_ref[0]
        u = jnp.dot(xb, wu_ref[0, 0], preferred_element_type=f32) + bu_ref[0]
        gate = jnp.minimum(g, SWIGLU_LIMIT)
        up = jnp.clip(u, -SWIGLU_LIMIT, SWIGLU_LIMIT)
        act = (up + 1.0) * (gate * _sigmoid(SWIGLU_ALPHA * gate))
        part = jnp.dot(act.astype(bf16), wd_ref[0].astype(bf16), preferred_element_type=f32)

        if nf == 1:
            total = part
        else:
            total = jnp.where(f == 0, part, acc_ref[...] + part)
            acc_ref[...] = total
        y = total + bd_ref[0]
        _store_row_tiles(o_ref, _pack_halves(y[:, :half], y[:, half:]))

    @pl.when(jnp.logical_not(used) & (f == nf - 1))
    def _():
        o_ref[...] = jnp.zeros(o_ref.shape, u32)


def _moe(tile_expert, n_used, xs, wg, wu, bg, bu, wd, bd):
    n_slots = xs.shape[0] // ROW_TILES
    tm, tf = MOE_TM, MOE_TF
    n_tiles = n_slots // tm
    nf = D_FF // tf

    def fidx(i, f, te, nu):
        return jnp.where(i < nu[0], f, nf - 1)

    grid_spec = pltpu.PrefetchScalarGridSpec(
        num_scalar_prefetch=2,
        grid=(n_tiles, nf),
        in_specs=[
            pl.BlockSpec((tm * ROW_TILES, LANES), lambda i, f, te, nu: (i, 0)),
            pl.BlockSpec((1, 1, D_MODEL, tf), lambda i, f, te, nu: (te[i], fidx(i, f, te, nu), 0, 0)),
            pl.BlockSpec((1, 1, D_MODEL, tf), lambda i, f, te, nu: (te[i], fidx(i, f, te, nu), 0, 0)),
            pl.BlockSpec((1, 1, tf), lambda i, f, te, nu: (te[i], 0, fidx(i, f, te, nu))),
            pl.BlockSpec((1, 1, tf), lambda i, f, te, nu: (te[i], 0, fidx(i, f, te, nu))),
            pl.BlockSpec((1, tf, D_MODEL), lambda i, f, te, nu: (te[i], fidx(i, f, te, nu), 0)),
            pl.BlockSpec((1, 1, D_MODEL), lambda i, f, te, nu: (te[i], 0, 0)),
        ],
        out_specs=pl.BlockSpec((tm * ROW_TILES, LANES), lambda i, f, te, nu: (i, 0)),
        scratch_shapes=[pltpu.VMEM((tm, D_MODEL), bf16), pltpu.VMEM((tm, D_MODEL), f32)],
    )
    return pl.pallas_call(
        functools.partial(_moe_kernel, nf=nf),
        grid_spec=grid_spec,
        out_shape=jax.ShapeDtypeStruct((n_slots * ROW_TILES, LANES), u32),
        compiler_params=_cparams(("arbitrary", "arbitrary")),
        name="moe_experts",
    )(tile_expert, n_used, xs, wg, wu, bg, bu, wd, bd)


def _combine_kernel(pos_ref, posn_ref, x2_ref, gate_ref, fw_ref, ys_ref, oa_ref, ob_ref,
                    buf_ref, acc_ref, sem, *, npb):
    R = x2_ref.shape[0]
    half = D_MODEL // 2
    nrows = TOP_K * R
    i = pl.program_id(0)
    slot = i % 2

    def issue(p_ref, dst_slot):
        def body(a, carry):
            p = p_ref[0, 0, a]
            pltpu.make_async_copy(ys_ref.at[_row_span(p)],
                                  buf_ref.at[_row_span(dst_slot * nrows + a)], sem.at[dst_slot]).start()
            return carry

        lax.fori_loop(0, nrows, body, 0, unroll=8)

    @pl.when(i == 0)
    def _():
        issue(pos_ref, 0)

    @pl.when(i + 1 < pl.num_programs(0))
    def _():
        issue(posn_ref, 1 - slot)

    pltpu.make_async_copy(ys_ref.at[_row_span(0, nrows)], buf_ref.at[_row_span(slot * nrows, nrows)],
                          sem.at[slot]).wait()
    row0 = slot * nrows

    gates = gate_ref[...]
    gk = [gates[:, k:k + 1] for k in range(TOP_K)]
    ss = jnp.zeros((R, 1), f32)
    for s in range(ROW_TILES):
        cl = slice(s * LANES, (s + 1) * LANES)
        ch = slice(half + s * LANES, half + (s + 1) * LANES)
        lo_acc = x2_ref[:, cl]
        hi_acc = x2_ref[:, ch]
        for k in range(TOP_K):
            lo, hi = _unpack_halves(buf_ref[_row_tile(row0 + k * R, R, s), :])
            lo_acc = lo_acc + gk[k] * lo
            hi_acc = hi_acc + gk[k] * hi
        ss = ss + jnp.sum(lo_acc * lo_acc, axis=-1, keepdims=True) + jnp.sum(
            hi_acc * hi_acc, axis=-1, keepdims=True)
        acc_ref[:, cl] = lo_acc
        acc_ref[:, ch] = hi_acc
    inv = lax.rsqrt(ss / D_MODEL + NORM_EPS)

    @pl.when(i < npb)
    def _():
        oa_ref[...] = acc_ref[...] * inv * fw_ref[...]

    @pl.when(i >= npb)
    def _():
        ob_ref[...] = acc_ref[...] * inv * fw_ref[...]


def _combine(pos, x2, gates, fw, ys, rows_a):
    T = x2.shape[0]
    R = min(COMB_R, rows_a, T - rows_a)
    assert rows_a % R == 0 and T % R == 0
    npb = rows_a // R
    nsteps = T // R
    return pl.pallas_call(
        functools.partial(_combine_kernel, npb=npb),
        grid=(nsteps,),
        in_specs=[
            pl.BlockSpec((1, 1, TOP_K * R), lambda i: (i, 0, 0), memory_space=pltpu.SMEM),
            pl.BlockSpec((1, 1, TOP_K * R), lambda i: (jnp.minimum(i + 1, nsteps - 1), 0, 0),
                         memory_space=pltpu.SMEM),
            pl.BlockSpec((R, D_MODEL), lambda i: (i, 0)),
            pl.BlockSpec((R, TOP_K), lambda i: (i, 0)),
            pl.BlockSpec((1, D_MODEL), lambda i: (0, 0)),
            pl.BlockSpec(memory_space=pl.ANY),
        ],
        out_specs=_group_row_specs(R, D_MODEL, npb),
        out_shape=[jax.ShapeDtypeStruct((rows_a, D_MODEL), f32),
                   jax.ShapeDtypeStruct((T - rows_a, D_MODEL), f32)],
        scratch_shapes=[pltpu.VMEM((2 * TOP_K * R * ROW_TILES, LANES), u32),
                        pltpu.VMEM((R, D_MODEL), f32),
                        pltpu.SemaphoreType.DMA((2,))],
        compiler_params=_cparams(("arbitrary",)),
        name="moe_combine",
    )(pos, pos, x2, gates, fw, ys)


def _dn_channels(t):
    lead = t.shape[:-1]
    t = t.reshape(lead + (2, 2, DN_QK_HEADS, 2))
    perm = tuple(range(len(lead))) + tuple(len(lead) + a for a in (2, 0, 1, 3))
    return t.transpose(perm).reshape(lead + (4 * DN_V_HEADS,))


def _pad_lanes(t):
    pad = [(0, 0)] * (t.ndim - 1) + [(0, LANES - t.shape[-1])]
    return jnp.pad(t, pad)


def _deint_kernel(w_ref, p_ref, g_ref, u_ref):
    for c in range(w_ref.shape[2] // (2 * LANES)):
        w = w_ref[0, :, c * 2 * LANES:(c + 1) * 2 * LANES].astype(bf16)
        r = jnp.dot(w, p_ref[...], preferred_element_type=f32)
        g_ref[0, 0, :, c * LANES:(c + 1) * LANES] = r[:, :LANES].astype(bf16)
        u_ref[0, 0, :, c * LANES:(c + 1) * LANES] = r[:, LANES:].astype(bf16)


def _deinterleave(w):
    E, D, F2 = w.shape
    rb, cb = 1024, 2 * MOE_TF
    i = jnp.arange(2 * LANES)
    src = jnp.where(i < LANES, 2 * i, 2 * (i - LANES) + 1)
    perm = (jnp.arange(2 * LANES)[:, None] == src[None, :]).astype(bf16)
    out = jax.ShapeDtypeStruct((E, F2 // cb, D, MOE_TF), bf16)
    return pl.pallas_call(
        _deint_kernel,
        grid=(E, D // rb, F2 // cb),
        in_specs=[pl.BlockSpec((1, rb, cb), lambda e, r, c: (e, r, c)),
                  pl.BlockSpec((2 * LANES, 2 * LANES), lambda e, r, c: (0, 0))],
        out_specs=[pl.BlockSpec((1, 1, rb, MOE_TF), lambda e, r, c: (e, c, r, 0))] * 2,
        out_shape=[out, out],
        compiler_params=_cparams(("parallel", "parallel", "parallel")),
        name="moe_deinterleave",
    )(w, perm)


def _routing(top_idx, T, tm):
    A = T * TOP_K
    n_tiles = A // tm + N_EXPERTS
    n_slots = n_tiles * tm
    flat_e = top_idx.reshape(-1)
    onehot = (flat_e[:, None] == jnp.arange(N_EXPERTS, dtype=i32)[None, :]).astype(i32)
    csum = jnp.cumsum(onehot, axis=0)
    counts = csum[-1]
    padded = ((counts + tm - 1) // tm) * tm
    padded_end = jnp.cumsum(padded)
    padded_start = padded_end - padded
    pos = jnp.sum(onehot * (csum - 1 + padded_start[None, :]), axis=1).astype(i32)
    tile_start = jnp.arange(n_tiles, dtype=i32) * tm
    tile_expert = jnp.minimum(
        jnp.sum(padded_end[None, :] <= tile_start[:, None], axis=1), N_EXPERTS - 1).astype(i32)
    n_used = (padded_end[-1] // tm).astype(i32).reshape(1)
    pad_off = jnp.concatenate([padded_start + counts, padded_end[-1:]]).astype(i32)
    pad_n = jnp.concatenate([padded - counts, (n_slots - padded_end[-1:]) // (tm // 2)]).astype(i32)
    return pos, pad_off, pad_n, tile_expert, n_used, n_slots


def _layer(xa, xb, seq, norm1_w, w_in, b_in, attn_sinks, dn_conv_w, dn_a_log, dn_dt_bias,
           dn_norm_w, w_attn_o, w_dn_o, w_out, norm2_w, router_w, router_b, w_gate_up, b_gate_up,
           w_down, b_down, final_norm_w):
    T = xa.shape[0] + xb.shape[0]
    nseq = T // seq

    order = ([_OFF_GA, D_MODEL], [_OFF_GD, D_MODEL], [_OFF_DNQKV, DN_CONV_W], [_OFF_AQ, ATTN_Q_W],
             [_OFF_DNZ, DN_V_W], [_OFF_AK, ATTN_KV_W], [_OFF_AV, ATTN_KV_W])
    w_main = jnp.concatenate([w_in[:, o:o + n] for o, n in order], axis=1).astype(bf16)
    w_main = w_main.reshape(D_MODEL, P_W // IN_TN, IN_TN).transpose(1, 0, 2)
    b_main = jnp.concatenate([b_in[o:o + n] for o, n in order])[None, :]
    nsm = 4 * DN_V_HEADS
    w_small = _pad_lanes(_dn_channels(w_in[:, _OFF_DNB:_OFF_DNB + nsm])).astype(bf16)
    b_small = _pad_lanes(_dn_channels(b_in[_OFF_DNB:_OFF_DNB + nsm]))
    zeros_b = jnp.zeros((2 * DN_V_HEADS,), f32)
    a_ch = _pad_lanes(_dn_channels(jnp.concatenate([zeros_b, dn_a_log.reshape(-1)])))
    d_ch = _pad_lanes(_dn_channels(jnp.concatenate([zeros_b, dn_dt_bias.reshape(-1)])))

    proj, small, small_t = _inproj(xa, xb, norm1_w[None, :], w_main, b_main, w_small, b_small[None, :],
                                   w_small.T, b_small[:, None])

    pos = jnp.arange(seq, dtype=f32)
    inv_freq = 1.0 / (ROPE_THETA ** (jnp.arange(0, ROPE_DIM, 2, dtype=f32) / ROPE_DIM))
    ang = pos[:, None] * inv_freq[None, :]
    cos, sin = jnp.cos(ang), jnp.sin(ang)
    half = ROPE_DIM // 2
    ones_tail = jnp.ones((seq, ATTN_HEAD_DIM - ROPE_DIM), f32)
    cos_head = jnp.concatenate([cos, cos, ones_tail], axis=1)
    sin_head = jnp.concatenate([-sin, sin, 0.0 * ones_tail], axis=1)
    cos_t = jnp.concatenate([cos_head, cos_head], axis=1)
    sin_t = jnp.concatenate([sin_head, sin_head], axis=1)

    o_attn = _attention(proj, attn_sinks, cos_t, sin_t, nseq, seq)

    qkv = _dnconv(proj, dn_conv_w, nseq, seq)
    u, w, qd, kd, aqk, egl = _dnprep(qkv, small, small_t, a_ch[None, :], d_ch[None, :],
                                     a_ch[:, None], d_ch[:, None])
    o_f, o_b = _dnscan(u, w, qd, kd, aqk, egl, nseq, seq)

    x2, hp, top_idx_t, top_gate_t = _post(
        xa, xb, o_attn, o_f, o_b, proj, dn_norm_w[None, :], w_attn_o.astype(bf16), w_dn_o.astype(bf16),
        w_out.astype(bf16), norm2_w[None, :], router_w.T, router_b[:, None])
    top_idx = top_idx_t[:TOP_K].T
    top_gate = top_gate_t[:TOP_K].T

    pos_a, pad_off, pad_n, tile_expert, n_used, n_slots = _routing(top_idx, T, MOE_TM)
    xs = _scatter_rows(pad_off, pad_n, pos_a, hp, n_slots)
    wg, wu = _deinterleave(w_gate_up)
    b_pairs = b_gate_up.reshape(N_EXPERTS, 1, D_FF, 2)
    bg, bu = b_pairs[..., 0], b_pairs[..., 1]
    ys = _moe(tile_expert, n_used, xs, wg, wu, bg, bu, w_down, b_down[:, None, :])
    R = min(COMB_R, xa.shape[0], xb.shape[0])
    pos_tiles = pos_a.reshape(T // R, R, TOP_K).transpose(0, 2, 1).reshape(T // R, 1, TOP_K * R)
    return _combine(pos_tiles, x2, top_gate, final_norm_w[None, :], ys, xa.shape[0])


def kernel(x_prompt, x_sample, norm1_w, w_in, b_in, attn_sinks, dn_conv_w, dn_a_log, dn_dt_bias,
           dn_norm_w, w_attn_o, w_dn_o, w_out, norm2_w, router_w, router_b, w_gate_up, b_gate_up,
           w_down, b_down, final_norm_w):
    bp, seq, d = x_prompt.shape
    bs = x_sample.shape[0]
    assert x_sample.shape[1] == seq and d == D_MODEL
    ya, yb = _layer(x_prompt.reshape(bp * seq, d), x_sample.reshape(bs * seq, d), seq,
                    norm1_w[0], w_in[0], b_in[0], attn_sinks[0], dn_conv_w[0], dn_a_log[0],
                    dn_dt_bias[0], dn_norm_w[0], w_attn_o[0], w_dn_o[0], w_out[0], norm2_w[0],
                    router_w[0], router_b[0], w_gate_up[0], b_gate_up[0], w_down[0], b_down[0],
                    final_norm_w)
    return (ya.reshape(bp, seq, d), yb.reshape(bs, seq, d))
```

```python
import functools

import jax
import jax.numpy as jnp
from jax import lax
from jax.experimental import pallas as pl
from jax.experimental.pallas import tpu as pltpu

f32 = jnp.float32
bf16 = jnp.bfloat16
u32 = jnp.uint32
i32 = jnp.int32

D_MODEL = 2048
ATTN_HEADS = 16
ATTN_KV_HEADS = 4
ATTN_HEAD_DIM = 64
ATTN_BLOCK = 128
ROPE_THETA = 500000.0
ROPE_DIM = ATTN_HEAD_DIM // 4
DN_QK_HEADS = 4
DN_V_HEADS = 8
DN_K_DIM = 128
DN_V_DIM = 128
DN_CONV = 5
DN_CHUNK = 64
N_EXPERTS = 32
TOP_K = 4
D_FF = D_MODEL
SWIGLU_LIMIT = 7.0
SWIGLU_ALPHA = 1.702
NORM_EPS = 1e-6
NEG_INF = -1e30

ATTN_Q_W = ATTN_HEADS * ATTN_HEAD_DIM
ATTN_KV_W = ATTN_KV_HEADS * ATTN_HEAD_DIM
DN_QK_W = DN_QK_HEADS * DN_K_DIM
DN_V_W = DN_V_HEADS * DN_V_DIM
DN_CONV_W = 2 * DN_QK_W + DN_V_W

_OFF_AQ = 0
_OFF_AK = _OFF_AQ + ATTN_Q_W
_OFF_AV = _OFF_AK + ATTN_KV_W
_OFF_DNQKV = _OFF_AV + ATTN_KV_W
_OFF_DNZ = _OFF_DNQKV + DN_CONV_W
_OFF_DNB = _OFF_DNZ + DN_V_W
_OFF_DNA = _OFF_DNB + 2 * DN_V_HEADS
_OFF_GA = _OFF_DNA + 2 * DN_V_HEADS
_OFF_GD = _OFF_GA + D_MODEL

P_GA = 0
P_GD = P_GA + D_MODEL
P_DNQKV = P_GD + D_MODEL
P_AQ = P_DNQKV + DN_CONV_W
P_DNZ = P_AQ + ATTN_Q_W
P_AK = P_DNZ + DN_V_W
P_AV = P_AK + ATTN_KV_W
P_W = P_AV + ATTN_KV_W

LANES = 128
SUBLANES = 8
DN_GROUP = 8
VMEM_LIMIT = 56 * 1024 * 1024

IN_TM = 1024
IN_TN = 512
DN_TB = 256
DN_PREP_TB = 512
POST_TM = 256
POST_SUB = 128
MOE_TM = 512
MOE_TF = 1024
SCATTER_R = 1024
COMB_R = 256


def _cparams(sem):
    return pltpu.CompilerParams(dimension_semantics=sem, vmem_limit_bytes=VMEM_LIMIT)


def _sigmoid(x):
    return 1.0 / (1.0 + jnp.exp(-x))


def _softplus(x):
    return jnp.maximum(x, 0.0) + jnp.log(1.0 + jnp.exp(-jnp.abs(x)))


def _pack_halves(lo, hi):
    a = lax.bitcast_convert_type(lo.astype(bf16).astype(f32), u32)
    b = lax.bitcast_convert_type(hi.astype(bf16).astype(f32), u32)
    return (a >> 16) | (b & jnp.uint32(0xFFFF0000))


def _unpack_halves(w):
    lo = lax.bitcast_convert_type(w << 16, f32)
    hi = lax.bitcast_convert_type(w & jnp.uint32(0xFFFF0000), f32)
    return lo, hi


ROW_TILES = D_MODEL // 2 // LANES


def _row_tile(r0, n, s):
    start = r0 * ROW_TILES
    if not isinstance(r0, int):
        start = pl.multiple_of(start, ROW_TILES)
    return pl.ds(start + s, n, stride=ROW_TILES)


def _row_span(r, n=1):
    start = r * ROW_TILES
    if not isinstance(r, int):
        start = pl.multiple_of(start, ROW_TILES)
    return pl.ds(start, n * ROW_TILES)


def _store_row_tiles(ref, packed):
    for s in range(ROW_TILES):
        ref[_row_tile(0, packed.shape[0], s), :] = packed[:, s * LANES:(s + 1) * LANES]


def _group_row_specs(tm, width, npb):
    first = pl.BlockSpec((tm, width), lambda i, *_: (jnp.minimum(i, npb - 1), 0))
    second = pl.BlockSpec((tm, width), lambda i, *_: (jnp.maximum(i - npb, 0), 0))
    return [first, second]


def _inproj_kernel(xa_ref, xb_ref, nw_ref, w_ref, b_ref, ws_ref, bs_ref, wst_ref, bst_ref,
                   o_ref, os_ref, ost_ref, h_ref, *, npb):
    @pl.when(pl.program_id(1) == 0)
    def _():
        x = jnp.where(pl.program_id(0) < npb, xa_ref[...], xb_ref[...])
        ms = jnp.mean(x * x, axis=-1, keepdims=True)
        hb = (x * lax.rsqrt(ms + NORM_EPS) * nw_ref[...]).astype(bf16)
        h_ref[...] = hb
        os_ref[...] = jnp.dot(hb, ws_ref[...], preferred_element_type=f32) + bs_ref[...]
        ost_ref[...] = lax.dot_general(wst_ref[...], hb, (((1,), (1,)), ((), ())),
                                       preferred_element_type=f32) + bst_ref[...]

    acc = jnp.dot(h_ref[...], w_ref[0], preferred_element_type=f32)
    o_ref[...] = (acc + b_ref[...]).astype(bf16)


def _inproj(xa, xb, nw, w, b, ws, bs, wst, bst):
    T = xa.shape[0] + xb.shape[0]
    tm = min(IN_TM, xa.shape[0], xb.shape[0])
    assert xa.shape[0] % tm == 0 and xb.shape[0] % tm == 0
    npb = xa.shape[0] // tm
    grid = (T // tm, P_W // IN_TN)
    return pl.pallas_call(
        functools.partial(_inproj_kernel, npb=npb),
        grid=grid,
        in_specs=_group_row_specs(tm, D_MODEL, npb) + [
            pl.BlockSpec((1, D_MODEL), lambda i, j: (0, 0)),
            pl.BlockSpec((1, D_MODEL, IN_TN), lambda i, j: (j, 0, 0)),
            pl.BlockSpec((1, IN_TN), lambda i, j: (0, j)),
            pl.BlockSpec((D_MODEL, LANES), lambda i, j: (0, 0)),
            pl.BlockSpec((1, LANES), lambda i, j: (0, 0)),
            pl.BlockSpec((LANES, D_MODEL), lambda i, j: (0, 0)),
            pl.BlockSpec((LANES, 1), lambda i, j: (0, 0)),
        ],
        out_specs=[
            pl.BlockSpec((tm, IN_TN), lambda i, j: (i, j)),
            pl.BlockSpec((tm, LANES), lambda i, j: (i, 0)),
            pl.BlockSpec((LANES, tm), lambda i, j: (0, i)),
        ],
        out_shape=[
            jax.ShapeDtypeStruct((T, P_W), bf16),
            jax.ShapeDtypeStruct((T, LANES), f32),
            jax.ShapeDtypeStruct((LANES, T), f32),
        ],
        scratch_shapes=[pltpu.VMEM((tm, D_MODEL), bf16)],
        compiler_params=_cparams(("parallel", "arbitrary")),
        name="inproj",
    )(xa, xb, nw, w, b, ws, bs, wst, bst)


def _rope(x, cos, sin):
    lane = lax.broadcasted_iota(i32, x.shape, 1) % ATTN_HEAD_DIM
    half = ROPE_DIM // 2
    partner = jnp.where((lane >= half) & (lane < ROPE_DIM),
                        pltpu.roll(x, half, 1), pltpu.roll(x, LANES - half, 1))
    return x * cos + partner * sin


def _attn_kernel(sink_ref, q_ref, kp_ref, kc_ref, kn_ref, vp_ref, vc_ref, vn_ref,
                 cos_ref, sin_ref, o_ref, *, nb):
    n = pl.program_id(1)
    W = ATTN_BLOCK
    HD = ATTN_HEAD_DIM
    G = ATTN_HEADS // ATTN_KV_HEADS

    q0 = pl.multiple_of(n * W, W)
    cq, sq = cos_ref[pl.ds(q0, W), :], sin_ref[pl.ds(q0, W), :]
    qs = []
    for c in range(ATTN_Q_W // LANES):
        xq = q_ref[:, c * LANES:(c + 1) * LANES].astype(f32)
        qs.append((_rope(xq, cq, sq) * (HD ** -0.5)).astype(bf16))

    kblocks = (kp_ref, kc_ref, kn_ref)
    ks = [[], []]
    for bi in range(3):
        p0 = pl.multiple_of(jnp.clip(n + bi - 1, 0, nb - 1) * W, W)
        ck, sk = cos_ref[pl.ds(p0, W), :], sin_ref[pl.ds(p0, W), :]
        for c in range(ATTN_KV_W // LANES):
            xk = kblocks[bi][:, c * LANES:(c + 1) * LANES].astype(f32)
            ks[c].append(_rope(xk, ck, sk).astype(bf16))
    kcat = [jnp.concatenate(ks[c], axis=0) for c in range(2)]
    vcat = jnp.concatenate([vp_ref[...], vc_ref[...], vn_ref[...]], axis=0)

    r = lax.broadcasted_iota(i32, (W, 3 * W), 0)
    c = lax.broadcasted_iota(i32, (W, 3 * W), 1)
    valid = (c >= r) & (c <= r + 2 * W)
    valid = valid & ((n > 0) | (c >= W)) & ((n < nb - 1) | (c < 2 * W))
    valid4 = jnp.concatenate([valid] * G, axis=0)

    scores, sinks = [], []
    for kh in range(ATTN_KV_HEADS):
        kc_, ko = divmod(kh * HD, LANES)
        k_h = kcat[kc_][:, ko:ko + HD]
        parts = []
        sink_parts = []
        for g in range(G):
            h = kh * G + g
            qc, qo = divmod(h * HD, LANES)
            parts.append(qs[qc][:, qo:qo + HD])
            sink_parts.append(jnp.full((W, 1), sink_ref[h], f32))
        qg = jnp.concatenate(parts, axis=0)
        sinks.append(jnp.concatenate(sink_parts, axis=0))
        scores.append(lax.dot_general(qg, k_h, (((1,), (1,)), ((), ())), preferred_element_type=f32))
    probs, rdenoms = [], []
    for s, sink in zip(scores, sinks):
        s = jnp.where(valid4, s, NEG_INF)
        m = jnp.maximum(jnp.max(s, axis=-1, keepdims=True), sink)
        p = jnp.exp(s - m)
        rdenoms.append(1.0 / (jnp.sum(p, axis=-1, keepdims=True) + jnp.exp(sink - m)))
        probs.append(p.astype(bf16))
    outs = [jnp.dot(p, vcat[:, kh * HD:(kh + 1) * HD], preferred_element_type=f32)
            for kh, p in enumerate(probs)]
    for kh in range(ATTN_KV_HEADS):
        o = outs[kh] * rdenoms[kh]
        for g in range(G):
            h = kh * G + g
            o_ref[:, h * HD:(h + 1) * HD] = o[g * W:(g + 1) * W].astype(bf16)


def _attention(proj, sinks, cos_t, sin_t, nseq, seq):
    W = ATTN_BLOCK
    nb = seq // W
    T = nseq * seq
    qcol = P_AQ // ATTN_Q_W
    kcol = P_AK // ATTN_KV_W
    vcol = P_AV // ATTN_KV_W

    def kv_spec(col, d):
        return pl.BlockSpec(
            (W, ATTN_KV_W), lambda b, n: (b * nb + jnp.clip(n + d, 0, nb - 1), col))

    return pl.pallas_call(
        functools.partial(_attn_kernel, nb=nb),
        grid=(nseq, nb),
        in_specs=[
            pl.BlockSpec(memory_space=pltpu.SMEM),
            pl.BlockSpec((W, ATTN_Q_W), lambda b, n: (b * nb + n, qcol)),
            kv_spec(kcol, -1), kv_spec(kcol, 0), kv_spec(kcol, 1),
            kv_spec(vcol, -1), kv_spec(vcol, 0), kv_spec(vcol, 1),
            pl.BlockSpec(memory_space=pltpu.VMEM),
            pl.BlockSpec(memory_space=pltpu.VMEM),
        ],
        out_specs=pl.BlockSpec((W, ATTN_Q_W), lambda b, n: (b * nb + n, 0)),
        out_shape=jax.ShapeDtypeStruct((T, ATTN_Q_W), bf16),
        compiler_params=_cparams(("parallel", "parallel")),
        name="window_attn",
    )(sinks, proj, proj, proj, proj, proj, proj, proj, cos_t, sin_t)


def _dnconv_kernel(x_ref, w_ref, o_ref, pad_ref, *, seq, rt):
    j = pl.program_id(1)
    half = DN_CONV // 2
    cw = x_ref.shape[1]
    pad_ref[0:SUBLANES, :] = jnp.zeros((SUBLANES, cw), f32)
    pad_ref[seq + SUBLANES:seq + 2 * SUBLANES, :] = jnp.zeros((SUBLANES, cw), f32)
    pad_ref[SUBLANES:seq + SUBLANES, :] = x_ref[...].astype(f32)
    w = w_ref[...]
    qk_blocks = 2 * DN_QK_W // cw
    q_blocks = DN_QK_W // cw
    for r0 in range(0, seq, rt):
        acc = jnp.zeros((rt, cw), f32)
        for t in range(DN_CONV):
            s0 = r0 + SUBLANES - half + t
            acc = acc + pad_ref[s0:s0 + rt, :] * w[t:t + 1, :]
        y = acc * _sigmoid(acc)
        outs = []
        for hh in range(cw // DN_K_DIM):
            seg = y[:, hh * DN_K_DIM:(hh + 1) * DN_K_DIM]
            inv = lax.rsqrt(jnp.sum(seg * seg, axis=-1, keepdims=True) + NORM_EPS)
            fac = jnp.where(j < q_blocks, inv * (DN_K_DIM ** -0.5),
                            jnp.where(j < qk_blocks, inv, jnp.ones_like(inv)))
            outs.append(seg * fac)
        o_ref[r0:r0 + rt, :] = jnp.concatenate(outs, axis=1).astype(bf16)


def _dnconv(proj, conv_w, nseq, seq):
    cw = 256
    T = nseq * seq
    col0 = P_DNQKV // cw
    rt = min(512, seq)
    return pl.pallas_call(
        functools.partial(_dnconv_kernel, seq=seq, rt=rt),
        grid=(nseq, DN_CONV_W // cw),
        in_specs=[
            pl.BlockSpec((seq, cw), lambda b, j: (b, col0 + j)),
            pl.BlockSpec((DN_CONV, cw), lambda b, j: (0, j)),
        ],
        out_specs=pl.BlockSpec((seq, cw), lambda b, j: (b, j)),
        out_shape=jax.ShapeDtypeStruct((T, DN_CONV_W), bf16),
        scratch_shapes=[pltpu.VMEM((seq + 2 * SUBLANES, cw), f32)],
        compiler_params=_cparams(("parallel", "parallel")),
        name="dn_conv",
    )(proj, conv_w)


def _mm(a, b):
    return jnp.dot(a.astype(bf16), b.astype(bf16), preferred_element_type=f32)


def _unit_tri_inverse_all(lmats, eye, blk_i, blk_j):
    same8 = (blk_i // 8) == (blk_j // 8)
    n1 = [jnp.where(same8, -l, 0.0) for l in lmats]
    p = [eye + n for n in n1]
    n2 = [_mm(n, n) for n in n1]
    p = [pp + _mm(pp, nn) for pp, nn in zip(p, n2)]
    n4 = [_mm(n, n) for n in n2]
    inv = [pp + _mm(pp, nn) for pp, nn in zip(p, n4)]
    for s in (8, 16, 32):
        coupling = ((blk_i // (2 * s)) == (blk_j // (2 * s))) & ((blk_i // s) != (blk_j // s))
        t = [_mm(jnp.where(coupling, l, 0.0), iv) for l, iv in zip(lmats, inv)]
        inv = [iv - _mm(iv, tt) for iv, tt in zip(inv, t)]
    return inv


def _chunk_cumsum(x, axis, reverse):
    n = x.shape[axis]
    pos = lax.broadcasted_iota(i32, x.shape, axis) % DN_CHUNK
    shift = 1
    while shift < DN_CHUNK:
        if reverse:
            moved = pltpu.roll(x, n - shift, axis)
            x = x + jnp.where(pos < DN_CHUNK - shift, moved, 0.0)
        else:
            moved = pltpu.roll(x, shift, axis)
            x = x + jnp.where(pos >= shift, moved, 0.0)
        shift *= 2
    return x


def _dnprep_kernel(q_ref, k_ref, v_ref, sm_ref, smt_ref, ar_ref, dr_ref, ac_ref, dc_ref,
                   u_ref, w_ref, qd_ref, kd_ref, aqk_ref, egl_ref):
    hq = pl.program_id(1)
    C = DN_CHUNK
    P = 2 * C
    TB = q_ref.shape[0]
    nch = TB // C

    sm = sm_ref[...]
    beta_full = _sigmoid(sm)
    g_full = -jnp.exp(ar_ref[...]) * _softplus(sm + dr_ref[...])
    shift = (LANES - hq * DN_GROUP) % LANES
    beta_c = pltpu.roll(beta_full, shift, 1)
    g_c = pltpu.roll(g_full, shift, 1)
    smt = smt_ref[...]
    g_r = -jnp.exp(ac_ref[...]) * _softplus(smt + dc_ref[...])

    bi = lax.broadcasted_iota(i32, (P, P), 0)
    bj = lax.broadcasted_iota(i32, (P, P), 1)
    eye = (bi == bj).astype(f32)
    same_head = (bi // C) == (bj // C)

    gcum_c = [_chunk_cumsum(g_c, 0, d == 1) for d in range(2)]
    gcum_r = [_chunk_cumsum(g_r, 1, d == 1) for d in range(2)]

    k2, q2, kk2, qk2, v2 = [], [], [], [], []
    for c in range(nch):
        rows = slice(c * C, (c + 1) * C)
        k = k_ref[rows, :]
        q = q_ref[rows, :]
        kp = jnp.concatenate([k, k], axis=0)
        qp = jnp.concatenate([q, q], axis=0)
        k2.append(kp.astype(f32))
        q2.append(qp.astype(f32))
        kk2.append(lax.dot_general(kp, kp, (((1,), (1,)), ((), ())), preferred_element_type=f32))
        qk2.append(lax.dot_general(qp, kp, (((1,), (1,)), ((), ())), preferred_element_type=f32))
        v2.append(jnp.concatenate([v_ref[rows, :DN_V_DIM], v_ref[rows, DN_V_DIM:]], axis=0).astype(f32))

    inst = [(d, c) for d in range(2) for c in range(nch)]
    gcol, bcol, decay, lmats = [], [], [], []
    for d, c in inst:
        rows = slice(c * C, (c + 1) * C)
        b0, g0 = d * 2, 4 + d * 2
        gc = jnp.concatenate([gcum_c[d][rows, g0:g0 + 1], gcum_c[d][rows, g0 + 1:g0 + 2]], axis=0)
        gr = jnp.concatenate([gcum_r[d][g0:g0 + 1, rows], gcum_r[d][g0 + 1:g0 + 2, rows]], axis=1)
        bc = jnp.concatenate([beta_c[rows, b0:b0 + 1], beta_c[rows, b0 + 1:b0 + 2]], axis=0)
        incl = same_head & ((bj <= bi) if d == 0 else (bj >= bi))
        strict = same_head & ((bj < bi) if d == 0 else (bj > bi))
        dec = jnp.where(incl, jnp.exp(jnp.where(incl, gc - gr, 0.0)), 0.0)
        gcol.append(gc)
        bcol.append(bc)
        decay.append(dec)
        lmats.append(jnp.where(strict, kk2[c] * bc * dec, 0.0))

    invs = _unit_tri_inverse_all(lmats, eye, bi, bj)

    for n, (d, c) in enumerate(inst):
        rows = slice(c * C, (c + 1) * C)
        last = C - 1 if d == 0 else 0
        gc, bc = gcol[n], bcol[n]
        eg = jnp.exp(gc)
        rhs = jnp.concatenate([v2[c] * bc, k2[c] * (bc * eg)], axis=1)
        sol = _mm(invs[n], rhs)
        gl0 = gc[last:last + 1, :]
        gl1 = gc[C + last:C + last + 1, :]
        glast = jnp.concatenate([jnp.broadcast_to(gl0, (C, 1)), jnp.broadcast_to(gl1, (C, 1))], axis=0)
        qd = q2[c] * eg
        kd = k2[c] * jnp.exp(glast - gc)
        aqk = qk2[c] * decay[n]
        for hvl in range(2):
            pr = slice(hvl * C, (hvl + 1) * C)
            cols = slice(hvl * DN_V_DIM, (hvl + 1) * DN_V_DIM)
            u_ref[d, rows, cols] = sol[pr, :DN_V_DIM].astype(bf16)
            w_ref[d, rows, cols] = sol[pr, DN_V_DIM:].astype(bf16)
            qd_ref[d, rows, cols] = qd[pr].astype(bf16)
            kd_ref[d, rows, cols] = kd[pr].astype(bf16)
            aqk_ref[d, rows, pr] = aqk[pr, pr].astype(bf16)
        egl_ref[d, 0, c, 0:1, :] = jnp.broadcast_to(jnp.exp(gl0), (1, LANES))
        egl_ref[d, 0, c, 1:2, :] = jnp.broadcast_to(jnp.exp(gl1), (1, LANES))


def _dnprep(qkv, small, small_t, ar, dr, ac, dc):
    T = qkv.shape[0]
    TB = DN_PREP_TB
    nblk = T // TB
    nch = TB // DN_CHUNK
    kcol = DN_QK_W // DN_K_DIM
    vcol = 2 * DN_QK_W // (2 * DN_V_DIM)
    wide = lambda: pl.BlockSpec((2, TB, 2 * DN_V_DIM), lambda i, h: (0, i, h))
    big = jax.ShapeDtypeStruct((2, T, DN_V_W), bf16)
    return pl.pallas_call(
        _dnprep_kernel,
        grid=(nblk, DN_QK_HEADS),
        in_specs=[
            pl.BlockSpec((TB, DN_K_DIM), lambda i, h: (i, h)),
            pl.BlockSpec((TB, DN_K_DIM), lambda i, h: (i, kcol + h)),
            pl.BlockSpec((TB, 2 * DN_V_DIM), lambda i, h: (i, vcol + h)),
            pl.BlockSpec((TB, LANES), lambda i, h: (i, 0)),
            pl.BlockSpec((DN_GROUP, TB), lambda i, h: (h, i)),
            pl.BlockSpec((1, LANES), lambda i, h: (0, 0)),
            pl.BlockSpec((1, LANES), lambda i, h: (0, 0)),
            pl.BlockSpec((DN_GROUP, 1), lambda i, h: (h, 0)),
            pl.BlockSpec((DN_GROUP, 1), lambda i, h: (h, 0)),
        ],
        out_specs=[
            wide(), wide(), wide(), wide(),
            pl.BlockSpec((2, TB, 2 * DN_CHUNK), lambda i, h: (0, i, h)),
            pl.BlockSpec((2, 1, nch, 2, LANES), lambda i, h: (0, h, i, 0, 0)),
        ],
        out_shape=[
            big, big, big, big,
            jax.ShapeDtypeStruct((2, T, DN_QK_HEADS * 2 * DN_CHUNK), bf16),
            jax.ShapeDtypeStruct((2, DN_QK_HEADS, T // DN_CHUNK, 2, LANES), f32),
        ],
        compiler_params=_cparams(("parallel", "parallel")),
        name="dn_prep",
    )(qkv, qkv, qkv, small, small_t, ar, dr, ac, dc)


def _dnscan_kernel(uf, wf, qf, kf, af, ef, ub, wb, qb, kb, ab, eb, of_ref, ob_ref, s_ref):
    t = pl.program_id(1)
    C = DN_CHUNK
    nch = uf.shape[1] // C

    @pl.when(t == 0)
    def _():
        s_ref[...] = jnp.zeros(s_ref.shape, f32)

    dirs = ((uf, wf, qf, kf, af, ef, of_ref), (ub, wb, qb, kb, ab, eb, ob_ref))
    chains = [(d, hv) for d in range(2) for hv in range(DN_V_HEADS)]
    for step in range(nch):
        def rows(d):
            c = step if d == 0 else nch - 1 - step
            return c, slice(c * C, (c + 1) * C)

        def cols(hv):
            return slice(hv * DN_V_DIM, (hv + 1) * DN_V_DIM)

        s_old = [s_ref[d * DN_V_HEADS + hv] for d, hv in chains]
        s_bf = [s.astype(bf16) for s in s_old]
        w_s = [jnp.dot(dirs[d][1][0, rows(d)[1], cols(hv)], sb, preferred_element_type=f32)
               for (d, hv), sb in zip(chains, s_bf)]
        q_s = [jnp.dot(dirs[d][2][0, rows(d)[1], cols(hv)], sb, preferred_element_type=f32)
               for (d, hv), sb in zip(chains, s_bf)]
        v_bf = [(dirs[d][0][0, rows(d)[1], cols(hv)].astype(f32) - ws).astype(bf16)
                for (d, hv), ws in zip(chains, w_s)]
        a_v = [jnp.dot(dirs[d][4][0, rows(d)[1], hv * C:(hv + 1) * C], vb, preferred_element_type=f32)
               for (d, hv), vb in zip(chains, v_bf)]
        upd = [lax.dot_general(dirs[d][3][0, rows(d)[1], cols(hv)], vb, (((0,), (0,)), ((), ())),
                               preferred_element_type=f32)
               for (d, hv), vb in zip(chains, v_bf)]
        for n, (d, hv) in enumerate(chains):
            c, rs = rows(d)
            e = dirs[d][5][0, hv // 2, c, hv % 2:hv % 2 + 1, :]
            s_ref[d * DN_V_HEADS + hv] = s_old[n] * e + upd[n]
            dirs[d][6][rs, cols(hv)] = (q_s[n] + a_v[n]).astype(bf16)


def _dnscan(u, w, qd, kd, aqk, egl, nseq, seq):
    TB = DN_TB
    nt = seq // TB
    nch = TB // DN_CHUNK
    T = nseq * seq

    def blk(d):
        if d == 0:
            return lambda b, t: b * nt + t
        return lambda b, t: b * nt + nt - 1 - t

    def specs(d):
        wide = lambda: pl.BlockSpec((1, TB, DN_V_W), lambda b, t: (d, blk(d)(b, t), 0))
        return [wide(), wide(), wide(), wide(),
                pl.BlockSpec((1, TB, DN_V_HEADS * DN_CHUNK), lambda b, t: (d, blk(d)(b, t), 0)),
                pl.BlockSpec((1, DN_QK_HEADS, nch, 2, LANES), lambda b, t: (d, 0, blk(d)(b, t), 0, 0))]

    def ospec(d):
        return pl.BlockSpec((TB, DN_V_W), lambda b, t: (blk(d)(b, t), 0))

    args = (u, w, qd, kd, aqk, egl)
    return pl.pallas_call(
        _dnscan_kernel,
        grid=(nseq, nt),
        in_specs=specs(0) + specs(1),
        out_specs=[ospec(0), ospec(1)],
        out_shape=[jax.ShapeDtypeStruct((T, DN_V_W), bf16)] * 2,
        scratch_shapes=[pltpu.VMEM((2 * DN_V_HEADS, DN_K_DIM, DN_V_DIM), f32)],
        compiler_params=_cparams(("parallel", "arbitrary")),
        name="dn_scan",
    )(*args, *args)


def _post_kernel(xa_ref, xb_ref, oa_ref, of_ref, ob_ref, z_ref, ga_ref, gd_ref, dnw_ref, wa_ref, wd_ref,
                 wo_ref, n2_ref, rw_ref, rb_ref, x2_ref, hp_ref, idx_ref, gate_ref, *, npb):
    sub = POST_SUB
    subs = [slice(s * sub, (s + 1) * sub) for s in range(x2_ref.shape[0] // sub)]
    first_group = pl.program_id(0) < npb
    half = D_MODEL // 2

    odn = []
    for r in subs:
        od = of_ref[r, :].astype(f32) + ob_ref[r, :].astype(f32)
        z = z_ref[r, :].astype(f32)
        parts = []
        for h in range(DN_V_HEADS):
            cols = slice(h * DN_V_DIM, (h + 1) * DN_V_DIM)
            seg = od[:, cols]
            seg = seg * lax.rsqrt(jnp.mean(seg * seg, axis=-1, keepdims=True) + NORM_EPS) * dnw_ref[...]
            zz = z[:, cols]
            parts.append((seg * (zz * _sigmoid(zz))).astype(bf16))
        odn.append(jnp.concatenate(parts, axis=1))
    ya = [jnp.dot(oa_ref[r, :], wa_ref[...], preferred_element_type=f32) for r in subs]
    yd = [jnp.dot(o, wd_ref[...], preferred_element_type=f32) for o in odn]
    merged = [(_sigmoid(ga_ref[r, :].astype(f32)) * a + _sigmoid(gd_ref[r, :].astype(f32)) * d).astype(bf16)
              for r, a, d in zip(subs, ya, yd)]
    mo = [jnp.dot(m, wo_ref[...], preferred_element_type=f32) for m in merged]
    h2s = []
    for r, m in zip(subs, mo):
        x2 = jnp.where(first_group, xa_ref[r, :], xb_ref[r, :]) + m
        x2_ref[r, :] = x2
        h2s.append(x2 * lax.rsqrt(jnp.mean(x2 * x2, axis=-1, keepdims=True) + NORM_EPS) * n2_ref[...])
    logits = [lax.dot_general(rw_ref[...], h2, (((1,), (1,)), ((), ())), precision=lax.Precision.HIGHEST,
                              preferred_element_type=f32) + rb_ref[...] for h2 in h2s]
    for n, (r, h2) in enumerate(zip(subs, h2s)):
        packed = _pack_halves(h2[:, :half], h2[:, half:])
        for s in range(ROW_TILES):
            hp_ref[_row_tile(n * sub, sub, s), :] = packed[:, s * LANES:(s + 1) * LANES]
        cur = logits[n]
        erow = lax.broadcasted_iota(i32, cur.shape, 0).astype(f32)
        vals, idxs = [], []
        for _ in range(TOP_K):
            m = jnp.max(cur, axis=0, keepdims=True)
            idx = jnp.min(jnp.where(cur == m, erow, float(N_EXPERTS)), axis=0, keepdims=True)
            vals.append(m)
            idxs.append(idx)
            cur = jnp.where(erow == idx, -jnp.inf, cur)
        exps = [jnp.exp(v - vals[0]) for v in vals]
        rtot = 1.0 / (exps[0] + exps[1] + exps[2] + exps[3])
        out_row = lax.broadcasted_iota(i32, (SUBLANES, sub), 0)
        idx_out = jnp.zeros((SUBLANES, sub), f32)
        gate_out = jnp.zeros((SUBLANES, sub), f32)
        for kk in range(TOP_K):
            idx_out = jnp.where(out_row == kk, idxs[kk], idx_out)
            gate_out = jnp.where(out_row == kk, exps[kk] * rtot, gate_out)
        idx_ref[:, r] = idx_out.astype(i32)
        gate_ref[:, r] = gate_out


def _post(xa, xb, o_attn, o_f, o_b, proj, dnw, wa, wd, wo, n2, rw, rb):
    T = xa.shape[0] + xb.shape[0]
    tm = min(POST_TM, xa.shape[0], xb.shape[0])
    assert xa.shape[0] % tm == 0 and xb.shape[0] % tm == 0
    npb = xa.shape[0] // tm
    zcol = P_DNZ // DN_V_W
    const = lambda shape: pl.BlockSpec(shape, lambda i: (0, 0))
    row = lambda width, col=0: pl.BlockSpec((tm, width), lambda i: (i, col))
    return pl.pallas_call(
        functools.partial(_post_kernel, npb=npb),
        grid=(T // tm,),
        in_specs=_group_row_specs(tm, D_MODEL, npb) + [
            row(ATTN_Q_W), row(DN_V_W), row(DN_V_W),
            row(DN_V_W, zcol), row(D_MODEL, P_GA // D_MODEL), row(D_MODEL, P_GD // D_MODEL),
            const((1, DN_V_DIM)), const((ATTN_Q_W, D_MODEL)), const((DN_V_W, D_MODEL)),
            const((D_MODEL, D_MODEL)), const((1, D_MODEL)), const((N_EXPERTS, D_MODEL)),
            const((N_EXPERTS, 1)),
        ],
        out_specs=[row(D_MODEL), pl.BlockSpec((tm * ROW_TILES, LANES), lambda i: (i, 0)),
                   pl.BlockSpec((SUBLANES, tm), lambda i: (0, i)),
                   pl.BlockSpec((SUBLANES, tm), lambda i: (0, i))],
        out_shape=[
            jax.ShapeDtypeStruct((T, D_MODEL), f32),
            jax.ShapeDtypeStruct((T * ROW_TILES, LANES), u32),
            jax.ShapeDtypeStruct((SUBLANES, T), i32),
            jax.ShapeDtypeStruct((SUBLANES, T), f32),
        ],
        compiler_params=_cparams(("parallel",)),
        name="post_mix",
    )(xa, xb, o_attn, o_f, o_b, proj, proj, proj, dnw, wa, wd, wo, n2, rw, rb)


def _scatter_kernel(poff_ref, pn_ref, pos_ref, h_ref, xs_ref, zero_ref, sem, zsem):
    R = pos_ref.shape[2] // TOP_K
    nbits = MOE_TM.bit_length() - 1

    @pl.when(pl.program_id(0) == 0)
    def _():
        zero_ref[...] = jnp.zeros(zero_ref.shape, u32)

        def per_expert(e, carry):
            off = poff_ref[e]
            n = pn_ref[e]
            for bit in reversed(range(nbits)):
                size = 1 << bit
                take = ((n >> bit) & 1) == 1

                @pl.when(take)
                def _():
                    cp = pltpu.make_async_copy(zero_ref.at[_row_span(0, size)],
                                               xs_ref.at[_row_span(off, size)], zsem)
                    cp.start()
                    cp.wait()

                off = off + jnp.where(take, size, 0)
            return carry

        lax.fori_loop(0, N_EXPERTS, per_expert, 0)

        zrows = zero_ref.shape[0] // ROW_TILES

        def tail(j, carry):
            row = poff_ref[N_EXPERTS] + j * zrows
            cp = pltpu.make_async_copy(zero_ref, xs_ref.at[_row_span(row, zrows)], zsem)
            cp.start()
            cp.wait()
            return carry

        lax.fori_loop(0, pn_ref[N_EXPERTS], tail, 0)

    def body(t, carry):
        for k in range(TOP_K):
            p = pos_ref[0, 0, t * TOP_K + k]
            pltpu.make_async_copy(h_ref.at[_row_span(t)], xs_ref.at[_row_span(p)], sem).start()
        return carry

    lax.fori_loop(0, R, body, 0, unroll=2)
    done = xs_ref.at[_row_span(0, TOP_K * R)]
    pltpu.make_async_copy(done, done, sem).wait()


def _scatter_rows(pad_off, pad_n, pos, hp, n_slots):
    T = hp.shape[0] // ROW_TILES
    R = min(SCATTER_R, T)
    grid_spec = pltpu.PrefetchScalarGridSpec(
        num_scalar_prefetch=2,
        grid=(T // R,),
        in_specs=[
            pl.BlockSpec((1, 1, TOP_K * R), lambda i, po, pn: (i, 0, 0), memory_space=pltpu.SMEM),
            pl.BlockSpec((R * ROW_TILES, LANES), lambda i, po, pn: (i, 0)),
        ],
        out_specs=pl.BlockSpec(memory_space=pl.ANY),
        scratch_shapes=[pltpu.VMEM((MOE_TM // 2 * ROW_TILES, LANES), u32),
                        pltpu.SemaphoreType.DMA(()), pltpu.SemaphoreType.DMA(())],
    )
    return pl.pallas_call(
        _scatter_kernel,
        grid_spec=grid_spec,
        out_shape=jax.ShapeDtypeStruct((n_slots * ROW_TILES, LANES), u32),
        compiler_params=_cparams(("arbitrary",)),
        name="moe_scatter",
    )(pad_off, pad_n, pos.reshape(T // R, 1, TOP_K * R), hp)


def _moe_kernel(te_ref, nu_ref, tr_ref, x_ref, wg_ref, wu_ref, bg_ref, bu_ref, wd_ref, bd_ref,
                o_ref, xb_ref, acc_ref, *, nf):
    i = pl.program_id(0)
    f = pl.program_id(1)
    half = D_MODEL // 2
    tm = xb_ref.shape[0]
    used = i < nu_ref[0]
    half_tile = tr_ref[i] <= tm // 2

    @pl.when((i == 0) & (f == 0))
    def _():
        acc_ref[...] = jnp.zeros(acc_ref.shape, f32)

    @pl.when(used & (f == 0))
    def _():
        for s in range(ROW_TILES):
            lo, hi = _unpack_halves(x_ref[_row_tile(0, xb_ref.shape[0], s), :])
            xb_ref[:, s * LANES:(s + 1) * LANES] = lo.astype(bf16)
            xb_ref[:, half + s * LANES:half + (s + 1) * LANES] = hi.astype(bf16)

    def compute(rows):
        xb = xb_ref[0:rows, :]
        g = jnp.dot(xb, wg_ref[0, 0], preferred_element_type=f32) + bg_ref[0]
        u = jnp.dot(xb, wu_ref[0, 0], preferred_element_type=f32) + bu_ref[0]
        gate = jnp.minimum(g, SWIGLU_LIMIT)
        up = jnp.clip(u, -SWIGLU_LIMIT, SWIGLU_LIMIT)
        act = (up + 1.0) * (gate * _sigmoid(SWIGLU_ALPHA * gate))
        part = jnp.dot(act.astype(bf16), wd_ref[0].astype(bf16), preferred_element_type=f32)

        if nf == 1:
            total = part
        else:
            total = jnp.where(f == 0, part, acc_ref[0:rows, :] + part)
            acc_ref[0:rows, :] = total
        y = total + bd_ref[0]
        _store_row_tiles(o_ref, _pack_halves(y[:, :half], y[:, half:]))

    @pl.when(used & jnp.logical_not(half_tile))
    def _():
        compute(tm)

    @pl.when(used & half_tile)
    def _():
        compute(tm // 2)
        rest = tm // 2 * ROW_TILES
        o_ref[rest:, :] = jnp.zeros((rest, LANES), u32)

    @pl.when(jnp.logical_not(used) & (f == nf - 1))
    def _():
        o_ref[...] = jnp.zeros(o_ref.shape, u32)


def _moe(tile_expert, n_used, tile_rows, xs, wg, wu, bg, bu, wd, bd):
    n_slots = xs.shape[0] // ROW_TILES
    tm, tf = MOE_TM, MOE_TF
    n_tiles = n_slots // tm
    nf = D_FF // tf

    def fidx(i, f, nu):
        return jnp.where(i < nu[0], f, nf - 1)

    grid_spec = pltpu.PrefetchScalarGridSpec(
        num_scalar_prefetch=3,
        grid=(n_tiles, nf),
        in_specs=[
            pl.BlockSpec((tm * ROW_TILES, LANES), lambda i, f, te, nu, tr: (i, 0)),
            pl.BlockSpec((1, 1, D_MODEL, tf), lambda i, f, te, nu, tr: (te[i], fidx(i, f, nu), 0, 0)),
            pl.BlockSpec((1, 1, D_MODEL, tf), lambda i, f, te, nu, tr: (te[i], fidx(i, f, nu), 0, 0)),
            pl.BlockSpec((1, 1, tf), lambda i, f, te, nu, tr: (te[i], 0, fidx(i, f, nu))),
            pl.BlockSpec((1, 1, tf), lambda i, f, te, nu, tr: (te[i], 0, fidx(i, f, nu))),
            pl.BlockSpec((1, tf, D_MODEL), lambda i, f, te, nu, tr: (te[i], fidx(i, f, nu), 0)),
            pl.BlockSpec((1, 1, D_MODEL), lambda i, f, te, nu, tr: (te[i], 0, 0)),
        ],
        out_specs=pl.BlockSpec((tm * ROW_TILES, LANES), lambda i, f, te, nu, tr: (i, 0)),
        scratch_shapes=[pltpu.VMEM((tm, D_MODEL), bf16), pltpu.VMEM((tm, D_MODEL), f32)],
    )
    return pl.pallas_call(
        functools.partial(_moe_kernel, nf=nf),
        grid_spec=grid_spec,
        out_shape=jax.ShapeDtypeStruct((n_slots * ROW_TILES, LANES), u32),
        compiler_params=_cparams(("arbitrary", "arbitrary")),
        name="moe_experts",
    )(tile_expert, n_used, tile_rows, xs, wg, wu, bg, bu, wd, bd)


def _combine_kernel(pos_ref, posn_ref, x2_ref, gate_ref, fw_ref, ys_ref, oa_ref, ob_ref,
                    buf_ref, acc_ref, sem, *, npb):
    R = x2_ref.shape[0]
    half = D_MODEL // 2
    nrows = TOP_K * R
    i = pl.program_id(0)
    slot = i % 2

    def issue(p_ref, dst_slot):
        def body(a, carry):
            p = p_ref[0, 0, a]
            pltpu.make_async_copy(ys_ref.at[_row_span(p)],
                                  buf_ref.at[_row_span(dst_slot * nrows + a)], sem.at[dst_slot]).start()
            return carry

        lax.fori_loop(0, nrows, body, 0, unroll=8)

    @pl.when(i == 0)
    def _():
        issue(pos_ref, 0)

    @pl.when(i + 1 < pl.num_programs(0))
    def _():
        issue(posn_ref, 1 - slot)

    pltpu.make_async_copy(ys_ref.at[_row_span(0, nrows)], buf_ref.at[_row_span(slot * nrows, nrows)],
                          sem.at[slot]).wait()
    row0 = slot * nrows

    gates = gate_ref[...]
    gk = [gates[:, k:k + 1] for k in range(TOP_K)]
    ss = jnp.zeros((R, 1), f32)
    for s in range(ROW_TILES):
        cl = slice(s * LANES, (s + 1) * LANES)
        ch = slice(half + s * LANES, half + (s + 1) * LANES)
        lo_acc = x2_ref[:, cl]
        hi_acc = x2_ref[:, ch]
        for k in range(TOP_K):
            lo, hi = _unpack_halves(buf_ref[_row_tile(row0 + k * R, R, s), :])
            lo_acc = lo_acc + gk[k] * lo
            hi_acc = hi_acc + gk[k] * hi
        ss = ss + jnp.sum(lo_acc * lo_acc, axis=-1, keepdims=True) + jnp.sum(
            hi_acc * hi_acc, axis=-1, keepdims=True)
        acc_ref[:, cl] = lo_acc
        acc_ref[:, ch] = hi_acc
    inv = lax.rsqrt(ss / D_MODEL + NORM_EPS)

    @pl.when(i < npb)
    def _():
        oa_ref[...] = acc_ref[...] * inv * fw_ref[...]

    @pl.when(i >= npb)
    def _():
        ob_ref[...] = acc_ref[...] * inv * fw_ref[...]


def _combine(pos, x2, gates, fw, ys, rows_a):
    T = x2.shape[0]
    R = min(COMB_R, rows_a, T - rows_a)
    assert rows_a % R == 0 and T % R == 0
    npb = rows_a // R
    nsteps = T // R
    return pl.pallas_call(
        functools.partial(_combine_kernel, npb=npb),
        grid=(nsteps,),
        in_specs=[
            pl.BlockSpec((1, 1, TOP_K * R), lambda i: (i, 0, 0), memory_space=pltpu.SMEM),
            pl.BlockSpec((1, 1, TOP_K * R), lambda i: (jnp.minimum(i + 1, nsteps - 1), 0, 0),
                         memory_space=pltpu.SMEM),
            pl.BlockSpec((R, D_MODEL), lambda i: (i, 0)),
            pl.BlockSpec((R, TOP_K), lambda i: (i, 0)),
            pl.BlockSpec((1, D_MODEL), lambda i: (0, 0)),
            pl.BlockSpec(memory_space=pl.ANY),
        ],
        out_specs=_group_row_specs(R, D_MODEL, npb),
        out_shape=[jax.ShapeDtypeStruct((rows_a, D_MODEL), f32),
                   jax.ShapeDtypeStruct((T - rows_a, D_MODEL), f32)],
        scratch_shapes=[pltpu.VMEM((2 * TOP_K * R * ROW_TILES, LANES), u32),
                        pltpu.VMEM((R, D_MODEL), f32),
                        pltpu.SemaphoreType.DMA((2,))],
        compiler_params=_cparams(("arbitrary",)),
        name="moe_combine",
    )(pos, pos, x2, gates, fw, ys)


def _dn_channels(t):
    lead = t.shape[:-1]
    t = t.reshape(lead + (2, 2, DN_QK_HEADS, 2))
    perm = tuple(range(len(lead))) + tuple(len(lead) + a for a in (2, 0, 1, 3))
    return t.transpose(perm).reshape(lead + (4 * DN_V_HEADS,))


def _pad_lanes(t):
    pad = [(0, 0)] * (t.ndim - 1) + [(0, LANES - t.shape[-1])]
    return jnp.pad(t, pad)


def _deint_kernel(w_ref, p_ref, g_ref, u_ref):
    for c in range(w_ref.shape[2] // (2 * LANES)):
        w = w_ref[0, :, c * 2 * LANES:(c + 1) * 2 * LANES].astype(bf16)
        r = jnp.dot(w, p_ref[...], preferred_element_type=f32)
        g_ref[0, 0, :, c * LANES:(c + 1) * LANES] = r[:, :LANES].astype(bf16)
        u_ref[0, 0, :, c * LANES:(c + 1) * LANES] = r[:, LANES:].astype(bf16)


def _deinterleave(w):
    E, D, F2 = w.shape
    rb, cb = 1024, 2 * MOE_TF
    i = jnp.arange(2 * LANES)
    src = jnp.where(i < LANES, 2 * i, 2 * (i - LANES) + 1)
    perm = (jnp.arange(2 * LANES)[:, None] == src[None, :]).astype(bf16)
    out = jax.ShapeDtypeStruct((E, F2 // cb, D, MOE_TF), bf16)
    return pl.pallas_call(
        _deint_kernel,
        grid=(E, D // rb, F2 // cb),
        in_specs=[pl.BlockSpec((1, rb, cb), lambda e, r, c: (e, r, c)),
                  pl.BlockSpec((2 * LANES, 2 * LANES), lambda e, r, c: (0, 0))],
        out_specs=[pl.BlockSpec((1, 1, rb, MOE_TF), lambda e, r, c: (e, c, r, 0))] * 2,
        out_shape=[out, out],
        compiler_params=_cparams(("parallel", "parallel", "parallel")),
        name="moe_deinterleave",
    )(w, perm)


def _routing(top_idx, T, tm):
    A = T * TOP_K
    n_tiles = A // tm + N_EXPERTS
    n_slots = n_tiles * tm
    flat_e = top_idx.reshape(-1)
    onehot = (flat_e[:, None] == jnp.arange(N_EXPERTS, dtype=i32)[None, :]).astype(i32)
    csum = jnp.cumsum(onehot, axis=0)
    counts = csum[-1]
    padded = ((counts + tm - 1) // tm) * tm
    padded_end = jnp.cumsum(padded)
    padded_start = padded_end - padded
    pos = jnp.sum(onehot * (csum - 1 + padded_start[None, :]), axis=1).astype(i32)
    tile_start = jnp.arange(n_tiles, dtype=i32) * tm
    tile_expert = jnp.minimum(
        jnp.sum(padded_end[None, :] <= tile_start[:, None], axis=1), N_EXPERTS - 1).astype(i32)
    n_used = (padded_end[-1] // tm).astype(i32).reshape(1)
    tile_rows = jnp.clip(counts[tile_expert] - (tile_start - padded_start[tile_expert]), 0, tm).astype(i32)
    pad_off = jnp.concatenate([padded_start + counts, padded_end[-1:]]).astype(i32)
    pad_n = jnp.concatenate([padded - counts, (n_slots - padded_end[-1:]) // (tm // 2)]).astype(i32)
    return pos, pad_off, pad_n, tile_expert, n_used, tile_rows, n_slots


def _layer(xa, xb, seq, norm1_w, w_in, b_in, attn_sinks, dn_conv_w, dn_a_log, dn_dt_bias,
           dn_norm_w, w_attn_o, w_dn_o, w_out, norm2_w, router_w, router_b, w_gate_up, b_gate_up,
           w_down, b_down, final_norm_w):
    T = xa.shape[0] + xb.shape[0]
    nseq = T // seq

    order = ([_OFF_GA, D_MODEL], [_OFF_GD, D_MODEL], [_OFF_DNQKV, DN_CONV_W], [_OFF_AQ, ATTN_Q_W],
             [_OFF_DNZ, DN_V_W], [_OFF_AK, ATTN_KV_W], [_OFF_AV, ATTN_KV_W])
    w_main = jnp.concatenate([w_in[:, o:o + n] for o, n in order], axis=1).astype(bf16)
    w_main = w_main.reshape(D_MODEL, P_W // IN_TN, IN_TN).transpose(1, 0, 2)
    b_main = jnp.concatenate([b_in[o:o + n] for o, n in order])[None, :]
    nsm = 4 * DN_V_HEADS
    w_small = _pad_lanes(_dn_channels(w_in[:, _OFF_DNB:_OFF_DNB + nsm])).astype(bf16)
    b_small = _pad_lanes(_dn_channels(b_in[_OFF_DNB:_OFF_DNB + nsm]))
    zeros_b = jnp.zeros((2 * DN_V_HEADS,), f32)
    a_ch = _pad_lanes(_dn_channels(jnp.concatenate([zeros_b, dn_a_log.reshape(-1)])))
    d_ch = _pad_lanes(_dn_channels(jnp.concatenate([zeros_b, dn_dt_bias.reshape(-1)])))

    proj, small, small_t = _inproj(xa, xb, norm1_w[None, :], w_main, b_main, w_small, b_small[None, :],
                                   w_small.T, b_small[:, None])

    pos = jnp.arange(seq, dtype=f32)
    inv_freq = 1.0 / (ROPE_THETA ** (jnp.arange(0, ROPE_DIM, 2, dtype=f32) / ROPE_DIM))
    ang = pos[:, None] * inv_freq[None, :]
    cos, sin = jnp.cos(ang), jnp.sin(ang)
    half = ROPE_DIM // 2
    ones_tail = jnp.ones((seq, ATTN_HEAD_DIM - ROPE_DIM), f32)
    cos_head = jnp.concatenate([cos, cos, ones_tail], axis=1)
    sin_head = jnp.concatenate([-sin, sin, 0.0 * ones_tail], axis=1)
    cos_t = jnp.concatenate([cos_head, cos_head], axis=1)
    sin_t = jnp.concatenate([sin_head, sin_head], axis=1)

    o_attn = _attention(proj, attn_sinks, cos_t, sin_t, nseq, seq)

    qkv = _dnconv(proj, dn_conv_w, nseq, seq)
    u, w, qd, kd, aqk, egl = _dnprep(qkv, small, small_t, a_ch[None, :], d_ch[None, :],
                                     a_ch[:, None], d_ch[:, None])
    o_f, o_b = _dnscan(u, w, qd, kd, aqk, egl, nseq, seq)

    x2, hp, top_idx_t, top_gate_t = _post(
        xa, xb, o_attn, o_f, o_b, proj, dn_norm_w[None, :], w_attn_o.astype(bf16), w_dn_o.astype(bf16),
        w_out.astype(bf16), norm2_w[None, :], router_w.T, router_b[:, None])
    top_idx = top_idx_t[:TOP_K].T
    top_gate = top_gate_t[:TOP_K].T

    pos_a, pad_off, pad_n, tile_expert, n_used, tile_rows, n_slots = _routing(top_idx, T, MOE_TM)
    xs = _scatter_rows(pad_off, pad_n, pos_a, hp, n_slots)
    wg, wu = _deinterleave(w_gate_up)
    b_pairs = b_gate_up.reshape(N_EXPERTS, 1, D_FF, 2)
    bg, bu = b_pairs[..., 0], b_pairs[..., 1]
    ys = _moe(tile_expert, n_used, tile_rows, xs, wg, wu, bg, bu, w_down, b_down[:, None, :])
    R = min(COMB_R, xa.shape[0], xb.shape[0])
    pos_tiles = pos_a.reshape(T // R, R, TOP_K).transpose(0, 2, 1).reshape(T // R, 1, TOP_K * R)
    return _combine(pos_tiles, x2, top_gate, final_norm_w[None, :], ys, xa.shape[0])


def kernel(x_prompt, x_sample, norm1_w, w_in, b_in, attn_sinks, dn_conv_w, dn_a_log, dn_dt_bias,
           dn_norm_w, w_attn_o, w_dn_o, w_out, norm2_w, router_w, router_b, w_gate_up, b_gate_up,
           w_down, b_down, final_norm_w):
    bp, seq, d = x_prompt.shape
    bs = x_sample.shape[0]
    assert x_sample.shape[1] == seq and d == D_MODEL
    ya, yb = _layer(x_prompt.reshape(bp * seq, d), x_sample.reshape(bs * seq, d), seq,
                    norm1_w[0], w_in[0], b_in[0], attn_sinks[0], dn_conv_w[0], dn_a_log[0],
                    dn_dt_bias[0], dn_norm_w[0], w_attn_o[0], w_dn_o[0], w_out[0], norm2_w[0],
                    router_w[0], router_b[0], w_gate_up[0], b_gate_up[0], w_down[0], b_down[0],
                    final_norm_w)
    return (ya.reshape(bp, seq, d), yb.reshape(bs, seq, d))
```

```python
import functools

import jax
import jax.numpy as jnp
from jax import lax
from jax.experimental import pallas as pl
from jax.experimental.pallas import tpu as pltpu

f32 = jnp.float32
bf16 = jnp.bfloat16
u32 = jnp.uint32
i32 = jnp.int32

D_MODEL = 2048
ATTN_HEADS = 16
ATTN_KV_HEADS = 4
ATTN_HEAD_DIM = 64
ATTN_BLOCK = 128
ROPE_THETA = 500000.0
ROPE_DIM = ATTN_HEAD_DIM // 4
DN_QK_HEADS = 4
DN_V_HEADS = 8
DN_K_DIM = 128
DN_V_DIM = 128
DN_CONV = 5
DN_CHUNK = 64
N_EXPERTS = 32
TOP_K = 4
D_FF = D_MODEL
SWIGLU_LIMIT = 7.0
SWIGLU_ALPHA = 1.702
NORM_EPS = 1e-6
NEG_INF = -1e30

ATTN_Q_W = ATTN_HEADS * ATTN_HEAD_DIM
ATTN_KV_W = ATTN_KV_HEADS * ATTN_HEAD_DIM
DN_QK_W = DN_QK_HEADS * DN_K_DIM
DN_V_W = DN_V_HEADS * DN_V_DIM
DN_CONV_W = 2 * DN_QK_W + DN_V_W

_OFF_AQ = 0
_OFF_AK = _OFF_AQ + ATTN_Q_W
_OFF_AV = _OFF_AK + ATTN_KV_W
_OFF_DNQKV = _OFF_AV + ATTN_KV_W
_OFF_DNZ = _OFF_DNQKV + DN_CONV_W
_OFF_DNB = _OFF_DNZ + DN_V_W
_OFF_DNA = _OFF_DNB + 2 * DN_V_HEADS
_OFF_GA = _OFF_DNA + 2 * DN_V_HEADS
_OFF_GD = _OFF_GA + D_MODEL

P_GA = 0
P_GD = P_GA + D_MODEL
P_DNQKV = P_GD + D_MODEL
P_AQ = P_DNQKV + DN_CONV_W
P_DNZ = P_AQ + ATTN_Q_W
P_AK = P_DNZ + DN_V_W
P_AV = P_AK + ATTN_KV_W
P_W = P_AV + ATTN_KV_W

LANES = 128
SUBLANES = 8
DN_GROUP = 8
VMEM_LIMIT = 56 * 1024 * 1024

IN_TM = 1024
IN_TN = 512
DN_TB = 256
DN_PREP_TB = 512
POST_TM = 256
POST_SUB = 128
MOE_TM = 512
MOE_TF = 1024
SCATTER_R = 1024
COMB_R = 256


def _cparams(sem):
    return pltpu.CompilerParams(dimension_semantics=sem, vmem_limit_bytes=VMEM_LIMIT)


def _sigmoid(x):
    return 1.0 / (1.0 + jnp.exp(-x))


def _softplus(x):
    return jnp.maximum(x, 0.0) + jnp.log(1.0 + jnp.exp(-jnp.abs(x)))


def _pack_halves(lo, hi):
    a = lax.bitcast_convert_type(lo.astype(bf16).astype(f32), u32)
    b = lax.bitcast_convert_type(hi.astype(bf16).astype(f32), u32)
    return (a >> 16) | (b & jnp.uint32(0xFFFF0000))


def _unpack_halves(w):
    lo = lax.bitcast_convert_type(w << 16, f32)
    hi = lax.bitcast_convert_type(w & jnp.uint32(0xFFFF0000), f32)
    return lo, hi


ROW_TILES = D_MODEL // 2 // LANES


def _row_tile(r0, n, s):
    start = r0 * ROW_TILES
    if not isinstance(r0, int):
        start = pl.multiple_of(start, ROW_TILES)
    return pl.ds(start + s, n, stride=ROW_TILES)


def _row_span(r, n=1):
    start = r * ROW_TILES
    if not isinstance(r, int):
        start = pl.multiple_of(start, ROW_TILES)
    return pl.ds(start, n * ROW_TILES)


def _store_row_tiles(ref, packed):
    for s in range(ROW_TILES):
        ref[_row_tile(0, packed.shape[0], s), :] = packed[:, s * LANES:(s + 1) * LANES]


def _group_row_specs(tm, width, npb):
    first = pl.BlockSpec((tm, width), lambda i, *_: (jnp.minimum(i, npb - 1), 0))
    second = pl.BlockSpec((tm, width), lambda i, *_: (jnp.maximum(i - npb, 0), 0))
    return [first, second]


def _inproj_kernel(xa_ref, xb_ref, nw_ref, w_ref, b_ref, ws_ref, bs_ref, wst_ref, bst_ref,
                   o_ref, os_ref, ost_ref, h_ref, *, npb):
    @pl.when(pl.program_id(1) == 0)
    def _():
        x = jnp.where(pl.program_id(0) < npb, xa_ref[...], xb_ref[...])
        ms = jnp.mean(x * x, axis=-1, keepdims=True)
        hb = (x * lax.rsqrt(ms + NORM_EPS) * nw_ref[...]).astype(bf16)
        h_ref[...] = hb
        os_ref[...] = jnp.dot(hb, ws_ref[...], preferred_element_type=f32) + bs_ref[...]
        ost_ref[...] = lax.dot_general(wst_ref[...], hb, (((1,), (1,)), ((), ())),
                                       preferred_element_type=f32) + bst_ref[...]

    acc = jnp.dot(h_ref[...], w_ref[0], preferred_element_type=f32)
    o_ref[...] = (acc + b_ref[...]).astype(bf16)


def _inproj(xa, xb, nw, w, b, ws, bs, wst, bst):
    T = xa.shape[0] + xb.shape[0]
    tm = min(IN_TM, xa.shape[0], xb.shape[0])
    assert xa.shape[0] % tm == 0 and xb.shape[0] % tm == 0
    npb = xa.shape[0] // tm
    grid = (T // tm, P_W // IN_TN)
    return pl.pallas_call(
        functools.partial(_inproj_kernel, npb=npb),
        grid=grid,
        in_specs=_group_row_specs(tm, D_MODEL, npb) + [
            pl.BlockSpec((1, D_MODEL), lambda i, j: (0, 0)),
            pl.BlockSpec((1, D_MODEL, IN_TN), lambda i, j: (j, 0, 0)),
            pl.BlockSpec((1, IN_TN), lambda i, j: (0, j)),
            pl.BlockSpec((D_MODEL, LANES), lambda i, j: (0, 0)),
            pl.BlockSpec((1, LANES), lambda i, j: (0, 0)),
            pl.BlockSpec((LANES, D_MODEL), lambda i, j: (0, 0)),
            pl.BlockSpec((LANES, 1), lambda i, j: (0, 0)),
        ],
        out_specs=[
            pl.BlockSpec((tm, IN_TN), lambda i, j: (i, j)),
            pl.BlockSpec((tm, LANES), lambda i, j: (i, 0)),
            pl.BlockSpec((LANES, tm), lambda i, j: (0, i)),
        ],
        out_shape=[
            jax.ShapeDtypeStruct((T, P_W), bf16),
            jax.ShapeDtypeStruct((T, LANES), f32),
            jax.ShapeDtypeStruct((LANES, T), f32),
        ],
        scratch_shapes=[pltpu.VMEM((tm, D_MODEL), bf16)],
        compiler_params=_cparams(("parallel", "arbitrary")),
        name="inproj",
    )(xa, xb, nw, w, b, ws, bs, wst, bst)


def _rope(x, cos, sin):
    lane = lax.broadcasted_iota(i32, x.shape, 1) % ATTN_HEAD_DIM
    half = ROPE_DIM // 2
    partner = jnp.where((lane >= half) & (lane < ROPE_DIM),
                        pltpu.roll(x, half, 1), pltpu.roll(x, LANES - half, 1))
    return x * cos + partner * sin


def _attn_kernel(sink_ref, q_ref, kp_ref, kc_ref, kn_ref, vp_ref, vc_ref, vn_ref,
                 cos_ref, sin_ref, o_ref, *, nb):
    n = pl.program_id(1)
    W = ATTN_BLOCK
    HD = ATTN_HEAD_DIM
    G = ATTN_HEADS // ATTN_KV_HEADS

    q0 = pl.multiple_of(n * W, W)
    cq, sq = cos_ref[pl.ds(q0, W), :], sin_ref[pl.ds(q0, W), :]
    qs = []
    for c in range(ATTN_Q_W // LANES):
        xq = q_ref[:, c * LANES:(c + 1) * LANES].astype(f32)
        qs.append((_rope(xq, cq, sq) * (HD ** -0.5)).astype(bf16))

    kblocks = (kp_ref, kc_ref, kn_ref)
    ks = [[], []]
    for bi in range(3):
        p0 = pl.multiple_of(jnp.clip(n + bi - 1, 0, nb - 1) * W, W)
        ck, sk = cos_ref[pl.ds(p0, W), :], sin_ref[pl.ds(p0, W), :]
        for c in range(ATTN_KV_W // LANES):
            xk = kblocks[bi][:, c * LANES:(c + 1) * LANES].astype(f32)
            ks[c].append(_rope(xk, ck, sk).astype(bf16))
    kcat = [jnp.concatenate(ks[c], axis=0) for c in range(2)]
    vcat = jnp.concatenate([vp_ref[...], vc_ref[...], vn_ref[...]], axis=0)

    r = lax.broadcasted_iota(i32, (W, 3 * W), 0)
    c = lax.broadcasted_iota(i32, (W, 3 * W), 1)
    valid = (c >= r) & (c <= r + 2 * W)
    valid = valid & ((n > 0) | (c >= W)) & ((n < nb - 1) | (c < 2 * W))
    valid4 = jnp.concatenate([valid] * G, axis=0)

    scores, sinks = [], []
    for kh in range(ATTN_KV_HEADS):
        kc_, ko = divmod(kh * HD, LANES)
        k_h = kcat[kc_][:, ko:ko + HD]
        parts = []
        sink_parts = []
        for g in range(G):
            h = kh * G + g
            qc, qo = divmod(h * HD, LANES)
            parts.append(qs[qc][:, qo:qo + HD])
            sink_parts.append(jnp.full((W, 1), sink_ref[h], f32))
        qg = jnp.concatenate(parts, axis=0)
        sinks.append(jnp.concatenate(sink_parts, axis=0))
        scores.append(lax.dot_general(qg, k_h, (((1,), (1,)), ((), ())), preferred_element_type=f32))
    probs, rdenoms = [], []
    for s, sink in zip(scores, sinks):
        s = jnp.where(valid4, s, NEG_INF)
        m = jnp.maximum(jnp.max(s, axis=-1, keepdims=True), sink)
        p = jnp.exp(s - m)
        rdenoms.append(1.0 / (jnp.sum(p, axis=-1, keepdims=True) + jnp.exp(sink - m)))
        probs.append(p.astype(bf16))
    outs = [jnp.dot(p, vcat[:, kh * HD:(kh + 1) * HD], preferred_element_type=f32)
            for kh, p in enumerate(probs)]
    for kh in range(ATTN_KV_HEADS):
        o = outs[kh] * rdenoms[kh]
        for g in range(G):
            h = kh * G + g
            o_ref[:, h * HD:(h + 1) * HD] = o[g * W:(g + 1) * W].astype(bf16)


def _attention(proj, sinks, cos_t, sin_t, nseq, seq):
    W = ATTN_BLOCK
    nb = seq // W
    T = nseq * seq
    qcol = P_AQ // ATTN_Q_W
    kcol = P_AK // ATTN_KV_W
    vcol = P_AV // ATTN_KV_W

    def kv_spec(col, d):
        return pl.BlockSpec(
            (W, ATTN_KV_W), lambda b, n: (b * nb + jnp.clip(n + d, 0, nb - 1), col))

    return pl.pallas_call(
        functools.partial(_attn_kernel, nb=nb),
        grid=(nseq, nb),
        in_specs=[
            pl.BlockSpec(memory_space=pltpu.SMEM),
            pl.BlockSpec((W, ATTN_Q_W), lambda b, n: (b * nb + n, qcol)),
            kv_spec(kcol, -1), kv_spec(kcol, 0), kv_spec(kcol, 1),
            kv_spec(vcol, -1), kv_spec(vcol, 0), kv_spec(vcol, 1),
            pl.BlockSpec(memory_space=pltpu.VMEM),
            pl.BlockSpec(memory_space=pltpu.VMEM),
        ],
        out_specs=pl.BlockSpec((W, ATTN_Q_W), lambda b, n: (b * nb + n, 0)),
        out_shape=jax.ShapeDtypeStruct((T, ATTN_Q_W), bf16),
        compiler_params=_cparams(("parallel", "parallel")),
        name="window_attn",
    )(sinks, proj, proj, proj, proj, proj, proj, proj, cos_t, sin_t)


def _dnconv_kernel(x_ref, w_ref, o_ref, pad_ref, *, seq, rt):
    j = pl.program_id(1)
    half = DN_CONV // 2
    cw = x_ref.shape[1]
    pad_ref[0:SUBLANES, :] = jnp.zeros((SUBLANES, cw), f32)
    pad_ref[seq + SUBLANES:seq + 2 * SUBLANES, :] = jnp.zeros((SUBLANES, cw), f32)
    pad_ref[SUBLANES:seq + SUBLANES, :] = x_ref[...].astype(f32)
    w = w_ref[...]
    qk_blocks = 2 * DN_QK_W // cw
    q_blocks = DN_QK_W // cw
    for r0 in range(0, seq, rt):
        acc = jnp.zeros((rt, cw), f32)
        for t in range(DN_CONV):
            s0 = r0 + SUBLANES - half + t
            acc = acc + pad_ref[s0:s0 + rt, :] * w[t:t + 1, :]
        y = acc * _sigmoid(acc)
        outs = []
        for hh in range(cw // DN_K_DIM):
            seg = y[:, hh * DN_K_DIM:(hh + 1) * DN_K_DIM]
            inv = lax.rsqrt(jnp.sum(seg * seg, axis=-1, keepdims=True) + NORM_EPS)
            fac = jnp.where(j < q_blocks, inv * (DN_K_DIM ** -0.5),
                            jnp.where(j < qk_blocks, inv, jnp.ones_like(inv)))
            outs.append(seg * fac)
        o_ref[r0:r0 + rt, :] = jnp.concatenate(outs, axis=1).astype(bf16)


def _dnconv(proj, conv_w, nseq, seq):
    cw = 256
    T = nseq * seq
    col0 = P_DNQKV // cw
    rt = min(512, seq)
    return pl.pallas_call(
        functools.partial(_dnconv_kernel, seq=seq, rt=rt),
        grid=(nseq, DN_CONV_W // cw),
        in_specs=[
            pl.BlockSpec((seq, cw), lambda b, j: (b, col0 + j)),
            pl.BlockSpec((DN_CONV, cw), lambda b, j: (0, j)),
        ],
        out_specs=pl.BlockSpec((seq, cw), lambda b, j: (b, j)),
        out_shape=jax.ShapeDtypeStruct((T, DN_CONV_W), bf16),
        scratch_shapes=[pltpu.VMEM((seq + 2 * SUBLANES, cw), f32)],
        compiler_params=_cparams(("parallel", "parallel")),
        name="dn_conv",
    )(proj, conv_w)


def _mm(a, b):
    return jnp.dot(a.astype(bf16), b.astype(bf16), preferred_element_type=f32)


def _unit_tri_inverse_all(lmats, eye, blk_i, blk_j):
    same8 = (blk_i // 8) == (blk_j // 8)
    n1 = [jnp.where(same8, -l, 0.0) for l in lmats]
    p = [eye + n for n in n1]
    n2 = [_mm(n, n) for n in n1]
    p = [pp + _mm(pp, nn) for pp, nn in zip(p, n2)]
    n4 = [_mm(n, n) for n in n2]
    inv = [pp + _mm(pp, nn) for pp, nn in zip(p, n4)]
    for s in (8, 16, 32):
        coupling = ((blk_i // (2 * s)) == (blk_j // (2 * s))) & ((blk_i // s) != (blk_j // s))
        t = [_mm(jnp.where(coupling, l, 0.0), iv) for l, iv in zip(lmats, inv)]
        inv = [iv - _mm(iv, tt) for iv, tt in zip(inv, t)]
    return inv


def _chunk_cumsum(x, axis, reverse):
    n = x.shape[axis]
    pos = lax.broadcasted_iota(i32, x.shape, axis) % DN_CHUNK
    shift = 1
    while shift < DN_CHUNK:
        if reverse:
            moved = pltpu.roll(x, n - shift, axis)
            x = x + jnp.where(pos < DN_CHUNK - shift, moved, 0.0)
        else:
            moved = pltpu.roll(x, shift, axis)
            x = x + jnp.where(pos >= shift, moved, 0.0)
        shift *= 2
    return x


def _dnprep_kernel(q_ref, k_ref, v_ref, sm_ref, smt_ref, ar_ref, dr_ref, ac_ref, dc_ref,
                   u_ref, w_ref, qd_ref, kd_ref, aqk_ref, egl_ref):
    hq = pl.program_id(1)
    C = DN_CHUNK
    P = 2 * C
    TB = q_ref.shape[0]
    nch = TB // C

    sm = sm_ref[...]
    beta_full = _sigmoid(sm)
    g_full = -jnp.exp(ar_ref[...]) * _softplus(sm + dr_ref[...])
    shift = (LANES - hq * DN_GROUP) % LANES
    beta_c = pltpu.roll(beta_full, shift, 1)
    g_c = pltpu.roll(g_full, shift, 1)
    smt = smt_ref[...]
    g_r = -jnp.exp(ac_ref[...]) * _softplus(smt + dc_ref[...])

    bi = lax.broadcasted_iota(i32, (P, P), 0)
    bj = lax.broadcasted_iota(i32, (P, P), 1)
    eye = (bi == bj).astype(f32)
    same_head = (bi // C) == (bj // C)

    gcum_c = [_chunk_cumsum(g_c, 0, d == 1) for d in range(2)]
    gcum_r = [_chunk_cumsum(g_r, 1, d == 1) for d in range(2)]

    k2, q2, kk2, qk2, v2 = [], [], [], [], []
    for c in range(nch):
        rows = slice(c * C, (c + 1) * C)
        k = k_ref[rows, :]
        q = q_ref[rows, :]
        kp = jnp.concatenate([k, k], axis=0)
        qp = jnp.concatenate([q, q], axis=0)
        k2.append(kp.astype(f32))
        q2.append(qp.astype(f32))
        kk2.append(lax.dot_general(kp, kp, (((1,), (1,)), ((), ())), preferred_element_type=f32))
        qk2.append(lax.dot_general(qp, kp, (((1,), (1,)), ((), ())), preferred_element_type=f32))
        v2.append(jnp.concatenate([v_ref[rows, :DN_V_DIM], v_ref[rows, DN_V_DIM:]], axis=0).astype(f32))

    inst = [(d, c) for d in range(2) for c in range(nch)]
    gcol, bcol, decay, lmats = [], [], [], []
    for d, c in inst:
        rows = slice(c * C, (c + 1) * C)
        b0, g0 = d * 2, 4 + d * 2
        gc = jnp.concatenate([gcum_c[d][rows, g0:g0 + 1], gcum_c[d][rows, g0 + 1:g0 + 2]], axis=0)
        gr = jnp.concatenate([gcum_r[d][g0:g0 + 1, rows], gcum_r[d][g0 + 1:g0 + 2, rows]], axis=1)
        bc = jnp.concatenate([beta_c[rows, b0:b0 + 1], beta_c[rows, b0 + 1:b0 + 2]], axis=0)
        incl = same_head & ((bj <= bi) if d == 0 else (bj >= bi))
        strict = same_head & ((bj < bi) if d == 0 else (bj > bi))
        dec = jnp.where(incl, jnp.exp(jnp.where(incl, gc - gr, 0.0)), 0.0)
        gcol.append(gc)
        bcol.append(bc)
        decay.append(dec)
        lmats.append(jnp.where(strict, kk2[c] * bc * dec, 0.0))

    invs = _unit_tri_inverse_all(lmats, eye, bi, bj)

    for n, (d, c) in enumerate(inst):
        rows = slice(c * C, (c + 1) * C)
        last = C - 1 if d == 0 else 0
        gc, bc = gcol[n], bcol[n]
        eg = jnp.exp(gc)
        rhs = jnp.concatenate([v2[c] * bc, k2[c] * (bc * eg)], axis=1)
        sol = _mm(invs[n], rhs)
        gl0 = gc[last:last + 1, :]
        gl1 = gc[C + last:C + last + 1, :]
        glast = jnp.concatenate([jnp.broadcast_to(gl0, (C, 1)), jnp.broadcast_to(gl1, (C, 1))], axis=0)
        qd = q2[c] * eg
        kd = k2[c] * jnp.exp(glast - gc)
        aqk = qk2[c] * decay[n]
        for hvl in range(2):
            pr = slice(hvl * C, (hvl + 1) * C)
            cols = slice(hvl * DN_V_DIM, (hvl + 1) * DN_V_DIM)
            u_ref[d, rows, cols] = sol[pr, :DN_V_DIM].astype(bf16)
            w_ref[d, rows, cols] = sol[pr, DN_V_DIM:].astype(bf16)
            qd_ref[d, rows, cols] = qd[pr].astype(bf16)
            kd_ref[d, rows, cols] = kd[pr].astype(bf16)
            aqk_ref[d, rows, pr] = aqk[pr, pr].astype(bf16)
        egl_ref[d, 0, c, 0:1, :] = jnp.broadcast_to(jnp.exp(gl0), (1, LANES))
        egl_ref[d, 0, c, 1:2, :] = jnp.broadcast_to(jnp.exp(gl1), (1, LANES))


def _dnprep(qkv, small, small_t, ar, dr, ac, dc):
    T = qkv.shape[0]
    TB = DN_PREP_TB
    nblk = T // TB
    nch = TB // DN_CHUNK
    kcol = DN_QK_W // DN_K_DIM
    vcol = 2 * DN_QK_W // (2 * DN_V_DIM)
    wide = lambda: pl.BlockSpec((2, TB, 2 * DN_V_DIM), lambda i, h: (0, i, h))
    big = jax.ShapeDtypeStruct((2, T, DN_V_W), bf16)
    return pl.pallas_call(
        _dnprep_kernel,
        grid=(nblk, DN_QK_HEADS),
        in_specs=[
            pl.BlockSpec((TB, DN_K_DIM), lambda i, h: (i, h)),
            pl.BlockSpec((TB, DN_K_DIM), lambda i, h: (i, kcol + h)),
            pl.BlockSpec((TB, 2 * DN_V_DIM), lambda i, h: (i, vcol + h)),
            pl.BlockSpec((TB, LANES), lambda i, h: (i, 0)),
            pl.BlockSpec((DN_GROUP, TB), lambda i, h: (h, i)),
            pl.BlockSpec((1, LANES), lambda i, h: (0, 0)),
            pl.BlockSpec((1, LANES), lambda i, h: (0, 0)),
            pl.BlockSpec((DN_GROUP, 1), lambda i, h: (h, 0)),
            pl.BlockSpec((DN_GROUP, 1), lambda i, h: (h, 0)),
        ],
        out_specs=[
            wide(), wide(), wide(), wide(),
            pl.BlockSpec((2, TB, 2 * DN_CHUNK), lambda i, h: (0, i, h)),
            pl.BlockSpec((2, 1, nch, 2, LANES), lambda i, h: (0, h, i, 0, 0)),
        ],
        out_shape=[
            big, big, big, big,
            jax.ShapeDtypeStruct((2, T, DN_QK_HEADS * 2 * DN_CHUNK), bf16),
            jax.ShapeDtypeStruct((2, DN_QK_HEADS, T // DN_CHUNK, 2, LANES), f32),
        ],
        compiler_params=_cparams(("parallel", "parallel")),
        name="dn_prep",
    )(qkv, qkv, qkv, small, small_t, ar, dr, ac, dc)


def _dnscan_kernel(uf, wf, qf, kf, af, ef, ub, wb, qb, kb, ab, eb, of_ref, ob_ref, s_ref):
    t = pl.program_id(1)
    C = DN_CHUNK
    nch = uf.shape[1] // C

    @pl.when(t == 0)
    def _():
        s_ref[...] = jnp.zeros(s_ref.shape, f32)

    dirs = ((uf, wf, qf, kf, af, ef, of_ref), (ub, wb, qb, kb, ab, eb, ob_ref))
    chains = [(d, hv) for d in range(2) for hv in range(DN_V_HEADS)]
    for step in range(nch):
        def rows(d):
            c = step if d == 0 else nch - 1 - step
            return c, slice(c * C, (c + 1) * C)

        def cols(hv):
            return slice(hv * DN_V_DIM, (hv + 1) * DN_V_DIM)

        s_old = [s_ref[d * DN_V_HEADS + hv] for d, hv in chains]
        s_bf = [s.astype(bf16) for s in s_old]
        w_s = [jnp.dot(dirs[d][1][0, rows(d)[1], cols(hv)], sb, preferred_element_type=f32)
               for (d, hv), sb in zip(chains, s_bf)]
        q_s = [jnp.dot(dirs[d][2][0, rows(d)[1], cols(hv)], sb, preferred_element_type=f32)
               for (d, hv), sb in zip(chains, s_bf)]
        v_bf = [(dirs[d][0][0, rows(d)[1], cols(hv)].astype(f32) - ws).astype(bf16)
                for (d, hv), ws in zip(chains, w_s)]
        a_v = [jnp.dot(dirs[d][4][0, rows(d)[1], hv * C:(hv + 1) * C], vb, preferred_element_type=f32)
               for (d, hv), vb in zip(chains, v_bf)]
        upd = [lax.dot_general(dirs[d][3][0, rows(d)[1], cols(hv)], vb, (((0,), (0,)), ((), ())),
                               preferred_element_type=f32)
               for (d, hv), vb in zip(chains, v_bf)]
        for n, (d, hv) in enumerate(chains):
            c, rs = rows(d)
            e = dirs[d][5][0, hv // 2, c, hv % 2:hv % 2 + 1, :]
            s_ref[d * DN_V_HEADS + hv] = s_old[n] * e + upd[n]
            dirs[d][6][rs, cols(hv)] = (q_s[n] + a_v[n]).astype(bf16)


def _dnscan(u, w, qd, kd, aqk, egl, nseq, seq):
    TB = DN_TB
    nt = seq // TB
    nch = TB // DN_CHUNK
    T = nseq * seq

    def blk(d):
        if d == 0:
            return lambda b, t: b * nt + t
        return lambda b, t: b * nt + nt - 1 - t

    def specs(d):
        wide = lambda: pl.BlockSpec((1, TB, DN_V_W), lambda b, t: (d, blk(d)(b, t), 0))
        return [wide(), wide(), wide(), wide(),
                pl.BlockSpec((1, TB, DN_V_HEADS * DN_CHUNK), lambda b, t: (d, blk(d)(b, t), 0)),
                pl.BlockSpec((1, DN_QK_HEADS, nch, 2, LANES), lambda b, t: (d, 0, blk(d)(b, t), 0, 0))]

    def ospec(d):
        return pl.BlockSpec((TB, DN_V_W), lambda b, t: (blk(d)(b, t), 0))

    args = (u, w, qd, kd, aqk, egl)
    return pl.pallas_call(
        _dnscan_kernel,
        grid=(nseq, nt),
        in_specs=specs(0) + specs(1),
        out_specs=[ospec(0), ospec(1)],
        out_shape=[jax.ShapeDtypeStruct((T, DN_V_W), bf16)] * 2,
        scratch_shapes=[pltpu.VMEM((2 * DN_V_HEADS, DN_K_DIM, DN_V_DIM), f32)],
        compiler_params=_cparams(("parallel", "arbitrary")),
        name="dn_scan",
    )(*args, *args)


def _post_kernel(xa_ref, xb_ref, oa_ref, of_ref, ob_ref, z_ref, ga_ref, gd_ref, dnw_ref, wa_ref, wd_ref,
                 wo_ref, n2_ref, rw_ref, rb_ref, x2_ref, hp_ref, idx_ref, gate_ref, *, npb):
    sub = POST_SUB
    subs = [slice(s * sub, (s + 1) * sub) for s in range(x2_ref.shape[0] // sub)]
    first_group = pl.program_id(0) < npb
    half = D_MODEL // 2

    odn = []
    for r in subs:
        od = of_ref[r, :].astype(f32) + ob_ref[r, :].astype(f32)
        z = z_ref[r, :].astype(f32)
        parts = []
        for h in range(DN_V_HEADS):
            cols = slice(h * DN_V_DIM, (h + 1) * DN_V_DIM)
            seg = od[:, cols]
            seg = seg * lax.rsqrt(jnp.mean(seg * seg, axis=-1, keepdims=True) + NORM_EPS) * dnw_ref[...]
            zz = z[:, cols]
            parts.append((seg * (zz * _sigmoid(zz))).astype(bf16))
        odn.append(jnp.concatenate(parts, axis=1))
    ya = [jnp.dot(oa_ref[r, :], wa_ref[...], preferred_element_type=f32) for r in subs]
    yd = [jnp.dot(o, wd_ref[...], preferred_element_type=f32) for o in odn]
    merged = [(_sigmoid(ga_ref[r, :].astype(f32)) * a + _sigmoid(gd_ref[r, :].astype(f32)) * d).astype(bf16)
              for r, a, d in zip(subs, ya, yd)]
    mo = [jnp.dot(m, wo_ref[...], preferred_element_type=f32) for m in merged]
    h2s = []
    for r, m in zip(subs, mo):
        x2 = jnp.where(first_group, xa_ref[r, :], xb_ref[r, :]) + m
        x2_ref[r, :] = x2
        h2s.append(x2 * lax.rsqrt(jnp.mean(x2 * x2, axis=-1, keepdims=True) + NORM_EPS) * n2_ref[...])
    logits = [lax.dot_general(rw_ref[...], h2, (((1,), (1,)), ((), ())), precision=lax.Precision.HIGHEST,
                              preferred_element_type=f32) + rb_ref[...] for h2 in h2s]
    for n, (r, h2) in enumerate(zip(subs, h2s)):
        packed = _pack_halves(h2[:, :half], h2[:, half:])
        for s in range(ROW_TILES):
            hp_ref[_row_tile(n * sub, sub, s), :] = packed[:, s * LANES:(s + 1) * LANES]
        cur = logits[n]
        erow = lax.broadcasted_iota(i32, cur.shape, 0).astype(f32)
        vals, idxs = [], []
        for _ in range(TOP_K):
            m = jnp.max(cur, axis=0, keepdims=True)
            idx = jnp.min(jnp.where(cur == m, erow, float(N_EXPERTS)), axis=0, keepdims=True)
            vals.append(m)
            idxs.append(idx)
            cur = jnp.where(erow == idx, -jnp.inf, cur)
        exps = [jnp.exp(v - vals[0]) for v in vals]
        rtot = 1.0 / (exps[0] + exps[1] + exps[2] + exps[3])
        out_row = lax.broadcasted_iota(i32, (SUBLANES, sub), 0)
        idx_out = jnp.zeros((SUBLANES, sub), f32)
        gate_out = jnp.zeros((SUBLANES, sub), f32)
        for kk in range(TOP_K):
            idx_out = jnp.where(out_row == kk, idxs[kk], idx_out)
            gate_out = jnp.where(out_row == kk, exps[kk] * rtot, gate_out)
        idx_ref[:, r] = idx_out.astype(i32)
        gate_ref[:, r] = gate_out


def _post(xa, xb, o_attn, o_f, o_b, proj, dnw, wa, wd, wo, n2, rw, rb):
    T = xa.shape[0] + xb.shape[0]
    tm = min(POST_TM, xa.shape[0], xb.shape[0])
    assert xa.shape[0] % tm == 0 and xb.shape[0] % tm == 0
    npb = xa.shape[0] // tm
    zcol = P_DNZ // DN_V_W
    const = lambda shape: pl.BlockSpec(shape, lambda i: (0, 0))
    row = lambda width, col=0: pl.BlockSpec((tm, width), lambda i: (i, col))
    return pl.pallas_call(
        functools.partial(_post_kernel, npb=npb),
        grid=(T // tm,),
        in_specs=_group_row_specs(tm, D_MODEL, npb) + [
            row(ATTN_Q_W), row(DN_V_W), row(DN_V_W),
            row(DN_V_W, zcol), row(D_MODEL, P_GA // D_MODEL), row(D_MODEL, P_GD // D_MODEL),
            const((1, DN_V_DIM)), const((ATTN_Q_W, D_MODEL)), const((DN_V_W, D_MODEL)),
            const((D_MODEL, D_MODEL)), const((1, D_MODEL)), const((N_EXPERTS, D_MODEL)),
            const((N_EXPERTS, 1)),
        ],
        out_specs=[row(D_MODEL), pl.BlockSpec((tm * ROW_TILES, LANES), lambda i: (i, 0)),
                   pl.BlockSpec((SUBLANES, tm), lambda i: (0, i)),
                   pl.BlockSpec((SUBLANES, tm), lambda i: (0, i))],
        out_shape=[
            jax.ShapeDtypeStruct((T, D_MODEL), f32),
            jax.ShapeDtypeStruct((T * ROW_TILES, LANES), u32),
            jax.ShapeDtypeStruct((SUBLANES, T), i32),
            jax.ShapeDtypeStruct((SUBLANES, T), f32),
        ],
        compiler_params=_cparams(("parallel",)),
        name="post_mix",
    )(xa, xb, o_attn, o_f, o_b, proj, proj, proj, dnw, wa, wd, wo, n2, rw, rb)


def _scatter_kernel(poff_ref, pn_ref, pos_ref, h_ref, xs_ref, zero_ref, sem, zsem):
    R = pos_ref.shape[2] // TOP_K
    nbits = MOE_TM.bit_length() - 1

    @pl.when(pl.program_id(0) == 0)
    def _():
        zero_ref[...] = jnp.zeros(zero_ref.shape, u32)

        def per_expert(e, carry):
            off = poff_ref[e]
            n = pn_ref[e]
            for bit in reversed(range(nbits)):
                size = 1 << bit
                take = ((n >> bit) & 1) == 1

                @pl.when(take)
                def _():
                    cp = pltpu.make_async_copy(zero_ref.at[_row_span(0, size)],
                                               xs_ref.at[_row_span(off, size)], zsem)
                    cp.start()
                    cp.wait()

                off = off + jnp.where(take, size, 0)
            return carry

        lax.fori_loop(0, N_EXPERTS, per_expert, 0)

        zrows = zero_ref.shape[0] // ROW_TILES

        def tail(j, carry):
            row = poff_ref[N_EXPERTS] + j * zrows
            cp = pltpu.make_async_copy(zero_ref, xs_ref.at[_row_span(row, zrows)], zsem)
            cp.start()
            cp.wait()
            return carry

        lax.fori_loop(0, pn_ref[N_EXPERTS], tail, 0)

    def body(t, carry):
        for k in range(TOP_K):
            p = pos_ref[0, 0, t * TOP_K + k]
            pltpu.make_async_copy(h_ref.at[_row_span(t)], xs_ref.at[_row_span(p)], sem).start()
        return carry

    lax.fori_loop(0, R, body, 0, unroll=2)
    done = xs_ref.at[_row_span(0, TOP_K * R)]
    pltpu.make_async_copy(done, done, sem).wait()


def _scatter_rows(pad_off, pad_n, pos, hp, n_slots):
    T = hp.shape[0] // ROW_TILES
    R = min(SCATTER_R, T)
    grid_spec = pltpu.PrefetchScalarGridSpec(
        num_scalar_prefetch=2,
        grid=(T // R,),
        in_specs=[
            pl.BlockSpec((1, 1, TOP_K * R), lambda i, po, pn: (i, 0, 0), memory_space=pltpu.SMEM),
            pl.BlockSpec((R * ROW_TILES, LANES), lambda i, po, pn: (i, 0)),
        ],
        out_specs=pl.BlockSpec(memory_space=pl.ANY),
        scratch_shapes=[pltpu.VMEM((MOE_TM // 2 * ROW_TILES, LANES), u32),
                        pltpu.SemaphoreType.DMA(()), pltpu.SemaphoreType.DMA(())],
    )
    return pl.pallas_call(
        _scatter_kernel,
        grid_spec=grid_spec,
        out_shape=jax.ShapeDtypeStruct((n_slots * ROW_TILES, LANES), u32),
        compiler_params=_cparams(("arbitrary",)),
        name="moe_scatter",
    )(pad_off, pad_n, pos.reshape(T // R, 1, TOP_K * R), hp)


def _moe_kernel(te_ref, nu_ref, tr_ref, x_ref, wg_ref, wu_ref, bg_ref, bu_ref, wd_ref, bd_ref,
                o_ref, xb_ref, acc_ref, *, nf):
    i = pl.program_id(0)
    f = pl.program_id(1)
    half = D_MODEL // 2
    tm = xb_ref.shape[0]
    used = i < nu_ref[0]
    half_tile = tr_ref[i] <= tm // 2

    @pl.when((i == 0) & (f == 0))
    def _():
        acc_ref[...] = jnp.zeros(acc_ref.shape, f32)

    @pl.when(used & (f == 0))
    def _():
        for s in range(ROW_TILES):
            lo, hi = _unpack_halves(x_ref[_row_tile(0, xb_ref.shape[0], s), :])
            xb_ref[:, s * LANES:(s + 1) * LANES] = lo.astype(bf16)
            xb_ref[:, half + s * LANES:half + (s + 1) * LANES] = hi.astype(bf16)

    def compute(rows):
        xb = xb_ref[0:rows, :]
        g = jnp.dot(xb, wg_ref[0, 0], preferred_element_type=f32) + bg_ref[0]
        u = jnp.dot(xb, wu_ref[0, 0], preferred_element_type=f32) + bu_ref[0]
        gate = jnp.minimum(g, SWIGLU_LIMIT)
        up = jnp.clip(u, -SWIGLU_LIMIT, SWIGLU_LIMIT)
        act = (up + 1.0) * (gate * _sigmoid(SWIGLU_ALPHA * gate))
        part = jnp.dot(act.astype(bf16), wd_ref[0].astype(bf16), preferred_element_type=f32)

        if nf == 1:
            total = part
        else:
            total = jnp.where(f == 0, part, acc_ref[0:rows, :] + part)
            acc_ref[0:rows, :] = total
        y = total + bd_ref[0]
        _store_row_tiles(o_ref, _pack_halves(y[:, :half], y[:, half:]))

    @pl.when(used & jnp.logical_not(half_tile))
    def _():
        compute(tm)

    @pl.when(used & half_tile)
    def _():
        compute(tm // 2)
        rest = tm // 2 * ROW_TILES
        o_ref[rest:, :] = jnp.zeros((rest, LANES), u32)

    @pl.when(jnp.logical_not(used) & (f == nf - 1))
    def _():
        o_ref[...] = jnp.zeros(o_ref.shape, u32)


def _moe(tile_expert, n_used, tile_rows, xs, wg, wu, bg, bu, wd, bd):
    n_slots = xs.shape[0] // ROW_TILES
    tm, tf = MOE_TM, MOE_TF
    n_tiles = n_slots // tm
    nf = D_FF // tf

    def fidx(i, f, nu):
        return jnp.where(i < nu[0], f, nf - 1)

    grid_spec = pltpu.PrefetchScalarGridSpec(
        num_scalar_prefetch=3,
        grid=(n_tiles, nf),
        in_specs=[
            pl.BlockSpec((tm * ROW_TILES, LANES), lambda i, f, te, nu, tr: (i, 0)),
            pl.BlockSpec((1, 1, D_MODEL, tf), lambda i, f, te, nu, tr: (te[i], fidx(i, f, nu), 0, 0)),
            pl.BlockSpec((1, 1, D_MODEL, tf), lambda i, f, te, nu, tr: (te[i], fidx(i, f, nu), 0, 0)),
            pl.BlockSpec((1, 1, tf), lambda i, f, te, nu, tr: (te[i], 0, fidx(i, f, nu))),
            pl.BlockSpec((1, 1, tf), lambda i, f, te, nu, tr: (te[i], 0, fidx(i, f, nu))),
            pl.BlockSpec((1, tf, D_MODEL), lambda i, f, te, nu, tr: (te[i], fidx(i, f, nu), 0)),
            pl.BlockSpec((1, 1, D_MODEL), lambda i, f, te, nu, tr: (te[i], 0, 0)),
        ],
        out_specs=pl.BlockSpec((tm * ROW_TILES, LANES), lambda i, f, te, nu, tr: (i, 0)),
        scratch_shapes=[pltpu.VMEM((tm, D_MODEL), bf16), pltpu.VMEM((tm, D_MODEL), f32)],
    )
    return pl.pallas_call(
        functools.partial(_moe_kernel, nf=nf),
        grid_spec=grid_spec,
        out_shape=jax.ShapeDtypeStruct((n_slots * ROW_TILES, LANES), u32),
        compiler_params=_cparams(("arbitrary", "arbitrary")),
        name="moe_experts",
    )(tile_expert, n_used, tile_rows, xs, wg, wu, bg, bu, wd, bd)


def _combine_kernel(pos_ref, posn_ref, x2_ref, gate_ref, fw_ref, ys_ref, oa_ref, ob_ref,
                    buf_ref, acc_ref, sem, *, npb):
    R = x2_ref.shape[0]
    half = D_MODEL // 2
    nrows = TOP_K * R
    i = pl.program_id(0)
    slot = i % 2

    def issue(p_ref, dst_slot):
        def body(a, carry):
            p = p_ref[0, 0, a]
            pltpu.make_async_copy(ys_ref.at[_row_span(p)],
                                  buf_ref.at[_row_span(dst_slot * nrows + a)], sem.at[dst_slot]).start()
            return carry

        lax.fori_loop(0, nrows, body, 0, unroll=8)

    @pl.when(i == 0)
    def _():
        issue(pos_ref, 0)

    @pl.when(i + 1 < pl.num_programs(0))
    def _():
        issue(posn_ref, 1 - slot)

    pltpu.make_async_copy(ys_ref.at[_row_span(0, nrows)], buf_ref.at[_row_span(slot * nrows, nrows)],
                          sem.at[slot]).wait()
    row0 = slot * nrows

    gates = gate_ref[...]
    gk = [gates[:, k:k + 1] for k in range(TOP_K)]
    ss = jnp.zeros((R, 1), f32)
    for s in range(ROW_TILES):
        cl = slice(s * LANES, (s + 1) * LANES)
        ch = slice(half + s * LANES, half + (s + 1) * LANES)
        lo_acc = x2_ref[:, cl]
        hi_acc = x2_ref[:, ch]
        for k in range(TOP_K):
            lo, hi = _unpack_halves(buf_ref[_row_tile(row0 + k * R, R, s), :])
            lo_acc = lo_acc + gk[k] * lo
            hi_acc = hi_acc + gk[k] * hi
        ss = ss + jnp.sum(lo_acc * lo_acc, axis=-1, keepdims=True) + jnp.sum(
            hi_acc * hi_acc, axis=-1, keepdims=True)
        acc_ref[:, cl] = lo_acc
        acc_ref[:, ch] = hi_acc
    inv = lax.rsqrt(ss / D_MODEL + NORM_EPS)

    @pl.when(i < npb)
    def _():
        oa_ref[...] = acc_ref[...] * inv * fw_ref[...]

    @pl.when(i >= npb)
    def _():
        ob_ref[...] = acc_ref[...] * inv * fw_ref[...]


def _combine(pos, x2, gates, fw, ys, rows_a):
    T = x2.shape[0]
    R = min(COMB_R, rows_a, T - rows_a)
    assert rows_a % R == 0 and T % R == 0
    npb = rows_a // R
    nsteps = T // R
    return pl.pallas_call(
        functools.partial(_combine_kernel, npb=npb),
        grid=(nsteps,),
        in_specs=[
            pl.BlockSpec((1, 1, TOP_K * R), lambda i: (i, 0, 0), memory_space=pltpu.SMEM),
            pl.BlockSpec((1, 1, TOP_K * R), lambda i: (jnp.minimum(i + 1, nsteps - 1), 0, 0),
                         memory_space=pltpu.SMEM),
            pl.BlockSpec((R, D_MODEL), lambda i: (i, 0)),
            pl.BlockSpec((R, TOP_K), lambda i: (i, 0)),
            pl.BlockSpec((1, D_MODEL), lambda i: (0, 0)),
            pl.BlockSpec(memory_space=pl.ANY),
        ],
        out_specs=_group_row_specs(R, D_MODEL, npb),
        out_shape=[jax.ShapeDtypeStruct((rows_a, D_MODEL), f32),
                   jax.ShapeDtypeStruct((T - rows_a, D_MODEL), f32)],
        scratch_shapes=[pltpu.VMEM((2 * TOP_K * R * ROW_TILES, LANES), u32),
                        pltpu.VMEM((R, D_MODEL), f32),
                        pltpu.SemaphoreType.DMA((2,))],
        compiler_params=_cparams(("arbitrary",)),
        name="moe_combine",
    )(pos, pos, x2, gates, fw, ys)


def _dn_channels(t):
    lead = t.shape[:-1]
    t = t.reshape(lead + (2, 2, DN_QK_HEADS, 2))
    perm = tuple(range(len(lead))) + tuple(len(lead) + a for a in (2, 0, 1, 3))
    return t.transpose(perm).reshape(lead + (4 * DN_V_HEADS,))


def _pad_lanes(t):
    pad = [(0, 0)] * (t.ndim - 1) + [(0, LANES - t.shape[-1])]
    return jnp.pad(t, pad)


def _deint_kernel(w_ref, p_ref, g_ref, u_ref):
    for c in range(w_ref.shape[2] // (2 * LANES)):
        w = w_ref[0, :, c * 2 * LANES:(c + 1) * 2 * LANES].astype(bf16)
        r = jnp.dot(w, p_ref[...], preferred_element_type=f32)
        g_ref[0, 0, :, c * LANES:(c + 1) * LANES] = r[:, :LANES].astype(bf16)
        u_ref[0, 0, :, c * LANES:(c + 1) * LANES] = r[:, LANES:].astype(bf16)


def _deinterleave(w):
    E, D, F2 = w.shape
    rb, cb = 1024, 2 * MOE_TF
    i = jnp.arange(2 * LANES)
    src = jnp.where(i < LANES, 2 * i, 2 * (i - LANES) + 1)
    perm = (jnp.arange(2 * LANES)[:, None] == src[None, :]).astype(bf16)
    out = jax.ShapeDtypeStruct((E, F2 // cb, D, MOE_TF), bf16)
    return pl.pallas_call(
        _deint_kernel,
        grid=(E, D // rb, F2 // cb),
        in_specs=[pl.BlockSpec((1, rb, cb), lambda e, r, c: (e, r, c)),
                  pl.BlockSpec((2 * LANES, 2 * LANES), lambda e, r, c: (0, 0))],
        out_specs=[pl.BlockSpec((1, 1, rb, MOE_TF), lambda e, r, c: (e, c, r, 0))] * 2,
        out_shape=[out, out],
        compiler_params=_cparams(("parallel", "parallel", "parallel")),
        name="moe_deinterleave",
    )(w, perm)


def _routing(top_idx, T, tm):
    A = T * TOP_K
    n_tiles = A // tm + N_EXPERTS
    n_slots = n_tiles * tm
    flat_e = top_idx.reshape(-1)
    onehot = (flat_e[:, None] == jnp.arange(N_EXPERTS, dtype=i32)[None, :]).astype(i32)
    csum = jnp.cumsum(onehot, axis=0)
    counts = csum[-1]
    padded = ((counts + tm - 1) // tm) * tm
    padded_end = jnp.cumsum(padded)
    padded_start = padded_end - padded
    pos = jnp.sum(onehot * (csum - 1 + padded_start[None, :]), axis=1).astype(i32)
    tile_start = jnp.arange(n_tiles, dtype=i32) * tm
    tile_expert = jnp.minimum(
        jnp.sum(padded_end[None, :] <= tile_start[:, None], axis=1), N_EXPERTS - 1).astype(i32)
    n_used = (padded_end[-1] // tm).astype(i32).reshape(1)
    is_e = tile_expert[:, None] == jnp.arange(N_EXPERTS, dtype=i32)[None, :]
    valid_end = jnp.sum(jnp.where(is_e, (padded_start + counts)[None, :], 0), axis=1)
    tile_rows = jnp.clip(valid_end - tile_start, 0, tm).astype(i32)
    pad_off = jnp.concatenate([padded_start + counts, padded_end[-1:]]).astype(i32)
    pad_n = jnp.concatenate([padded - counts, (n_slots - padded_end[-1:]) // (tm // 2)]).astype(i32)
    return pos, pad_off, pad_n, tile_expert, n_used, tile_rows, n_slots


def _layer(xa, xb, seq, norm1_w, w_in, b_in, attn_sinks, dn_conv_w, dn_a_log, dn_dt_bias,
           dn_norm_w, w_attn_o, w_dn_o, w_out, norm2_w, router_w, router_b, w_gate_up, b_gate_up,
           w_down, b_down, final_norm_w):
    T = xa.shape[0] + xb.shape[0]
    nseq = T // seq

    order = ([_OFF_GA, D_MODEL], [_OFF_GD, D_MODEL], [_OFF_DNQKV, DN_CONV_W], [_OFF_AQ, ATTN_Q_W],
             [_OFF_DNZ, DN_V_W], [_OFF_AK, ATTN_KV_W], [_OFF_AV, ATTN_KV_W])
    w_main = jnp.concatenate([w_in[:, o:o + n] for o, n in order], axis=1).astype(bf16)
    w_main = w_main.reshape(D_MODEL, P_W // IN_TN, IN_TN).transpose(1, 0, 2)
    b_main = jnp.concatenate([b_in[o:o + n] for o, n in order])[None, :]
    nsm = 4 * DN_V_HEADS
    w_small = _pad_lanes(_dn_channels(w_in[:, _OFF_DNB:_OFF_DNB + nsm])).astype(bf16)
    b_small = _pad_lanes(_dn_channels(b_in[_OFF_DNB:_OFF_DNB + nsm]))
    zeros_b = jnp.zeros((2 * DN_V_HEADS,), f32)
    a_ch = _pad_lanes(_dn_channels(jnp.concatenate([zeros_b, dn_a_log.reshape(-1)])))
    d_ch = _pad_lanes(_dn_channels(jnp.concatenate([zeros_b, dn_dt_bias.reshape(-1)])))

    proj, small, small_t = _inproj(xa, xb, norm1_w[None, :], w_main, b_main, w_small, b_small[None, :],
                                   w_small.T, b_small[:, None])

    pos = jnp.arange(seq, dtype=f32)
    inv_freq = 1.0 / (ROPE_THETA ** (jnp.arange(0, ROPE_DIM, 2, dtype=f32) / ROPE_DIM))
    ang = pos[:, None] * inv_freq[None, :]
    cos, sin = jnp.cos(ang), jnp.sin(ang)
    half = ROPE_DIM // 2
    ones_tail = jnp.ones((seq, ATTN_HEAD_DIM - ROPE_DIM), f32)
    cos_head = jnp.concatenate([cos, cos, ones_tail], axis=1)
    sin_head = jnp.concatenate([-sin, sin, 0.0 * ones_tail], axis=1)
    cos_t = jnp.concatenate([cos_head, cos_head], axis=1)
    sin_t = jnp.concatenate([sin_head, sin_head], axis=1)

    o_attn = _attention(proj, attn_sinks, cos_t, sin_t, nseq, seq)

    qkv = _dnconv(proj, dn_conv_w, nseq, seq)
    u, w, qd, kd, aqk, egl = _dnprep(qkv, small, small_t, a_ch[None, :], d_ch[None, :],
                                     a_ch[:, None], d_ch[:, None])
    o_f, o_b = _dnscan(u, w, qd, kd, aqk, egl, nseq, seq)

    x2, hp, top_idx_t, top_gate_t = _post(
        xa, xb, o_attn, o_f, o_b, proj, dn_norm_w[None, :], w_attn_o.astype(bf16), w_dn_o.astype(bf16),
        w_out.astype(bf16), norm2_w[None, :], router_w.T, router_b[:, None])
    top_idx = top_idx_t[:TOP_K].T
    top_gate = top_gate_t[:TOP_K].T

    pos_a, pad_off, pad_n, tile_expert, n_used, tile_rows, n_slots = _routing(top_idx, T, MOE_TM)
    xs = _scatter_rows(pad_off, pad_n, pos_a, hp, n_slots)
    wg, wu = _deinterleave(w_gate_up)
    b_pairs = b_gate_up.reshape(N_EXPERTS, 1, D_FF, 2)
    bg, bu = b_pairs[..., 0], b_pairs[..., 1]
    ys = _moe(tile_expert, n_used, tile_rows, xs, wg, wu, bg, bu, w_down, b_down[:, None, :])
    R = min(COMB_R, xa.shape[0], xb.shape[0])
    pos_tiles = pos_a.reshape(T // R, R, TOP_K).transpose(0, 2, 1).reshape(T // R, 1, TOP_K * R)
    return _combine(pos_tiles, x2, top_gate, final_norm_w[None, :], ys, xa.shape[0])


def kernel(x_prompt, x_sample, norm1_w, w_in, b_in, attn_sinks, dn_conv_w, dn_a_log, dn_dt_bias,
           dn_norm_w, w_attn_o, w_dn_o, w_out, norm2_w, router_w, router_b, w_gate_up, b_gate_up,
           w_down, b_down, final_norm_w):
    bp, seq, d = x_prompt.shape
    bs = x_sample.shape[0]
    assert x_sample.shape[1] == seq and d == D_MODEL
    ya, yb = _layer(x_prompt.reshape(bp * seq, d), x_sample.reshape(bs * seq, d), seq,
                    norm1_w[0], w_in[0], b_in[0], attn_sinks[0], dn_conv_w[0], dn_a_log[0],
                    dn_dt_bias[0], dn_norm_w[0], w_attn_o[0], w_dn_o[0], w_out[0], norm2_w[0],
                    router_w[0], router_b[0], w_gate_up[0], b_gate_up[0], w_down[0], b_down[0],
                    final_norm_w)
    return (ya.reshape(bp, seq, d), yb.reshape(bs, seq, d))
```
